```python
import math
import jax, jax.numpy as jnp
from jax import lax
import numpy as np

D_MODEL = 1024
BATCH = 8
SEQ = 4096
DEPTH = 4

HEAD_DIM = 64
A_Q_HEADS = 4
A_KV_HEADS = 2
A_RADIUS = 128
B_WIDTH = 256
SHORT_CONV = 3
C_HEADS = 4
C_Q_RANK = 256
C_KV_RANK = 128
C_NOPE_DIM = 64
C_ROPE_DIM = 32
C_V_DIM = 64
C_Q_BLOCK = 128
ROPE_THETA = 10000.0
D_HEADS = 4
D_PATTERNS = ((128, 1), (512, 4), (2048, 16))
REL_BUCKETS = 32
REL_MAX_DISTANCE = 1024
N_BIAS_HEADS = A_Q_HEADS + D_HEADS * len(D_PATTERNS)
MIX_WIDTH = A_Q_HEADS * HEAD_DIM + B_WIDTH + C_HEADS * C_V_DIM + D_HEADS * HEAD_DIM
IN_SPLITS = (A_Q_HEADS * HEAD_DIM, A_KV_HEADS * HEAD_DIM, A_KV_HEADS * HEAD_DIM,
             B_WIDTH, B_WIDTH, B_WIDTH,
             C_Q_RANK, C_KV_RANK, C_ROPE_DIM) + (D_HEADS * HEAD_DIM,) * (3 * len(D_PATTERNS))
IN_WIDTH = sum(IN_SPLITS)
D_FF = ((8 * D_MODEL // 3 + 127) // 128) * 128
FFN_CONV = 3
EPS = 1e-6
NEG = -1e30

kernel_name = 'hybrid_parallel_mixer_encoder'


def rms_norm(x, g):
    x32 = x.astype(jnp.float32)
    y = x32 * lax.rsqrt(jnp.mean(x32 * x32, axis=-1, keepdims=True) + EPS)
    return (y * g.astype(jnp.float32)).astype(x.dtype)


def dwconv3(x, w):
    xp = jnp.pad(x, ((0, 0), (1, 1), (0, 0)))
    return xp[:, :-2] * w[0] + xp[:, 1:-1] * w[1] + xp[:, 2:] * w[2]


def t5_bucket(rel):
    half = REL_BUCKETS // 2
    max_exact = half // 2
    n = jnp.abs(rel)
    n_f = jnp.maximum(n, 1).astype(jnp.float32)
    large = max_exact + (jnp.log(n_f / max_exact) / math.log(REL_MAX_DISTANCE / max_exact)
                         * (half - max_exact)).astype(jnp.int32)
    large = jnp.minimum(large, half - 1)
    return jnp.where(rel > 0, half, 0) + jnp.where(n < max_exact, n, large)


def band_bias(table, radius, stride):
    qi = jnp.arange(radius)[:, None]
    kj = jnp.arange(3 * radius)[None, :]
    rel = (kj - radius - qi) * stride
    return jnp.transpose(table[t5_bucket(rel)], (2, 0, 1))


def banded_attention(q, k, v, bias, sink, return_lse):
    n, L, H, dh = q.shape
    hk = k.shape[2]
    g = H // hk
    R = bias.shape[1]
    nb = -(-L // R)
    Lp = nb * R
    qb = jnp.pad(q, ((0, 0), (0, Lp - L), (0, 0), (0, 0))).reshape(n, nb, R, hk, g, dh)

    def windows(t):
        tp = jnp.pad(t, ((0, 0), (R, Lp - L + R), (0, 0), (0, 0))).reshape(n, nb + 2, R, hk, dh)
        return jnp.concatenate([tp[:, :-2], tp[:, 1:-1], tp[:, 2:]], axis=2)

    kw, vw = windows(k), windows(v)
    s = jnp.einsum('nbqhgd,nbjhd->nbhgqj', qb, kw, preferred_element_type=jnp.float32) * (dh ** -0.5)
    s = s + bias.astype(jnp.float32).reshape(hk, g, R, 3 * R)
    qpos = jnp.arange(Lp).reshape(nb, R, 1)
    kpos = qpos[:, :1] - R + jnp.arange(3 * R)
    valid = (jnp.abs(kpos - qpos) <= R) & (kpos >= 0) & (kpos < L)
    s = jnp.where(valid[None, :, None, None], s, NEG)
    m = jnp.max(s, axis=-1)
    if sink is not None:
        sk = sink.astype(jnp.float32).reshape(1, 1, hk, g, 1)
        m = jnp.maximum(m, sk)
    p = jnp.exp(s - m[..., None])
    den = jnp.sum(p, axis=-1)
    if sink is not None:
        den = den + jnp.exp(sk - m)
    o = jnp.einsum('nbhgqj,nbjhd->nbqhgd', p, vw.astype(jnp.float32)) / jnp.moveaxis(den, -1, 2)[..., None]
    o = o.reshape(n, Lp, H, dh)[:, :L].astype(q.dtype)
    if not return_lse:
        return o
    lse = jnp.moveaxis(m + jnp.log(den), -1, 2).reshape(n, Lp, H)[:, :L]
    return o, lse


def to_strided(t, d):
    b, s, h, dh = t.shape
    return jnp.swapaxes(t.reshape(b, s // d, d, h, dh), 1, 2).reshape(b * d, s // d, h, dh)


def from_strided(t, batch, d):
    L = t.shape[1]
    rest = t.shape[2:]
    return jnp.swapaxes(t.reshape((batch, d, L) + rest), 1, 2).reshape((batch, L * d) + rest)


def rope(t, cos, sin):
    t1, t2 = jnp.split(t, 2, axis=-1)
    cos = cos.astype(t.dtype)
    sin = sin.astype(t.dtype)
    return jnp.concatenate([t1 * cos - t2 * sin, t2 * cos + t1 * sin], axis=-1)


def mla_attention(q_nope, q_rope, k_nope, k_rope, v):
    b, s, h, _ = q_nope.shape
    nq = s // C_Q_BLOCK
    scale = (C_NOPE_DIM + C_ROPE_DIM) ** -0.5

    def block(args):
        qn, qr = args
        sc = (jnp.einsum('bqhd,bkhd->bhqk', qn, k_nope, preferred_element_type=jnp.float32)
              + jnp.einsum('bqhr,bkr->bhqk', qr, k_rope, preferred_element_type=jnp.float32)) * scale
        p = jax.nn.softmax(sc, axis=-1)
        return jnp.einsum('bhqk,bkhd->bqhd', p.astype(v.dtype), v)

    qn_b = jnp.swapaxes(q_nope.reshape(b, nq, C_Q_BLOCK, h, C_NOPE_DIM), 0, 1)
    qr_b = jnp.swapaxes(q_rope.reshape(b, nq, C_Q_BLOCK, h, C_ROPE_DIM), 0, 1)
    o = lax.map(block, (qn_b, qr_b))
    return jnp.swapaxes(o, 0, 1).reshape(b, s, h * C_V_DIM)


def setup_inputs(seed: int = 0) -> dict:
    key = jax.random.key(seed)
    ks = jax.random.split(key, 18)
    f32 = jnp.float32

    def nrm(k, shape, scale):
        return jax.random.normal(k, shape, f32) * scale

    L = DEPTH
    return {
        'x': nrm(ks[0], (BATCH, SEQ, D_MODEL), 1.0),
        'c': nrm(ks[1], (BATCH, D_MODEL), 1.0),
        'positions': jnp.tile(jnp.arange(SEQ, dtype=jnp.int32)[None, :], (BATCH, 1)),
        'rel_bias': nrm(ks[2], (REL_BUCKETS, N_BIAS_HEADS), 0.5),
        'w_mod': nrm(ks[3], (L, D_MODEL, 6 * D_MODEL), 0.5 * D_MODEL ** -0.5),
        'b_mod': nrm(ks[4], (L, 6 * D_MODEL), 0.01),
        'norm_g': 1.0 + nrm(ks[5], (L, 4, D_MODEL), 0.1),
        'w_in': nrm(ks[6], (L, D_MODEL, IN_WIDTH), D_MODEL ** -0.5),
        'a_sink': nrm(ks[7], (L, A_Q_HEADS), 0.5),
        'b_conv': nrm(ks[8], (L, SHORT_CONV, B_WIDTH), SHORT_CONV ** -0.5),
        'c_norm_q': 1.0 + nrm(ks[9], (L, C_Q_RANK), 0.1),
        'c_norm_kv': 1.0 + nrm(ks[10], (L, C_KV_RANK), 0.1),
        'c_w_uq': nrm(ks[11], (L, C_Q_RANK, C_HEADS * (C_NOPE_DIM + C_ROPE_DIM)), C_Q_RANK ** -0.5),
        'c_w_ukv': nrm(ks[12], (L, C_KV_RANK, C_HEADS * (C_NOPE_DIM + C_V_DIM)), C_KV_RANK ** -0.5),
        'w_out': nrm(ks[13], (L, MIX_WIDTH, D_MODEL), MIX_WIDTH ** -0.5),
        'w_up': nrm(ks[14], (L, D_MODEL, 2 * D_FF), D_MODEL ** -0.5),
        'ffn_conv': nrm(ks[15], (L, FFN_CONV, 2 * D_FF), FFN_CONV ** -0.5),
        'w_down': nrm(ks[16], (L, D_FF, D_MODEL), D_FF ** -0.5),
    }


def reference(x, c, positions, rel_bias, w_mod, b_mod, norm_g, w_in, a_sink, b_conv, c_norm_q, c_norm_kv,
              c_w_uq, c_w_ukv, w_out, w_up, ffn_conv, w_down):
    b, s, _ = x.shape
    bias_a = band_bias(rel_bias[:, :A_Q_HEADS], A_RADIUS, 1)
    bias_d = [band_bias(rel_bias[:, A_Q_HEADS + i * D_HEADS:A_Q_HEADS + (i + 1) * D_HEADS], (w // 2) // d, d)
              for i, (w, d) in enumerate(D_PATTERNS)]
    half = C_ROPE_DIM // 2
    inv_freq = ROPE_THETA ** (-jnp.arange(half, dtype=jnp.float32) / half)
    ang = positions.astype(jnp.float32)[..., None] * inv_freq
    cos, sin = jnp.cos(ang), jnp.sin(ang)
    split_idx = [int(i) for i in np.cumsum(IN_SPLITS)[:-1]]
    c_act = jax.nn.silu(c)

    for l in range(DEPTH):
        mod = (c_act @ w_mod[l] + b_mod[l])[:, None, :]
        sh1, sc1, g1, sh2, sc2, g2 = jnp.split(mod, 6, axis=-1)

        h = rms_norm(x, norm_g[l, 0]) * (1 + sc1) + sh1
        parts = jnp.split(h @ w_in[l], split_idx, axis=-1)
        aq, ak, av, bb, bc, bh, cq, ckv, ckr = parts[:9]
        dqkv = parts[9:]

        oa = banded_attention(aq.reshape(b, s, A_Q_HEADS, HEAD_DIM), ak.reshape(b, s, A_KV_HEADS, HEAD_DIM),
                              av.reshape(b, s, A_KV_HEADS, HEAD_DIM), bias_a, a_sink[l], False).reshape(b, s, -1)

        ob = bb * dwconv3(bc * bh, b_conv[l])

        q = (rms_norm(cq, c_norm_q[l]) @ c_w_uq[l]).reshape(b, s, C_HEADS, C_NOPE_DIM + C_ROPE_DIM)
        kv = (rms_norm(ckv, c_norm_kv[l]) @ c_w_ukv[l]).reshape(b, s, C_HEADS, C_NOPE_DIM + C_V_DIM)
        q_rope = rope(q[..., C_NOPE_DIM:], cos[:, :, None, :], sin[:, :, None, :])
        k_rope = rope(ckr, cos, sin)
        oc = mla_attention(q[..., :C_NOPE_DIM], q_rope, kv[..., :C_NOPE_DIM], k_rope, kv[..., C_NOPE_DIM:])

        outs, lses = [], []
        for i, (w, d) in enumerate(D_PATTERNS):
            qd, kd, vd = (to_strided(t.reshape(b, s, D_HEADS, HEAD_DIM), d) for t in dqkv[3 * i:3 * i + 3])
            o, lse = banded_attention(qd, kd, vd, bias_d[i], None, True)
            outs.append(from_strided(o, b, d))
            lses.append(from_strided(lse, b, d))
        wts = jax.nn.softmax(jnp.stack(lses), axis=0)
        od = jnp.sum(jnp.stack(outs).astype(jnp.float32) * wts[..., None], axis=0).astype(x.dtype).reshape(b, s, -1)

        y = jnp.concatenate([oa, ob, oc, od], axis=-1) @ w_out[l]
        x = x + g1 * rms_norm(y, norm_g[l, 1])

        h = rms_norm(x, norm_g[l, 2]) * (1 + sc2) + sh2
        u = dwconv3(h @ w_up[l], ffn_conv[l])
        ug, uv = jnp.split(u, 2, axis=-1)
        y = (jax.nn.gelu(ug, approximate=True) * uv) @ w_down[l]
        x = x + g2 * rms_norm(y, norm_g[l, 3])
    return x
```

```python
import functools
import math

import jax
import jax.numpy as jnp
from jax import lax
from jax.experimental import pallas as pl
from jax.experimental.pallas import tpu as pltpu

D_MODEL = 1024
HEAD_DIM = 64
A_Q_HEADS = 4
A_KV_HEADS = 2
A_RADIUS = 128
B_WIDTH = 256
C_HEADS = 4
C_Q_RANK = 256
C_KV_RANK = 128
C_NOPE_DIM = 64
C_ROPE_DIM = 32
C_V_DIM = 64
ROPE_THETA = 10000.0
D_HEADS = 4
D_PATTERNS = ((128, 1), (512, 4), (2048, 16))
REL_BUCKETS = 32
REL_MAX_DISTANCE = 1024
D_FF = 2816
EPS = 1e-6
NEG = -1e30

LANES = 128
BF16_ROWS = 16
SUB_Q = 128
VMEM_LIMIT = 48 * 1024 * 1024

IN_COLS = 4096
OFF_AQ, OFF_AK, OFF_AV = 0, 256, 384
OFF_BB, OFF_BC, OFF_BH = 512, 768, 1024
OFF_CQ, OFF_CKV, OFF_CKR = 1280, 1536, 1664
OFF_D = 1792

F32 = jnp.float32
BF16 = jnp.bfloat16


def _cparams(sem):
    return pltpu.CompilerParams(dimension_semantics=sem, vmem_limit_bytes=VMEM_LIMIT)


def _rms(v, g):
    return v * lax.rsqrt(jnp.mean(v * v, axis=-1, keepdims=True) + EPS) * g


def _mod_body(c_ref, w_ref, b_ref, o_ref):
    c = c_ref[...]
    act = c / (1.0 + jnp.exp(-c))
    o_ref[...] = jnp.dot(act, w_ref[...], preferred_element_type=F32,
                         precision=lax.Precision.HIGHEST) + b_ref[...]


def _modulation(c, w_mod, b_mod):
    nl, d, _ = w_mod.shape
    b = c.shape[0]
    return pl.pallas_call(
        _mod_body,
        grid=(nl, 6),
        in_specs=[pl.BlockSpec((b, d), lambda l, k: (0, 0)),
                  pl.BlockSpec((None, d, d), lambda l, k: (l, 0, k)),
                  pl.BlockSpec((None, 1, d), lambda l, k: (l, 0, k))],
        out_specs=pl.BlockSpec((None, None, b, d), lambda l, k: (l, k, 0, 0)),
        out_shape=jax.ShapeDtypeStruct((nl, 6, b, d), F32),
        compiler_params=_cparams(("parallel", "parallel")),
        name="modulation",
    )(c, w_mod, b_mod.reshape(nl, 1, 6 * d))


def _inproj_body(x_ref, g_ref, sc_ref, sh_ref, w_ref, o_ref, h_ref):
    @pl.when(pl.program_id(2) == 0)
    def _():
        h = _rms(x_ref[...], g_ref[...]) * (1.0 + sc_ref[...]) + sh_ref[...]
        h_ref[...] = h.astype(BF16)

    o_ref[...] = jnp.dot(h_ref[...], w_ref[...], preferred_element_type=F32).astype(BF16)


def _inproj(x, g, sc, sh, w, tm=1024, tn=1024):
    b, s, d = x.shape
    n = w.shape[1]
    return pl.pallas_call(
        _inproj_body,
        grid=(b, s // tm, n // tn),
        in_specs=[pl.BlockSpec((None, tm, d), lambda bi, i, j: (bi, i, 0)),
                  pl.BlockSpec((1, d), lambda bi, i, j: (0, 0)),
                  pl.BlockSpec((None, 1, d), lambda bi, i, j: (bi, 0, 0)),
                  pl.BlockSpec((None, 1, d), lambda bi, i, j: (bi, 0, 0)),
                  pl.BlockSpec((d, tn), lambda bi, i, j: (0, j))],
        out_specs=pl.BlockSpec((None, tm, tn), lambda bi, i, j: (bi, i, j)),
        out_shape=jax.ShapeDtypeStruct((b, s, n), BF16),
        scratch_shapes=[pltpu.VMEM((tm, d), BF16)],
        compiler_params=_cparams(("parallel", "parallel", "arbitrary")),
        name="inproj",
    )(x, g, sc, sh, w)


def _t5_bucket(rel):
    half = REL_BUCKETS // 2
    max_exact = half // 2
    n = jnp.abs(rel)
    n_f = jnp.maximum(n, 1).astype(jnp.float32)
    large = max_exact + (jnp.log(n_f / max_exact) / math.log(REL_MAX_DISTANCE / max_exact)
                         * (half - max_exact)).astype(jnp.int32)
    large = jnp.minimum(large, half - 1)
    return jnp.where(rel > 0, half, 0) + jnp.where(n < max_exact, n, large)


def _band_bias(table, radius, stride):
    qi = jnp.arange(SUB_Q)[:, None]
    kj = jnp.arange(SUB_Q + 2 * radius)[None, :]
    off = kj - radius - qi
    bias = jnp.transpose(table[_t5_bucket(off * stride)], (2, 0, 1)).astype(F32)
    return jnp.where((jnp.abs(off) <= radius)[None], bias, NEG)


def _band_body(*refs, radius, tq, seq, kv_w, heads, has_sink, has_lse):
    q_ref, kp_ref, kc_ref, kn_ref, vp_ref, vc_ref, vn_ref, bias_ref = refs[:8]
    rest = refs[8:]
    if has_sink:
        sink_ref, rest = rest[0], rest[1:]
    o_ref = rest[0]
    lse_ref = rest[1] if has_lse else None

    i = pl.program_id(2)
    nk = SUB_Q + 2 * radius
    kcat = jnp.concatenate([kp_ref[...], kc_ref[...], kn_ref[...]], axis=0)
    vcat = jnp.concatenate([vp_ref[...], vc_ref[...], vn_ref[...]], axis=0)
    lo = lax.broadcasted_iota(jnp.int32, (1, LANES), 1) < HEAD_DIM
    scale = HEAD_DIM ** -0.5

    for sb in range(tq // SUB_Q):
        r0 = sb * SUB_Q
        qs = q_ref[r0:r0 + SUB_Q, :]
        ks = kcat[r0:r0 + nk]
        vs = vcat[r0:r0 + nk]
        kpos = i * tq + (r0 - radius) + lax.broadcasted_iota(jnp.int32, (1, nk), 1)
        inb = (kpos >= 0) & (kpos < seq)
        for g in range(2):
            qg = qs[:, g * LANES:(g + 1) * LANES] * scale
            kg = ks if kv_w == LANES else ks[:, g * LANES:(g + 1) * LANES]
            vg = vs if kv_w == LANES else vs[:, g * LANES:(g + 1) * LANES]
            acc = None
            lacc = None
            for half in range(2):
                head = heads[g][half]
                hm = lo if half == 0 else jnp.logical_not(lo)
                qm = jnp.where(hm, qg, jnp.zeros_like(qg))
                sc = lax.dot_general(qm, kg, (((1,), (1,)), ((), ())), preferred_element_type=F32)
                sc = jnp.where(inb, sc + bias_ref[head], NEG)
                m = jnp.max(sc, axis=-1, keepdims=True)
                if has_sink:
                    m = jnp.maximum(m, sink_ref[head])
                p = jnp.exp(sc - m)
                den = jnp.sum(p, axis=-1, keepdims=True)
                if has_sink:
                    den = den + jnp.exp(sink_ref[head] - m)
                o = jnp.dot(p.astype(BF16), vg, preferred_element_type=F32) / den
                acc = o if acc is None else jnp.where(hm, o, acc)
                if has_lse:
                    ls = jnp.broadcast_to(m + jnp.log(den), (SUB_Q, LANES))
                    lacc = ls if lacc is None else jnp.where(hm, ls, lacc)
            o_ref[r0:r0 + SUB_Q, g * LANES:(g + 1) * LANES] = acc.astype(BF16)
            if has_lse:
                lse_ref[r0:r0 + SUB_Q, g * LANES:(g + 1) * LANES] = lacc


def _band_attention(qkv, bias, sink, *, dil, radius, tq, q_off, k_off, v_off, kv_w, heads, has_lse):
    b, s, n = qkv.shape
    seq = s // dil
    tq = min(tq, seq)
    view = qkv.reshape(b, seq, dil * n)
    qw = 2 * LANES
    per_q, per_kv = n // qw, n // kv_w
    steps = tq // radius
    last = seq // radius - 1

    def cur(off, per):
        return lambda bi, r, i: (bi, i, r * per + off)

    def prev(off, per):
        return lambda bi, r, i: (bi, jnp.maximum(i * steps - 1, 0), r * per + off)

    def nxt(off, per):
        return lambda bi, r, i: (bi, jnp.minimum((i + 1) * steps, last), r * per + off)

    in_specs = [pl.BlockSpec((None, tq, qw), cur(q_off // qw, per_q))]
    for off in (k_off, v_off):
        in_specs += [pl.BlockSpec((None, radius, kv_w), prev(off // kv_w, per_kv)),
                     pl.BlockSpec((None, tq, kv_w), cur(off // kv_w, per_kv)),
                     pl.BlockSpec((None, radius, kv_w), nxt(off // kv_w, per_kv))]
    in_specs.append(pl.BlockSpec(bias.shape, lambda bi, r, i: (0, 0, 0)))
    args = [view] * 7 + [bias]
    if sink is not None:
        in_specs.append(pl.BlockSpec(memory_space=pltpu.SMEM))
        args.append(sink)

    o_spec = pl.BlockSpec((None, tq, qw), lambda bi, r, i: (bi, i, r))
    out_shape = [jax.ShapeDtypeStruct((b, seq, dil * qw), BF16)]
    out_specs = [o_spec]
    if has_lse:
        out_shape.append(jax.ShapeDtypeStruct((b, seq, dil * qw), F32))
        out_specs.append(o_spec)

    outs = pl.pallas_call(
        functools.partial(_band_body, radius=radius, tq=tq, seq=seq, kv_w=kv_w, heads=heads,
                          has_sink=sink is not None, has_lse=has_lse),
        grid=(b, dil, seq // tq),
        in_specs=in_specs,
        out_specs=out_specs,
        out_shape=out_shape,
        compiler_params=_cparams(("parallel", "parallel", "parallel")),
        name=f"band_attention_d{dil}_r{radius}",
    )(*args)
    return [o.reshape(b, s, qw) for o in outs]


def _mla_prep_body(cq_ref, ckv_ref, ckr_ref, cos_ref, sin_ref, gq_ref, gkv_ref, wq_ref, wkv_ref, swap_ref,
                   q_ref, k_ref, v_ref):
    cs = cos_ref[...]
    sn = sin_ref[...]
    scale = (C_NOPE_DIM + C_ROPE_DIM) ** -0.5
    nq = _rms(cq_ref[...].astype(F32), gq_ref[...]).astype(BF16)
    qq = jnp.dot(nq, wq_ref[...], preferred_element_type=F32)
    nkv = _rms(ckv_ref[...].astype(F32), gkv_ref[...]).astype(BF16)
    kvv = jnp.dot(nkv, wkv_ref[...], preferred_element_type=F32)
    kr = ckr_ref[...]
    kr_sw = jnp.dot(kr, swap_ref[...], preferred_element_type=F32)
    k_rope = kr.astype(F32) * cs + kr_sw * sn
    hw = C_HEADS * LANES
    for h in range(C_HEADS):
        a = qq[:, h * LANES:(h + 1) * LANES]
        a_sw = qq[:, hw + h * LANES:hw + (h + 1) * LANES]
        q_ref[:, h * LANES:(h + 1) * LANES] = ((a * cs + a_sw * sn) * scale).astype(BF16)
        k_ref[:, h * LANES:(h + 1) * LANES] = (kvv[:, h * LANES:(h + 1) * LANES] + k_rope).astype(BF16)
    v_ref[...] = kvv[:, hw:].astype(BF16)


def _mla_prep(qkv, cos_t, sin_t, gq, gkv, wq, wkv, swap, tm=512):
    b, s, n = qkv.shape
    hw = C_HEADS * LANES
    vw = C_HEADS * C_V_DIM

    def const(shape):
        return pl.BlockSpec(shape, lambda bi, i: (0,) * len(shape))

    def rows(width, off=0):
        return pl.BlockSpec((None, tm, width), lambda bi, i: (bi, i, off // width))

    return pl.pallas_call(
        _mla_prep_body,
        grid=(b, s // tm),
        in_specs=[rows(C_Q_RANK, OFF_CQ), rows(LANES, OFF_CKV), rows(LANES, OFF_CKR),
                  rows(LANES), rows(LANES),
                  const(gq.shape), const(gkv.shape), const(wq.shape), const(wkv.shape), const(swap.shape)],
        out_specs=[rows(hw), rows(hw), rows(vw)],
        out_shape=[jax.ShapeDtypeStruct((b, s, hw), BF16), jax.ShapeDtypeStruct((b, s, hw), BF16),
                   jax.ShapeDtypeStruct((b, s, vw), BF16)],
        compiler_params=_cparams(("parallel", "parallel")),
        name="mla_prep",
    )(qkv, qkv, qkv, cos_t, sin_t, gq, gkv, wq, wkv, swap)


def _mla_attn_body(q_ref, k_ref, v_ref, o_ref):
    lo = lax.broadcasted_iota(jnp.int32, (1, LANES), 1) < C_V_DIM
    for g in range(C_HEADS // 2):
        acc = None
        for half in range(2):
            h = 2 * g + half
            q = q_ref[:, h * LANES:(h + 1) * LANES]
            k = k_ref[:, h * LANES:(h + 1) * LANES]
            sc = lax.dot_general(q, k, (((1,), (1,)), ((), ())), preferred_element_type=F32)
            m = jnp.max(sc, axis=-1, keepdims=True)
            p = jnp.exp(sc - m)
            den = jnp.sum(p, axis=-1, keepdims=True)
            o = jnp.dot(p.astype(BF16), v_ref[:, g * LANES:(g + 1) * LANES], preferred_element_type=F32) / den
            acc = o if acc is None else jnp.where(lo, acc, o)
        o_ref[:, g * LANES:(g + 1) * LANES] = acc.astype(BF16)


def _mla_attn(q, k, v, tq=256):
    b, s, hw = q.shape
    vw = v.shape[-1]
    return pl.pallas_call(
        _mla_attn_body,
        grid=(b, s // tq),
        in_specs=[pl.BlockSpec((None, tq, hw), lambda bi, i: (bi, i, 0)),
                  pl.BlockSpec((None, s, hw), lambda bi, i: (bi, 0, 0)),
                  pl.BlockSpec((None, s, vw), lambda bi, i: (bi, 0, 0))],
        out_specs=pl.BlockSpec((None, tq, vw), lambda bi, i: (bi, i, 0)),
        out_shape=jax.ShapeDtypeStruct((b, s, vw), BF16),
        compiler_params=_cparams(("parallel", "arbitrary")),
        name="mla_attention",
    )(q, k, v)


def _outproj_body(x_ref, oa_ref, bb_ref, bc_ref, bh_ref, bcp_ref, bhp_ref, bcn_ref, bhn_ref, oc_ref,
                  o0_ref, o1_ref, o2_ref, l0_ref, l1_ref, l2_ref, w_ref, cw_ref, gn_ref, gate_ref,
                  out_ref, conv_ref, *, tm):
    i = pl.program_id(1)
    pad = 8
    first = (i > 0).astype(F32)
    final = (i < pl.num_programs(1) - 1).astype(F32)
    conv_ref[pad:pad + tm, :] = bc_ref[...].astype(F32) * bh_ref[...].astype(F32)
    conv_ref[pad - 1:pad, :] = (bcp_ref[BF16_ROWS - 1:, :].astype(F32)
                                * bhp_ref[BF16_ROWS - 1:, :].astype(F32)) * first
    conv_ref[pad + tm:pad + tm + 1, :] = (bcn_ref[:1, :].astype(F32) * bhn_ref[:1, :].astype(F32)) * final
    cw = cw_ref[...]
    conv = (conv_ref[pad - 1:pad - 1 + tm, :] * cw[0:1] + conv_ref[pad:pad + tm, :] * cw[1:2]
            + conv_ref[pad + 1:pad + 1 + tm, :] * cw[2:3])
    ob = (bb_ref[...].astype(F32) * conv).astype(BF16)
    l0, l1, l2 = l0_ref[...], l1_ref[...], l2_ref[...]
    mx = jnp.maximum(jnp.maximum(l0, l1), l2)
    e0, e1, e2 = jnp.exp(l0 - mx), jnp.exp(l1 - mx), jnp.exp(l2 - mx)
    od = ((o0_ref[...].astype(F32) * e0 + o1_ref[...].astype(F32) * e1 + o2_ref[...].astype(F32) * e2)
          / (e0 + e1 + e2)).astype(BF16)
    wd = 2 * LANES
    y = jnp.dot(oa_ref[...], w_ref[0:wd, :], preferred_element_type=F32)
    y = y + jnp.dot(ob, w_ref[wd:2 * wd, :], preferred_element_type=F32)
    y = y + jnp.dot(oc_ref[...], w_ref[2 * wd:3 * wd, :], preferred_element_type=F32)
    y = y + jnp.dot(od, w_ref[3 * wd:4 * wd, :], preferred_element_type=F32)
    out_ref[...] = x_ref[...] + gate_ref[...] * _rms(y, gn_ref[...])


def _outproj(x, oa, qkv, oc, od, lse, w, cw, gn, gate, tm=512):
    b, s, d = x.shape
    wd = 2 * LANES
    hb = tm // BF16_ROWS
    last = s // BF16_ROWS - 1

    def rows(width, off=0):
        return pl.BlockSpec((None, tm, width), lambda bi, i: (bi, i, off // width))

    def halo_prev(off):
        return pl.BlockSpec((None, BF16_ROWS, wd), lambda bi, i: (bi, jnp.maximum(i * hb - 1, 0), off // wd))

    def halo_next(off):
        return pl.BlockSpec((None, BF16_ROWS, wd), lambda bi, i: (bi, jnp.minimum((i + 1) * hb, last), off // wd))

    def const(shape):
        return pl.BlockSpec(shape, lambda bi, i: (0,) * len(shape))

    return pl.pallas_call(
        functools.partial(_outproj_body, tm=tm),
        grid=(b, s // tm),
        in_specs=[rows(d), rows(wd), rows(wd, OFF_BB), rows(wd, OFF_BC), rows(wd, OFF_BH),
                  halo_prev(OFF_BC), halo_prev(OFF_BH), halo_next(OFF_BC), halo_next(OFF_BH),
                  rows(wd), rows(wd), rows(wd), rows(wd), rows(wd), rows(wd), rows(wd),
                  const(w.shape), const(cw.shape), const(gn.shape),
                  pl.BlockSpec((None, 1, d), lambda bi, i: (bi, 0, 0))],
        out_specs=rows(d),
        out_shape=jax.ShapeDtypeStruct((b, s, d), F32),
        scratch_shapes=[pltpu.VMEM((tm + 16, wd), F32)],
        compiler_params=_cparams(("parallel", "arbitrary")),
        name="outproj",
    )(x, oa, qkv, qkv, qkv, qkv, qkv, qkv, qkv, oc, od[0], od[1], od[2], lse[0], lse[1], lse[2],
      w, cw, gn, gate)


def _ffn_body(x_ref, xp_ref, xn_ref, g_ref, sc_ref, sh_ref, gate_ref, gy_ref, wg_ref, wv_ref, cg_ref, cv_ref,
              wd_ref, out_ref, h_ref, ug_ref, uv_ref, acc_ref, *, tm):
    i = pl.program_id(1)
    j = pl.program_id(2)
    pad = BF16_ROWS

    @pl.when(j == 0)
    def _():
        def mod(v):
            return _rms(v, g_ref[...]) * (1.0 + sc_ref[...]) + sh_ref[...]

        first = (i > 0).astype(F32)
        final = (i < pl.num_programs(1) - 1).astype(F32)
        h_ref[0:pad, :] = (mod(xp_ref[...]) * first).astype(BF16)
        h_ref[pad:pad + tm, :] = mod(x_ref[...]).astype(BF16)
        h_ref[pad + tm:, :] = (mod(xn_ref[...]) * final).astype(BF16)
        acc_ref[...] = jnp.zeros_like(acc_ref)

    h = h_ref[...]
    ug_ref[...] = jnp.dot(h, wg_ref[...], preferred_element_type=F32)
    uv_ref[...] = jnp.dot(h, wv_ref[...], preferred_element_type=F32)

    def conv(u_ref, c_ref):
        cw = c_ref[...]
        return (u_ref[pad - 1:pad - 1 + tm, :] * cw[0:1] + u_ref[pad:pad + tm, :] * cw[1:2]
                + u_ref[pad + 1:pad + 1 + tm, :] * cw[2:3])

    a = jax.nn.gelu(conv(ug_ref, cg_ref), approximate=True) * conv(uv_ref, cv_ref)
    acc_ref[...] += jnp.dot(a.astype(BF16), wd_ref[...], preferred_element_type=F32)

    @pl.when(j == pl.num_programs(2) - 1)
    def _():
        out_ref[...] = x_ref[...] + gate_ref[...] * _rms(acc_ref[...], gy_ref[...])


def _ffn(x, g, sc, sh, gate, gy, w_up, cw, w_down, tm=512, tn=256):
    b, s, d = x.shape
    ff = w_down.shape[0]
    nj = ff // tn
    hb = tm // BF16_ROWS
    last = s // BF16_ROWS - 1

    def const(shape):
        return pl.BlockSpec(shape, lambda bi, i, j: (0,) * len(shape))

    def per_batch():
        return pl.BlockSpec((None, 1, d), lambda bi, i, j: (bi, 0, 0))

    return pl.pallas_call(
        functools.partial(_ffn_body, tm=tm),
        grid=(b, s // tm, nj),
        in_specs=[pl.BlockSpec((None, tm, d), lambda bi, i, j: (bi, i, 0)),
                  pl.BlockSpec((None, BF16_ROWS, d), lambda bi, i, j: (bi, jnp.maximum(i * hb - 1, 0), 0)),
                  pl.BlockSpec((None, BF16_ROWS, d), lambda bi, i, j: (bi, jnp.minimum((i + 1) * hb, last), 0)),
                  const(g.shape), per_batch(), per_batch(), per_batch(), const(gy.shape),
                  pl.BlockSpec((d, tn), lambda bi, i, j: (0, j)),
                  pl.BlockSpec((d, tn), lambda bi, i, j: (0, j + nj)),
                  pl.BlockSpec((3, tn), lambda bi, i, j: (0, j)),
                  pl.BlockSpec((3, tn), lambda bi, i, j: (0, j + nj)),
                  pl.BlockSpec((tn, d), lambda bi, i, j: (j, 0))],
        out_specs=pl.BlockSpec((None, tm, d), lambda bi, i, j: (bi, i, 0)),
        out_shape=jax.ShapeDtypeStruct((b, s, d), F32),
        scratch_shapes=[pltpu.VMEM((tm + 2 * BF16_ROWS, d), BF16),
                        pltpu.VMEM((tm + 2 * BF16_ROWS, tn), F32),
                        pltpu.VMEM((tm + 2 * BF16_ROWS, tn), F32),
                        pltpu.VMEM((tm, d), F32)],
        compiler_params=_cparams(("parallel", "parallel", "arbitrary")),
        name="ffn",
    )(x, x, x, g, sc, sh, gate, gy, w_up, w_up, cw, cw, w_down)


def _pack_w_in(w_in):
    nl, d, _ = w_in.shape
    hd = HEAD_DIM
    aq = w_in[..., 0:256]
    aq = jnp.concatenate([aq[..., 0:hd], aq[..., 2 * hd:3 * hd], aq[..., hd:2 * hd], aq[..., 3 * hd:]], axis=-1)
    ckr = w_in[..., 1664:1696]
    ckr = jnp.concatenate([jnp.zeros((nl, d, C_NOPE_DIM), w_in.dtype), ckr,
                           jnp.zeros((nl, d, LANES - C_NOPE_DIM - C_ROPE_DIM), w_in.dtype)], axis=-1)
    return jnp.concatenate([aq, w_in[..., 256:1664], ckr, w_in[..., 1696:]], axis=-1).astype(BF16)


def _pack_w_uq(w):
    nl, r, _ = w.shape
    half = C_ROPE_DIM // 2
    w = w.reshape(nl, r, C_HEADS, C_NOPE_DIM + C_ROPE_DIM)
    nope, r1, r2 = w[..., :C_NOPE_DIM], w[..., C_NOPE_DIM:C_NOPE_DIM + half], w[..., C_NOPE_DIM + half:]
    z_tail = jnp.zeros((nl, r, C_HEADS, LANES - C_NOPE_DIM - C_ROPE_DIM), w.dtype)
    plain = jnp.concatenate([nope, r1, r2, z_tail], axis=-1).reshape(nl, r, C_HEADS * LANES)
    swapped = jnp.concatenate([jnp.zeros_like(nope), r2, r1, z_tail], axis=-1).reshape(nl, r, C_HEADS * LANES)
    return jnp.concatenate([plain, swapped], axis=-1).astype(BF16)


def _pack_w_ukv(w):
    nl, r, _ = w.shape
    w = w.reshape(nl, r, C_HEADS, C_NOPE_DIM + C_V_DIM)
    kn = jnp.concatenate([w[..., :C_NOPE_DIM], jnp.zeros((nl, r, C_HEADS, LANES - C_NOPE_DIM), w.dtype)], axis=-1)
    return jnp.concatenate([kn.reshape(nl, r, C_HEADS * LANES),
                            w[..., C_NOPE_DIM:].reshape(nl, r, C_HEADS * C_V_DIM)], axis=-1).astype(BF16)


def _pack_w_out(w_out):
    hd = HEAD_DIM
    return jnp.concatenate([w_out[:, 0:hd], w_out[:, 2 * hd:3 * hd], w_out[:, hd:2 * hd], w_out[:, 3 * hd:]],
                           axis=1).astype(BF16)


def _rope_tables(positions):
    half = C_ROPE_DIM // 2
    inv_freq = ROPE_THETA ** (-jnp.arange(half, dtype=jnp.float32) / half)
    ang = positions.astype(jnp.float32)[..., None] * inv_freq
    cos, sin = jnp.cos(ang), jnp.sin(ang)
    lead = positions.shape + (C_NOPE_DIM,)
    tail = positions.shape + (LANES - C_NOPE_DIM - C_ROPE_DIM,)
    cos_t = jnp.concatenate([jnp.ones(lead, F32), cos, cos, jnp.zeros(tail, F32)], axis=-1)
    sin_t = jnp.concatenate([jnp.zeros(lead, F32), -sin, sin, jnp.zeros(tail, F32)], axis=-1)
    return cos_t, sin_t


def _swap_matrix():
    half = C_ROPE_DIM // 2
    idx = jnp.arange(LANES)
    src = jnp.where((idx >= C_NOPE_DIM) & (idx < C_NOPE_DIM + half), idx + half,
                    jnp.where((idx >= C_NOPE_DIM + half) & (idx < C_NOPE_DIM + C_ROPE_DIM), idx - half, idx))
    return (idx[:, None] == src[None, :]).astype(BF16)


def kernel(x, c, positions, rel_bias, w_mod, b_mod, norm_g, w_in, a_sink, b_conv, c_norm_q, c_norm_kv,
           c_w_uq, c_w_ukv, w_out, w_up, ffn_conv, w_down):
    b, s, d = x.shape
    depth = w_in.shape[0]

    bias_a = _band_bias(rel_bias[:, :A_Q_HEADS], A_RADIUS, 1)
    bias_d = [_band_bias(rel_bias[:, A_Q_HEADS + i * D_HEADS:A_Q_HEADS + (i + 1) * D_HEADS], (w // 2) // dil, dil)
              for i, (w, dil) in enumerate(D_PATTERNS)]
    cos_t, sin_t = _rope_tables(positions)
    swap = _swap_matrix()

    w_in_p = _pack_w_in(w_in)
    w_uq_p = _pack_w_uq(c_w_uq)
    w_ukv_p = _pack_w_ukv(c_w_ukv)
    w_out_p = _pack_w_out(w_out)
    w_up_b = w_up.astype(BF16)
    w_down_b = w_down.astype(BF16)

    mod = _modulation(c, w_mod, b_mod)

    for l in range(depth):
        sh1, sc1, g1, sh2, sc2, g2 = (mod[l, k].reshape(b, 1, d) for k in range(6))
        gains = norm_g[l]

        qkv = _inproj(x, gains[0:1], sc1, sh1, w_in_p[l])

        oa = _band_attention(qkv, bias_a, a_sink[l], dil=1, radius=A_RADIUS, tq=256, q_off=OFF_AQ, k_off=OFF_AK,
                             v_off=OFF_AV, kv_w=LANES, heads=((0, 2), (1, 3)), has_lse=False)[0]

        qc, kc, vc = _mla_prep(qkv, cos_t, sin_t, c_norm_q[l][None], c_norm_kv[l][None], w_uq_p[l], w_ukv_p[l],
                               swap)
        oc = _mla_attn(qc, kc, vc)

        od, lse = [], []
        for i, (w, dil) in enumerate(D_PATTERNS):
            off = OFF_D + 3 * i * D_HEADS * HEAD_DIM
            o_i, l_i = _band_attention(qkv, bias_d[i], None, dil=dil, radius=(w // 2) // dil, tq=256, q_off=off,
                                       k_off=off + 256, v_off=off + 512, kv_w=2 * LANES,
                                       heads=((0, 1), (2, 3)), has_lse=True)
            od.append(o_i)
            lse.append(l_i)

        x = _outproj(x, oa, qkv, oc, od, lse, w_out_p[l], b_conv[l], gains[1:2], g1)
        x = _ffn(x, gains[2:3], sc2, sh2, g2, gains[3:4], w_up_b[l], ffn_conv[l], w_down_b[l])
    return x
```

```python
import functools
import math

import jax
import jax.numpy as jnp
from jax import lax
from jax.experimental import pallas as pl
from jax.experimental.pallas import tpu as pltpu

D_MODEL = 1024
HEAD_DIM = 64
A_Q_HEADS = 4
A_KV_HEADS = 2
A_RADIUS = 128
B_WIDTH = 256
C_HEADS = 4
C_Q_RANK = 256
C_KV_RANK = 128
C_NOPE_DIM = 64
C_ROPE_DIM = 32
C_V_DIM = 64
ROPE_THETA = 10000.0
D_HEADS = 4
D_PATTERNS = ((128, 1), (512, 4), (2048, 16))
REL_BUCKETS = 32
REL_MAX_DISTANCE = 1024
D_FF = 2816
EPS = 1e-6
NEG = -1e30

LANES = 128
BF16_ROWS = 16
SUB_Q = 128
VMEM_LIMIT = 48 * 1024 * 1024

IN_COLS = 4096
OFF_AQ, OFF_AK, OFF_AV = 0, 256, 384
OFF_BB, OFF_BC, OFF_BH = 512, 768, 1024
OFF_CQ, OFF_CKV, OFF_CKR = 1280, 1536, 1664
OFF_D = 1792
PATTERN_COLS = 3 * D_HEADS * HEAD_DIM
MAIN_COLS = OFF_D + PATTERN_COLS
COL_CHUNK = 512

F32 = jnp.float32
BF16 = jnp.bfloat16


def _cparams(sem):
    return pltpu.CompilerParams(dimension_semantics=sem, vmem_limit_bytes=VMEM_LIMIT)


def _rms(v, g):
    return v * lax.rsqrt(jnp.mean(v * v, axis=-1, keepdims=True) + EPS) * g


def _mod_body(c_ref, w_ref, b_ref, o_ref):
    c = c_ref[...]
    act = c / (1.0 + jnp.exp(-c))
    o_ref[...] = jnp.dot(act, w_ref[...], preferred_element_type=F32,
                         precision=lax.Precision.HIGHEST) + b_ref[...]


def _modulation(c, w_mod, b_mod):
    nl, d, _ = w_mod.shape
    b = c.shape[0]
    return pl.pallas_call(
        _mod_body,
        grid=(nl, 6),
        in_specs=[pl.BlockSpec((b, d), lambda l, k: (0, 0)),
                  pl.BlockSpec((None, d, d), lambda l, k: (l, 0, k)),
                  pl.BlockSpec((None, 1, d), lambda l, k: (l, 0, k))],
        out_specs=pl.BlockSpec((None, None, b, d), lambda l, k: (l, k, 0, 0)),
        out_shape=jax.ShapeDtypeStruct((nl, 6, b, d), F32),
        compiler_params=_cparams(("parallel", "parallel")),
        name="modulation",
    )(c, w_mod, b_mod.reshape(nl, 1, 6 * d))


def _inproj_body(x_ref, g_ref, sc_ref, sh_ref, w_ref, om_ref, o4_ref, o16_ref, hf_ref, *, tm):
    h = _rms(x_ref[...], g_ref[...]) * (1.0 + sc_ref[...]) + sh_ref[...]
    nc = h.shape[-1] // LANES
    for c in range(nc):
        hf_ref[c] = h[:, c * LANES:(c + 1) * LANES]
    hb = h.astype(BF16)
    for c0 in range(0, MAIN_COLS, COL_CHUNK):
        om_ref[:, c0:c0 + COL_CHUNK] = jnp.dot(hb, w_ref[:, c0:c0 + COL_CHUNK],
                                               preferred_element_type=F32).astype(BF16)
    for o_ref, dil, c0 in ((o4_ref, 4, MAIN_COLS), (o16_ref, 16, MAIN_COLS + PATTERN_COLS)):
        n = tm // dil
        hp = jnp.concatenate(
            [jnp.concatenate([hf_ref[c, pl.ds(r, n, stride=dil), :] for r in range(dil)], axis=0)
             for c in range(nc)], axis=1).astype(BF16)
        res = jnp.dot(hp, w_ref[:, c0:c0 + PATTERN_COLS], preferred_element_type=F32).astype(BF16)
        for r in range(dil):
            o_ref[r] = res[r * n:(r + 1) * n]


def _inproj(x, g, sc, sh, w, tm=512):
    b, s, d = x.shape
    d4, d16 = D_PATTERNS[1][1], D_PATTERNS[2][1]
    return pl.pallas_call(
        functools.partial(_inproj_body, tm=tm),
        grid=(b, s // tm),
        in_specs=[pl.BlockSpec((None, tm, d), lambda bi, i: (bi, i, 0)),
                  pl.BlockSpec((1, d), lambda bi, i: (0, 0)),
                  pl.BlockSpec((None, 1, d), lambda bi, i: (bi, 0, 0)),
                  pl.BlockSpec((None, 1, d), lambda bi, i: (bi, 0, 0)),
                  pl.BlockSpec(w.shape, lambda bi, i: (0, 0), pipeline_mode=pl.Buffered(1))],
        out_specs=[pl.BlockSpec((None, tm, MAIN_COLS), lambda bi, i: (bi, i, 0)),
                   pl.BlockSpec((None, d4, tm // d4, PATTERN_COLS), lambda bi, i: (bi, 0, i, 0)),
                   pl.BlockSpec((None, d16, tm // d16, PATTERN_COLS), lambda bi, i: (bi, 0, i, 0))],
        out_shape=[jax.ShapeDtypeStruct((b, s, MAIN_COLS), BF16),
                   jax.ShapeDtypeStruct((b, d4, s // d4, PATTERN_COLS), BF16),
                   jax.ShapeDtypeStruct((b, d16, s // d16, PATTERN_COLS), BF16)],
        scratch_shapes=[pltpu.VMEM((d // LANES, tm, LANES), F32)],
        compiler_params=_cparams(("parallel", "parallel")),
        name="inproj",
    )(x, g, sc, sh, w)


def _t5_bucket(rel):
    half = REL_BUCKETS // 2
    max_exact = half // 2
    n = jnp.abs(rel)
    n_f = jnp.maximum(n, 1).astype(jnp.float32)
    large = max_exact + (jnp.log(n_f / max_exact) / math.log(REL_MAX_DISTANCE / max_exact)
                         * (half - max_exact)).astype(jnp.int32)
    large = jnp.minimum(large, half - 1)
    return jnp.where(rel > 0, half, 0) + jnp.where(n < max_exact, n, large)


def _band_bias(table, radius, stride):
    qi = jnp.arange(SUB_Q)[:, None]
    kj = jnp.arange(SUB_Q + 2 * radius)[None, :]
    off = kj - radius - qi
    bias = jnp.transpose(table[_t5_bucket(off * stride)], (2, 0, 1)).astype(F32)
    return jnp.where((jnp.abs(off) <= radius)[None], bias, NEG)


def _band_body(*refs, radius, tq, seq, kv_w, heads, has_sink, has_lse):
    q_ref, kp_ref, kc_ref, kn_ref, vp_ref, vc_ref, vn_ref, bias_ref = refs[:8]
    rest = refs[8:]
    if has_sink:
        sink_ref, rest = rest[0], rest[1:]
    o_ref = rest[0]
    lse_ref = rest[1] if has_lse else None

    i = pl.program_id(2)
    nk = SUB_Q + 2 * radius
    kcat = jnp.concatenate([kp_ref[...], kc_ref[...], kn_ref[...]], axis=0)
    vcat = jnp.concatenate([vp_ref[...], vc_ref[...], vn_ref[...]], axis=0)
    lo = lax.broadcasted_iota(jnp.int32, (1, LANES), 1) < HEAD_DIM
    scale = HEAD_DIM ** -0.5

    for sb in range(tq // SUB_Q):
        r0 = sb * SUB_Q
        qs = q_ref[r0:r0 + SUB_Q, :]
        ks = kcat[r0:r0 + nk]
        vs = vcat[r0:r0 + nk]
        kpos = i * tq + (r0 - radius) + lax.broadcasted_iota(jnp.int32, (1, nk), 1)
        inb = (kpos >= 0) & (kpos < seq)
        for g in range(2):
            qg = qs[:, g * LANES:(g + 1) * LANES] * scale
            kg = ks if kv_w == LANES else ks[:, g * LANES:(g + 1) * LANES]
            vg = vs if kv_w == LANES else vs[:, g * LANES:(g + 1) * LANES]
            acc = None
            lacc = None
            for half in range(2):
                head = heads[g][half]
                hm = lo if half == 0 else jnp.logical_not(lo)
                qm = jnp.where(hm, qg, jnp.zeros_like(qg))
                sc = lax.dot_general(qm, kg, (((1,), (1,)), ((), ())), preferred_element_type=F32)
                sc = jnp.where(inb, sc + bias_ref[head], NEG)
                m = jnp.max(sc, axis=-1, keepdims=True)
                if has_sink:
                    m = jnp.maximum(m, sink_ref[head])
                p = jnp.exp(sc - m)
                den = jnp.sum(p, axis=-1, keepdims=True)
                if has_sink:
                    den = den + jnp.exp(sink_ref[head] - m)
                o = jnp.dot(p.astype(BF16), vg, preferred_element_type=F32) / den
                acc = o if acc is None else jnp.where(hm, o, acc)
                if has_lse:
                    ls = jnp.broadcast_to(m + jnp.log(den), (SUB_Q, LANES))
                    lacc = ls if lacc is None else jnp.where(hm, ls, lacc)
            o_ref[r0:r0 + SUB_Q, g * LANES:(g + 1) * LANES] = acc.astype(BF16)
            if has_lse:
                lse_ref[r0:r0 + SUB_Q, g * LANES:(g + 1) * LANES] = lacc


def _band_attention(qkv, bias, sink, *, dil, radius, tq, q_off, k_off, v_off, kv_w, heads, has_lse):
    b = qkv.shape[0]
    seq = qkv.shape[-2]
    s = seq * dil
    tq = min(tq, seq)
    qw = 2 * LANES
    steps = tq // radius
    last = seq // radius - 1
    if dil == 1:
        qkv = qkv.reshape(b, 1, seq, qkv.shape[-1])

    def spec(rows, width, off, row_index):
        return pl.BlockSpec((None, None, rows, width), lambda bi, r, i: (bi, r, row_index(i), off // width))

    def cur(i):
        return i

    def prev(i):
        return jnp.maximum(i * steps - 1, 0)

    def nxt(i):
        return jnp.minimum((i + 1) * steps, last)

    in_specs = [spec(tq, qw, q_off, cur)]
    for off in (k_off, v_off):
        in_specs += [spec(radius, kv_w, off, prev), spec(tq, kv_w, off, cur), spec(radius, kv_w, off, nxt)]
    in_specs.append(pl.BlockSpec(bias.shape, lambda bi, r, i: (0, 0, 0)))
    args = [qkv] * 7 + [bias]
    if sink is not None:
        in_specs.append(pl.BlockSpec(memory_space=pltpu.SMEM))
        args.append(sink)

    o_spec = pl.BlockSpec((None, tq, qw), lambda bi, r, i: (bi, i, r))
    out_shape = [jax.ShapeDtypeStruct((b, seq, dil * qw), BF16)]
    out_specs = [o_spec]
    if has_lse:
        out_shape.append(jax.ShapeDtypeStruct((b, seq, dil * qw), F32))
        out_specs.append(o_spec)

    outs = pl.pallas_call(
        functools.partial(_band_body, radius=radius, tq=tq, seq=seq, kv_w=kv_w, heads=heads,
                          has_sink=sink is not None, has_lse=has_lse),
        grid=(b, dil, seq // tq),
        in_specs=in_specs,
        out_specs=out_specs,
        out_shape=out_shape,
        compiler_params=_cparams(("parallel", "parallel", "parallel")),
        name=f"band_attention_d{dil}_r{radius}",
    )(*args)
    return [o.reshape(b, s, qw) for o in outs]


def _mla_prep_body(cq_ref, ckv_ref, ckr_ref, cos_ref, sin_ref, gq_ref, gkv_ref, wq_ref, wkv_ref, swap_ref,
                   q_ref, k_ref, v_ref):
    cs = cos_ref[...]
    sn = sin_ref[...]
    scale = (C_NOPE_DIM + C_ROPE_DIM) ** -0.5
    nq = _rms(cq_ref[...].astype(F32), gq_ref[...]).astype(BF16)
    qq = jnp.dot(nq, wq_ref[...], preferred_element_type=F32)
    nkv = _rms(ckv_ref[...].astype(F32), gkv_ref[...]).astype(BF16)
    kvv = jnp.dot(nkv, wkv_ref[...], preferred_element_type=F32)
    kr = ckr_ref[...]
    kr_sw = jnp.dot(kr, swap_ref[...], preferred_element_type=F32)
    k_rope = kr.astype(F32) * cs + kr_sw * sn
    hw = C_HEADS * LANES
    for h in range(C_HEADS):
        a = qq[:, h * LANES:(h + 1) * LANES]
        a_sw = qq[:, hw + h * LANES:hw + (h + 1) * LANES]
        q_ref[:, h * LANES:(h + 1) * LANES] = ((a * cs + a_sw * sn) * scale).astype(BF16)
        k_ref[:, h * LANES:(h + 1) * LANES] = (kvv[:, h * LANES:(h + 1) * LANES] + k_rope).astype(BF16)
    v_ref[...] = kvv[:, hw:].astype(BF16)


def _mla_prep(qkv, cos_t, sin_t, gq, gkv, wq, wkv, swap, tm=512):
    b, s, n = qkv.shape
    hw = C_HEADS * LANES
    vw = C_HEADS * C_V_DIM

    def const(shape):
        return pl.BlockSpec(shape, lambda bi, i: (0,) * len(shape))

    def rows(width, off=0):
        return pl.BlockSpec((None, tm, width), lambda bi, i: (bi, i, off // width))

    return pl.pallas_call(
        _mla_prep_body,
        grid=(b, s // tm),
        in_specs=[rows(C_Q_RANK, OFF_CQ), rows(LANES, OFF_CKV), rows(LANES, OFF_CKR),
                  rows(LANES), rows(LANES),
                  const(gq.shape), const(gkv.shape), const(wq.shape), const(wkv.shape), const(swap.shape)],
        out_specs=[rows(hw), rows(hw), rows(vw)],
        out_shape=[jax.ShapeDtypeStruct((b, s, hw), BF16), jax.ShapeDtypeStruct((b, s, hw), BF16),
                   jax.ShapeDtypeStruct((b, s, vw), BF16)],
        compiler_params=_cparams(("parallel", "parallel")),
        name="mla_prep",
    )(qkv, qkv, qkv, cos_t, sin_t, gq, gkv, wq, wkv, swap)


def _mla_attn_body(q_ref, k_ref, v_ref, o_ref):
    lo = lax.broadcasted_iota(jnp.int32, (1, LANES), 1) < C_V_DIM
    for g in range(C_HEADS // 2):
        acc = None
        for half in range(2):
            h = 2 * g + half
            q = q_ref[:, h * LANES:(h + 1) * LANES]
            k = k_ref[:, h * LANES:(h + 1) * LANES]
            sc = lax.dot_general(q, k, (((1,), (1,)), ((), ())), preferred_element_type=F32)
            m = jnp.max(sc, axis=-1, keepdims=True)
            p = jnp.exp(sc - m)
            den = jnp.sum(p, axis=-1, keepdims=True)
            o = jnp.dot(p.astype(BF16), v_ref[:, g * LANES:(g + 1) * LANES], preferred_element_type=F32) / den
            acc = o if acc is None else jnp.where(lo, acc, o)
        o_ref[:, g * LANES:(g + 1) * LANES] = acc.astype(BF16)


def _mla_attn(q, k, v, tq=256):
    b, s, hw = q.shape
    vw = v.shape[-1]
    return pl.pallas_call(
        _mla_attn_body,
        grid=(b, s // tq),
        in_specs=[pl.BlockSpec((None, tq, hw), lambda bi, i: (bi, i, 0)),
                  pl.BlockSpec((None, s, hw), lambda bi, i: (bi, 0, 0)),
                  pl.BlockSpec((None, s, vw), lambda bi, i: (bi, 0, 0))],
        out_specs=pl.BlockSpec((None, tq, vw), lambda bi, i: (bi, i, 0)),
        out_shape=jax.ShapeDtypeStruct((b, s, vw), BF16),
        compiler_params=_cparams(("parallel", "arbitrary")),
        name="mla_attention",
    )(q, k, v)


def _outproj_body(x_ref, oa_ref, bb_ref, bc_ref, bh_ref, bcp_ref, bhp_ref, bcn_ref, bhn_ref, oc_ref,
                  o0_ref, o1_ref, o2_ref, l0_ref, l1_ref, l2_ref, w_ref, cw_ref, gn_ref, gate_ref,
                  out_ref, conv_ref, *, tm):
    i = pl.program_id(1)
    pad = 8
    first = (i > 0).astype(F32)
    final = (i < pl.num_programs(1) - 1).astype(F32)
    conv_ref[pad:pad + tm, :] = bc_ref[...].astype(F32) * bh_ref[...].astype(F32)
    conv_ref[pad - 1:pad, :] = (bcp_ref[BF16_ROWS - 1:, :].astype(F32)
                                * bhp_ref[BF16_ROWS - 1:, :].astype(F32)) * first
    conv_ref[pad + tm:pad + tm + 1, :] = (bcn_ref[:1, :].astype(F32) * bhn_ref[:1, :].astype(F32)) * final
    cw = cw_ref[...]
    conv = (conv_ref[pad - 1:pad - 1 + tm, :] * cw[0:1] + conv_ref[pad:pad + tm, :] * cw[1:2]
            + conv_ref[pad + 1:pad + 1 + tm, :] * cw[2:3])
    ob = (bb_ref[...].astype(F32) * conv).astype(BF16)
    l0, l1, l2 = l0_ref[...], l1_ref[...], l2_ref[...]
    mx = jnp.maximum(jnp.maximum(l0, l1), l2)
    e0, e1, e2 = jnp.exp(l0 - mx), jnp.exp(l1 - mx), jnp.exp(l2 - mx)
    od = ((o0_ref[...].astype(F32) * e0 + o1_ref[...].astype(F32) * e1 + o2_ref[...].astype(F32) * e2)
          / (e0 + e1 + e2)).astype(BF16)
    wd = 2 * LANES
    y = jnp.dot(oa_ref[...], w_ref[0:wd, :], preferred_element_type=F32)
    y = y + jnp.dot(ob, w_ref[wd:2 * wd, :], preferred_element_type=F32)
    y = y + jnp.dot(oc_ref[...], w_ref[2 * wd:3 * wd, :], preferred_element_type=F32)
    y = y + jnp.dot(od, w_ref[3 * wd:4 * wd, :], preferred_element_type=F32)
    out_ref[...] = x_ref[...] + gate_ref[...] * _rms(y, gn_ref[...])


def _outproj(x, oa, qkv, oc, od, lse, w, cw, gn, gate, tm=512):
    b, s, d = x.shape
    wd = 2 * LANES
    hb = tm // BF16_ROWS
    last = s // BF16_ROWS - 1

    def rows(width, off=0):
        return pl.BlockSpec((None, tm, width), lambda bi, i: (bi, i, off // width))

    def halo_prev(off):
        return pl.BlockSpec((None, BF16_ROWS, wd), lambda bi, i: (bi, jnp.maximum(i * hb - 1, 0), off // wd))

    def halo_next(off):
        return pl.BlockSpec((None, BF16_ROWS, wd), lambda bi, i: (bi, jnp.minimum((i + 1) * hb, last), off // wd))

    def const(shape):
        return pl.BlockSpec(shape, lambda bi, i: (0,) * len(shape))

    return pl.pallas_call(
        functools.partial(_outproj_body, tm=tm),
        grid=(b, s // tm),
        in_specs=[rows(d), rows(wd), rows(wd, OFF_BB), rows(wd, OFF_BC), rows(wd, OFF_BH),
                  halo_prev(OFF_BC), halo_prev(OFF_BH), halo_next(OFF_BC), halo_next(OFF_BH),
                  rows(wd), rows(wd), rows(wd), rows(wd), rows(wd), rows(wd), rows(wd),
                  const(w.shape), const(cw.shape), const(gn.shape),
                  pl.BlockSpec((None, 1, d), lambda bi, i: (bi, 0, 0))],
        out_specs=rows(d),
        out_shape=jax.ShapeDtypeStruct((b, s, d), F32),
        scratch_shapes=[pltpu.VMEM((tm + 16, wd), F32)],
        compiler_params=_cparams(("parallel", "arbitrary")),
        name="outproj",
    )(x, oa, qkv, qkv, qkv, qkv, qkv, qkv, qkv, oc, od[0], od[1], od[2], lse[0], lse[1], lse[2],
      w, cw, gn, gate)


def _ffn_body(x_ref, xp_ref, xn_ref, g_ref, sc_ref, sh_ref, gate_ref, gy_ref, wg_ref, wv_ref, cg_ref, cv_ref,
              wd_ref, out_ref, h_ref, ug_ref, uv_ref, acc_ref, *, tm):
    i = pl.program_id(1)
    j = pl.program_id(2)
    pad = BF16_ROWS

    @pl.when(j == 0)
    def _():
        def mod(v):
            return _rms(v, g_ref[...]) * (1.0 + sc_ref[...]) + sh_ref[...]

        first = (i > 0).astype(F32)
        final = (i < pl.num_programs(1) - 1).astype(F32)
        h_ref[0:pad, :] = (mod(xp_ref[...]) * first).astype(BF16)
        h_ref[pad:pad + tm, :] = mod(x_ref[...]).astype(BF16)
        h_ref[pad + tm:, :] = (mod(xn_ref[...]) * final).astype(BF16)
        acc_ref[...] = jnp.zeros_like(acc_ref)

    h = h_ref[...]
    ug_ref[...] = jnp.dot(h, wg_ref[...], preferred_element_type=F32)
    uv_ref[...] = jnp.dot(h, wv_ref[...], preferred_element_type=F32)

    def conv(u_ref, c_ref):
        cw = c_ref[...]
        return (u_ref[pad - 1:pad - 1 + tm, :] * cw[0:1] + u_ref[pad:pad + tm, :] * cw[1:2]
                + u_ref[pad + 1:pad + 1 + tm, :] * cw[2:3])

    a = jax.nn.gelu(conv(ug_ref, cg_ref), approximate=True) * conv(uv_ref, cv_ref)
    acc_ref[...] += jnp.dot(a.astype(BF16), wd_ref[...], preferred_element_type=F32)

    @pl.when(j == pl.num_programs(2) - 1)
    def _():
        out_ref[...] = x_ref[...] + gate_ref[...] * _rms(acc_ref[...], gy_ref[...])


def _ffn(x, g, sc, sh, gate, gy, w_up, cw, w_down, tm=512, tn=256):
    b, s, d = x.shape
    ff = w_down.shape[0]
    nj = ff // tn
    hb = tm // BF16_ROWS
    last = s // BF16_ROWS - 1

    def const(shape):
        return pl.BlockSpec(shape, lambda bi, i, j: (0,) * len(shape))

    def per_batch():
        return pl.BlockSpec((None, 1, d), lambda bi, i, j: (bi, 0, 0))

    return pl.pallas_call(
        functools.partial(_ffn_body, tm=tm),
        grid=(b, s // tm, nj),
        in_specs=[pl.BlockSpec((None, tm, d), lambda bi, i, j: (bi, i, 0)),
                  pl.BlockSpec((None, BF16_ROWS, d), lambda bi, i, j: (bi, jnp.maximum(i * hb - 1, 0), 0)),
                  pl.BlockSpec((None, BF16_ROWS, d), lambda bi, i, j: (bi, jnp.minimum((i + 1) * hb, last), 0)),
                  const(g.shape), per_batch(), per_batch(), per_batch(), const(gy.shape),
                  pl.BlockSpec((d, tn), lambda bi, i, j: (0, j)),
                  pl.BlockSpec((d, tn), lambda bi, i, j: (0, j + nj)),
                  pl.BlockSpec((3, tn), lambda bi, i, j: (0, j)),
                  pl.BlockSpec((3, tn), lambda bi, i, j: (0, j + nj)),
                  pl.BlockSpec((tn, d), lambda bi, i, j: (j, 0))],
        out_specs=pl.BlockSpec((None, tm, d), lambda bi, i, j: (bi, i, 0)),
        out_shape=jax.ShapeDtypeStruct((b, s, d), F32),
        scratch_shapes=[pltpu.VMEM((tm + 2 * BF16_ROWS, d), BF16),
                        pltpu.VMEM((tm + 2 * BF16_ROWS, tn), F32),
                        pltpu.VMEM((tm + 2 * BF16_ROWS, tn), F32),
                        pltpu.VMEM((tm, d), F32)],
        compiler_params=_cparams(("parallel", "parallel", "arbitrary")),
        name="ffn",
    )(x, x, x, g, sc, sh, gate, gy, w_up, w_up, cw, cw, w_down)


def _pack_w_in(w_in):
    nl, d, _ = w_in.shape
    hd = HEAD_DIM
    aq = w_in[..., 0:256]
    aq = jnp.concatenate([aq[..., 0:hd], aq[..., 2 * hd:3 * hd], aq[..., hd:2 * hd], aq[..., 3 * hd:]], axis=-1)
    ckr = w_in[..., 1664:1696]
    ckr = jnp.concatenate([jnp.zeros((nl, d, C_NOPE_DIM), w_in.dtype), ckr,
                           jnp.zeros((nl, d, LANES - C_NOPE_DIM - C_ROPE_DIM), w_in.dtype)], axis=-1)
    return jnp.concatenate([aq, w_in[..., 256:1664], ckr, w_in[..., 1696:]], axis=-1).astype(BF16)


def _pack_w_uq(w):
    nl, r, _ = w.shape
    half = C_ROPE_DIM // 2
    w = w.reshape(nl, r, C_HEADS, C_NOPE_DIM + C_ROPE_DIM)
    nope, r1, r2 = w[..., :C_NOPE_DIM], w[..., C_NOPE_DIM:C_NOPE_DIM + half], w[..., C_NOPE_DIM + half:]
    z_tail = jnp.zeros((nl, r, C_HEADS, LANES - C_NOPE_DIM - C_ROPE_DIM), w.dtype)
    plain = jnp.concatenate([nope, r1, r2, z_tail], axis=-1).reshape(nl, r, C_HEADS * LANES)
    swapped = jnp.concatenate([jnp.zeros_like(nope), r2, r1, z_tail], axis=-1).reshape(nl, r, C_HEADS * LANES)
    return jnp.concatenate([plain, swapped], axis=-1).astype(BF16)


def _pack_w_ukv(w):
    nl, r, _ = w.shape
    w = w.reshape(nl, r, C_HEADS, C_NOPE_DIM + C_V_DIM)
    kn = jnp.concatenate([w[..., :C_NOPE_DIM], jnp.zeros((nl, r, C_HEADS, LANES - C_NOPE_DIM), w.dtype)], axis=-1)
    return jnp.concatenate([kn.reshape(nl, r, C_HEADS * LANES),
                            w[..., C_NOPE_DIM:].reshape(nl, r, C_HEADS * C_V_DIM)], axis=-1).astype(BF16)


def _pack_w_out(w_out):
    hd = HEAD_DIM
    return jnp.concatenate([w_out[:, 0:hd], w_out[:, 2 * hd:3 * hd], w_out[:, hd:2 * hd], w_out[:, 3 * hd:]],
                           axis=1).astype(BF16)


def _rope_tables(positions):
    half = C_ROPE_DIM // 2
    inv_freq = ROPE_THETA ** (-jnp.arange(half, dtype=jnp.float32) / half)
    ang = positions.astype(jnp.float32)[..., None] * inv_freq
    cos, sin = jnp.cos(ang), jnp.sin(ang)
    lead = positions.shape + (C_NOPE_DIM,)
    tail = positions.shape + (LANES - C_NOPE_DIM - C_ROPE_DIM,)
    cos_t = jnp.concatenate([jnp.ones(lead, F32), cos, cos, jnp.zeros(tail, F32)], axis=-1)
    sin_t = jnp.concatenate([jnp.zeros(lead, F32), -sin, sin, jnp.zeros(tail, F32)], axis=-1)
    return cos_t, sin_t


def _swap_matrix():
    half = C_ROPE_DIM // 2
    idx = jnp.arange(LANES)
    src = jnp.where((idx >= C_NOPE_DIM) & (idx < C_NOPE_DIM + half), idx + half,
                    jnp.where((idx >= C_NOPE_DIM + half) & (idx < C_NOPE_DIM + C_ROPE_DIM), idx - half, idx))
    return (idx[:, None] == src[None, :]).astype(BF16)


def kernel(x, c, positions, rel_bias, w_mod, b_mod, norm_g, w_in, a_sink, b_conv, c_norm_q, c_norm_kv,
           c_w_uq, c_w_ukv, w_out, w_up, ffn_conv, w_down):
    b, s, d = x.shape
    depth = w_in.shape[0]

    bias_a = _band_bias(rel_bias[:, :A_Q_HEADS], A_RADIUS, 1)
    bias_d = [_band_bias(rel_bias[:, A_Q_HEADS + i * D_HEADS:A_Q_HEADS + (i + 1) * D_HEADS], (w // 2) // dil, dil)
              for i, (w, dil) in enumerate(D_PATTERNS)]
    cos_t, sin_t = _rope_tables(positions)
    swap = _swap_matrix()

    w_in_p = _pack_w_in(w_in)
    w_uq_p = _pack_w_uq(c_w_uq)
    w_ukv_p = _pack_w_ukv(c_w_ukv)
    w_out_p = _pack_w_out(w_out)
    w_up_b = w_up.astype(BF16)
    w_down_b = w_down.astype(BF16)

    mod = _modulation(c, w_mod, b_mod)

    for l in range(depth):
        sh1, sc1, g1, sh2, sc2, g2 = (mod[l, k].reshape(b, 1, d) for k in range(6))
        gains = norm_g[l]

        qkv, qkv_d4, qkv_d16 = _inproj(x, gains[0:1], sc1, sh1, w_in_p[l])

        oa = _band_attention(qkv, bias_a, a_sink[l], dil=1, radius=A_RADIUS, tq=256, q_off=OFF_AQ, k_off=OFF_AK,
                             v_off=OFF_AV, kv_w=LANES, heads=((0, 2), (1, 3)), has_lse=False)[0]

        qc, kc, vc = _mla_prep(qkv, cos_t, sin_t, c_norm_q[l][None], c_norm_kv[l][None], w_uq_p[l], w_ukv_p[l],
                               swap)
        oc = _mla_attn(qc, kc, vc)

        od, lse = [], []
        for i, ((w, dil), src) in enumerate(zip(D_PATTERNS, (qkv, qkv_d4, qkv_d16))):
            off = OFF_D if dil == 1 else 0
            o_i, l_i = _band_attention(src, bias_d[i], None, dil=dil, radius=(w // 2) // dil, tq=256, q_off=off,
                                       k_off=off + 256, v_off=off + 512, kv_w=2 * LANES,
                                       heads=((0, 1), (2, 3)), has_lse=True)
            od.append(o_i)
            lse.append(l_i)

        x = _outproj(x, oa, qkv, oc, od, lse, w_out_p[l], b_conv[l], gains[1:2], g1)
        x = _ffn(x, gains[2:3], sc2, sh2, g2, gains[3:4], w_up_b[l], ffn_conv[l], w_down_b[l])
    return x
```

```python
import functools
import math

import jax
import jax.numpy as jnp
from jax import lax
from jax.experimental import pallas as pl
from jax.experimental.pallas import tpu as pltpu

D_MODEL = 1024
HEAD_DIM = 64
A_Q_HEADS = 4
A_KV_HEADS = 2
A_RADIUS = 128
B_WIDTH = 256
C_HEADS = 4
C_Q_RANK = 256
C_KV_RANK = 128
C_NOPE_DIM = 64
C_ROPE_DIM = 32
C_V_DIM = 64
ROPE_THETA = 10000.0
D_HEADS = 4
D_PATTERNS = ((128, 1), (512, 4), (2048, 16))
REL_BUCKETS = 32
REL_MAX_DISTANCE = 1024
D_FF = 2816
EPS = 1e-6
NEG = -1e30

LANES = 128
BF16_ROWS = 16
SUB_Q = 128
VMEM_LIMIT = 48 * 1024 * 1024

IN_COLS = 4096
OFF_AQ, OFF_AK, OFF_AV = 0, 256, 384
OFF_BB, OFF_BC, OFF_BH = 512, 768, 1024
OFF_CQ, OFF_CKV, OFF_CKR = 1280, 1536, 1664
OFF_D = 1792
PATTERN_COLS = 3 * D_HEADS * HEAD_DIM
MAIN_COLS = OFF_D + PATTERN_COLS
COL_CHUNK = 512

F32 = jnp.float32
BF16 = jnp.bfloat16


def _cparams(sem):
    return pltpu.CompilerParams(dimension_semantics=sem, vmem_limit_bytes=VMEM_LIMIT)


def _rms(v, g):
    return v * lax.rsqrt(jnp.mean(v * v, axis=-1, keepdims=True) + EPS) * g


def _mod_body(c_ref, w_ref, b_ref, o_ref):
    c = c_ref[...]
    act = c / (1.0 + jnp.exp(-c))
    o_ref[...] = jnp.dot(act, w_ref[...], preferred_element_type=F32,
                         precision=lax.Precision.HIGHEST) + b_ref[...]


def _modulation(c, w_mod, b_mod):
    nl, d, _ = w_mod.shape
    b = c.shape[0]
    return pl.pallas_call(
        _mod_body,
        grid=(nl, 6),
        in_specs=[pl.BlockSpec((b, d), lambda l, k: (0, 0)),
                  pl.BlockSpec((None, d, d), lambda l, k: (l, 0, k)),
                  pl.BlockSpec((None, 1, d), lambda l, k: (l, 0, k))],
        out_specs=pl.BlockSpec((None, None, b, d), lambda l, k: (l, k, 0, 0)),
        out_shape=jax.ShapeDtypeStruct((nl, 6, b, d), F32),
        compiler_params=_cparams(("parallel", "parallel")),
        name="modulation",
    )(c, w_mod, b_mod.reshape(nl, 1, 6 * d))


def _inproj_body(x_ref, g_ref, sc_ref, sh_ref, w_ref, om_ref, o4_ref, o16_ref, hf_ref, *, tm):
    h = _rms(x_ref[...], g_ref[...]) * (1.0 + sc_ref[...]) + sh_ref[...]
    nc = h.shape[-1] // LANES
    for c in range(nc):
        hf_ref[c] = h[:, c * LANES:(c + 1) * LANES]
    hb = h.astype(BF16)
    for c0 in range(0, MAIN_COLS, COL_CHUNK):
        om_ref[:, c0:c0 + COL_CHUNK] = jnp.dot(hb, w_ref[:, c0:c0 + COL_CHUNK],
                                               preferred_element_type=F32).astype(BF16)
    for o_ref, dil, c0 in ((o4_ref, 4, MAIN_COLS), (o16_ref, 16, MAIN_COLS + PATTERN_COLS)):
        n = tm // dil
        hp = jnp.concatenate(
            [jnp.concatenate([hf_ref[c, pl.ds(r, n, stride=dil), :] for r in range(dil)], axis=0)
             for c in range(nc)], axis=1).astype(BF16)
        res = jnp.dot(hp, w_ref[:, c0:c0 + PATTERN_COLS], preferred_element_type=F32).astype(BF16)
        for r in range(dil):
            o_ref[r] = res[r * n:(r + 1) * n]


def _inproj(x, g, sc, sh, w, tm=512):
    b, s, d = x.shape
    d4, d16 = D_PATTERNS[1][1], D_PATTERNS[2][1]
    return pl.pallas_call(
        functools.partial(_inproj_body, tm=tm),
        grid=(b, s // tm),
        in_specs=[pl.BlockSpec((None, tm, d), lambda bi, i: (bi, i, 0)),
                  pl.BlockSpec((1, d), lambda bi, i: (0, 0)),
                  pl.BlockSpec((None, 1, d), lambda bi, i: (bi, 0, 0)),
                  pl.BlockSpec((None, 1, d), lambda bi, i: (bi, 0, 0)),
                  pl.BlockSpec(w.shape, lambda bi, i: (0, 0), pipeline_mode=pl.Buffered(1))],
        out_specs=[pl.BlockSpec((None, tm, MAIN_COLS), lambda bi, i: (bi, i, 0)),
                   pl.BlockSpec((None, d4, tm // d4, PATTERN_COLS), lambda bi, i: (bi, 0, i, 0)),
                   pl.BlockSpec((None, d16, tm // d16, PATTERN_COLS), lambda bi, i: (bi, 0, i, 0))],
        out_shape=[jax.ShapeDtypeStruct((b, s, MAIN_COLS), BF16),
                   jax.ShapeDtypeStruct((b, d4, s // d4, PATTERN_COLS), BF16),
                   jax.ShapeDtypeStruct((b, d16, s // d16, PATTERN_COLS), BF16)],
        scratch_shapes=[pltpu.VMEM((d // LANES, tm, LANES), F32)],
        compiler_params=_cparams(("parallel", "parallel")),
        name="inproj",
    )(x, g, sc, sh, w)


def _t5_bucket(rel):
    half = REL_BUCKETS // 2
    max_exact = half // 2
    n = jnp.abs(rel)
    n_f = jnp.maximum(n, 1).astype(jnp.float32)
    large = max_exact + (jnp.log(n_f / max_exact) / math.log(REL_MAX_DISTANCE / max_exact)
                         * (half - max_exact)).astype(jnp.int32)
    large = jnp.minimum(large, half - 1)
    return jnp.where(rel > 0, half, 0) + jnp.where(n < max_exact, n, large)


def _band_bias(table, radius, stride):
    qi = jnp.arange(SUB_Q)[:, None]
    kj = jnp.arange(SUB_Q + 2 * radius)[None, :]
    off = kj - radius - qi
    bias = jnp.transpose(table[_t5_bucket(off * stride)], (2, 0, 1)).astype(F32)
    return jnp.where((jnp.abs(off) <= radius)[None], bias, NEG)


def _band_body(*refs, radius, tq, seq, kv_w, heads, has_sink, has_lse):
    q_ref, kp_ref, kc_ref, kn_ref, vp_ref, vc_ref, vn_ref, bias_ref = refs[:8]
    rest = refs[8:]
    if has_sink:
        sink_ref, rest = rest[0], rest[1:]
    o_ref = rest[0]
    lse_ref = rest[1] if has_lse else None

    i = pl.program_id(2)
    nk = SUB_Q + 2 * radius
    kcat = jnp.concatenate([kp_ref[...], kc_ref[...], kn_ref[...]], axis=0)
    vcat = jnp.concatenate([vp_ref[...], vc_ref[...], vn_ref[...]], axis=0)
    lo = lax.broadcasted_iota(jnp.int32, (1, LANES), 1) < HEAD_DIM
    scale = HEAD_DIM ** -0.5

    for sb in range(tq // SUB_Q):
        r0 = sb * SUB_Q
        qs = q_ref[r0:r0 + SUB_Q, :]
        ks = kcat[r0:r0 + nk]
        vs = vcat[r0:r0 + nk]
        kpos = i * tq + (r0 - radius) + lax.broadcasted_iota(jnp.int32, (1, nk), 1)
        inb = (kpos >= 0) & (kpos < seq)
        for g in range(2):
            qg = qs[:, g * LANES:(g + 1) * LANES] * scale
            kg = ks if kv_w == LANES else ks[:, g * LANES:(g + 1) * LANES]
            vg = vs if kv_w == LANES else vs[:, g * LANES:(g + 1) * LANES]
            acc = None
            lacc = None
            for half in range(2):
                head = heads[g][half]
                hm = lo if half == 0 else jnp.logical_not(lo)
                qm = jnp.where(hm, qg, jnp.zeros_like(qg))
                sc = lax.dot_general(qm, kg, (((1,), (1,)), ((), ())), preferred_element_type=F32)
                sc = jnp.where(inb, sc + bias_ref[head], NEG)
                m = jnp.max(sc, axis=-1, keepdims=True)
                if has_sink:
                    m = jnp.maximum(m, sink_ref[head])
                p = jnp.exp(sc - m)
                den = jnp.sum(p, axis=-1, keepdims=True)
                if has_sink:
                    den = den + jnp.exp(sink_ref[head] - m)
                o = jnp.dot(p.astype(BF16), vg, preferred_element_type=F32) / den
                acc = o if acc is None else jnp.where(hm, o, acc)
                if has_lse:
                    ls = jnp.broadcast_to(m + jnp.log(den), (SUB_Q, LANES))
                    lacc = ls if lacc is None else jnp.where(hm, ls, lacc)
            o_ref[r0:r0 + SUB_Q, g * LANES:(g + 1) * LANES] = acc.astype(BF16)
            if has_lse:
                lse_ref[r0:r0 + SUB_Q, g * LANES:(g + 1) * LANES] = lacc


def _band_attention(qkv, bias, sink, *, dil, radius, tq, q_off, k_off, v_off, kv_w, heads, has_lse):
    b = qkv.shape[0]
    seq = qkv.shape[-2]
    s = seq * dil
    tq = min(tq, seq)
    qw = 2 * LANES
    steps = tq // radius
    last = seq // radius - 1
    if dil == 1:
        qkv = qkv.reshape(b, 1, seq, qkv.shape[-1])

    def spec(rows, width, off, row_index):
        return pl.BlockSpec((None, None, rows, width), lambda bi, r, i: (bi, r, row_index(i), off // width))

    def cur(i):
        return i

    def prev(i):
        return jnp.maximum(i * steps - 1, 0)

    def nxt(i):
        return jnp.minimum((i + 1) * steps, last)

    in_specs = [spec(tq, qw, q_off, cur)]
    for off in (k_off, v_off):
        in_specs += [spec(radius, kv_w, off, prev), spec(tq, kv_w, off, cur), spec(radius, kv_w, off, nxt)]
    in_specs.append(pl.BlockSpec(bias.shape, lambda bi, r, i: (0, 0, 0)))
    args = [qkv] * 7 + [bias]
    if sink is not None:
        in_specs.append(pl.BlockSpec(memory_space=pltpu.SMEM))
        args.append(sink)

    o_spec = pl.BlockSpec((None, tq, qw), lambda bi, r, i: (bi, i, r))
    out_shape = [jax.ShapeDtypeStruct((b, seq, dil * qw), BF16)]
    out_specs = [o_spec]
    if has_lse:
        out_shape.append(jax.ShapeDtypeStruct((b, seq, dil * qw), F32))
        out_specs.append(o_spec)

    outs = pl.pallas_call(
        functools.partial(_band_body, radius=radius, tq=tq, seq=seq, kv_w=kv_w, heads=heads,
                          has_sink=sink is not None, has_lse=has_lse),
        grid=(b, dil, seq // tq),
        in_specs=in_specs,
        out_specs=out_specs,
        out_shape=out_shape,
        compiler_params=_cparams(("parallel", "parallel", "parallel")),
        name=f"band_attention_d{dil}_r{radius}",
    )(*args)
    return [o.reshape(b, s, qw) for o in outs]


def _mla_prep_body(cq_ref, ckv_ref, ckr_ref, cos_ref, sin_ref, gq_ref, gkv_ref, wq_ref, wkv_ref, swap_ref,
                   q_ref, k_ref, v_ref):
    cs = cos_ref[...]
    sn = sin_ref[...]
    scale = (C_NOPE_DIM + C_ROPE_DIM) ** -0.5
    nq = _rms(cq_ref[...].astype(F32), gq_ref[...]).astype(BF16)
    qq = jnp.dot(nq, wq_ref[...], preferred_element_type=F32)
    nkv = _rms(ckv_ref[...].astype(F32), gkv_ref[...]).astype(BF16)
    kvv = jnp.dot(nkv, wkv_ref[...], preferred_element_type=F32)
    kr = ckr_ref[...]
    kr_sw = jnp.dot(kr, swap_ref[...], preferred_element_type=F32)
    k_rope = kr.astype(F32) * cs + kr_sw * sn
    hw = C_HEADS * LANES
    for h in range(C_HEADS):
        a = qq[:, h * LANES:(h + 1) * LANES]
        a_sw = qq[:, hw + h * LANES:hw + (h + 1) * LANES]
        q_ref[:, h * LANES:(h + 1) * LANES] = ((a * cs + a_sw * sn) * scale).astype(BF16)
        k_ref[:, h * LANES:(h + 1) * LANES] = (kvv[:, h * LANES:(h + 1) * LANES] + k_rope).astype(BF16)
    v_ref[...] = kvv[:, hw:].astype(BF16)


def _mla_prep(qkv, cos_t, sin_t, gq, gkv, wq, wkv, swap, tm=512):
    b, s, n = qkv.shape
    hw = C_HEADS * LANES
    vw = C_HEADS * C_V_DIM

    def const(shape):
        return pl.BlockSpec(shape, lambda bi, i: (0,) * len(shape))

    def rows(width, off=0):
        return pl.BlockSpec((None, tm, width), lambda bi, i: (bi, i, off // width))

    return pl.pallas_call(
        _mla_prep_body,
        grid=(b, s // tm),
        in_specs=[rows(C_Q_RANK, OFF_CQ), rows(LANES, OFF_CKV), rows(LANES, OFF_CKR),
                  rows(LANES), rows(LANES),
                  const(gq.shape), const(gkv.shape), const(wq.shape), const(wkv.shape), const(swap.shape)],
        out_specs=[rows(hw), rows(hw), rows(vw)],
        out_shape=[jax.ShapeDtypeStruct((b, s, hw), BF16), jax.ShapeDtypeStruct((b, s, hw), BF16),
                   jax.ShapeDtypeStruct((b, s, vw), BF16)],
        compiler_params=_cparams(("parallel", "parallel")),
        name="mla_prep",
    )(qkv, qkv, qkv, cos_t, sin_t, gq, gkv, wq, wkv, swap)


def _mla_attn_body(q_ref, k_ref, v_ref, o_ref):
    lo = lax.broadcasted_iota(jnp.int32, (1, LANES), 1) < C_V_DIM
    for g in range(C_HEADS // 2):
        acc = None
        for half in range(2):
            h = 2 * g + half
            q = q_ref[:, h * LANES:(h + 1) * LANES]
            k = k_ref[:, h * LANES:(h + 1) * LANES]
            sc = lax.dot_general(q, k, (((1,), (1,)), ((), ())), preferred_element_type=F32)
            m = jnp.max(sc, axis=-1, keepdims=True)
            p = jnp.exp(sc - m)
            den = jnp.sum(p, axis=-1, keepdims=True)
            o = jnp.dot(p.astype(BF16), v_ref[:, g * LANES:(g + 1) * LANES], preferred_element_type=F32) / den
            acc = o if acc is None else jnp.where(lo, acc, o)
        o_ref[:, g * LANES:(g + 1) * LANES] = acc.astype(BF16)


def _mla_attn(q, k, v, tq=256):
    b, s, hw = q.shape
    vw = v.shape[-1]
    return pl.pallas_call(
        _mla_attn_body,
        grid=(b, s // tq),
        in_specs=[pl.BlockSpec((None, tq, hw), lambda bi, i: (bi, i, 0)),
                  pl.BlockSpec((None, s, hw), lambda bi, i: (bi, 0, 0)),
                  pl.BlockSpec((None, s, vw), lambda bi, i: (bi, 0, 0))],
        out_specs=pl.BlockSpec((None, tq, vw), lambda bi, i: (bi, i, 0)),
        out_shape=jax.ShapeDtypeStruct((b, s, vw), BF16),
        compiler_params=_cparams(("parallel", "arbitrary")),
        name="mla_attention",
    )(q, k, v)


def _outproj_body(x_ref, oa_ref, bb_ref, bc_ref, bh_ref, bcp_ref, bhp_ref, bcn_ref, bhn_ref, oc_ref,
                  o0_ref, o1_ref, o2_ref, l0_ref, l1_ref, l2_ref, w_ref, cw_ref, gn_ref, gate_ref,
                  out_ref, conv_ref, *, tm):
    i = pl.program_id(1)
    pad = 8
    first = (i > 0).astype(F32)
    final = (i < pl.num_programs(1) - 1).astype(F32)
    conv_ref[pad:pad + tm, :] = bc_ref[...].astype(F32) * bh_ref[...].astype(F32)
    conv_ref[pad - 1:pad, :] = (bcp_ref[BF16_ROWS - 1:, :].astype(F32)
                                * bhp_ref[BF16_ROWS - 1:, :].astype(F32)) * first
    conv_ref[pad + tm:pad + tm + 1, :] = (bcn_ref[:1, :].astype(F32) * bhn_ref[:1, :].astype(F32)) * final
    cw = cw_ref[...]
    conv = (conv_ref[pad - 1:pad - 1 + tm, :] * cw[0:1] + conv_ref[pad:pad + tm, :] * cw[1:2]
            + conv_ref[pad + 1:pad + 1 + tm, :] * cw[2:3])
    ob = (bb_ref[...].astype(F32) * conv).astype(BF16)
    l0, l1, l2 = l0_ref[...], l1_ref[...], l2_ref[...]
    mx = jnp.maximum(jnp.maximum(l0, l1), l2)
    e0, e1, e2 = jnp.exp(l0 - mx), jnp.exp(l1 - mx), jnp.exp(l2 - mx)
    od = ((o0_ref[...].astype(F32) * e0 + o1_ref[...].astype(F32) * e1 + o2_ref[...].astype(F32) * e2)
          / (e0 + e1 + e2)).astype(BF16)
    wd = 2 * LANES
    y = jnp.dot(oa_ref[...], w_ref[0:wd, :], preferred_element_type=F32)
    y = y + jnp.dot(ob, w_ref[wd:2 * wd, :], preferred_element_type=F32)
    y = y + jnp.dot(oc_ref[...], w_ref[2 * wd:3 * wd, :], preferred_element_type=F32)
    y = y + jnp.dot(od, w_ref[3 * wd:4 * wd, :], preferred_element_type=F32)
    out_ref[...] = x_ref[...] + gate_ref[...] * _rms(y, gn_ref[...])


def _outproj(x, oa, qkv, oc, od, lse, w, cw, gn, gate, tm=512):
    b, s, d = x.shape
    wd = 2 * LANES
    hb = tm // BF16_ROWS
    last = s // BF16_ROWS - 1

    def rows(width, off=0):
        return pl.BlockSpec((None, tm, width), lambda bi, i: (bi, i, off // width))

    def halo_prev(off):
        return pl.BlockSpec((None, BF16_ROWS, wd), lambda bi, i: (bi, jnp.maximum(i * hb - 1, 0), off // wd))

    def halo_next(off):
        return pl.BlockSpec((None, BF16_ROWS, wd), lambda bi, i: (bi, jnp.minimum((i + 1) * hb, last), off // wd))

    def const(shape):
        return pl.BlockSpec(shape, lambda bi, i: (0,) * len(shape))

    return pl.pallas_call(
        functools.partial(_outproj_body, tm=tm),
        grid=(b, s // tm),
        in_specs=[rows(d), rows(wd), rows(wd, OFF_BB), rows(wd, OFF_BC), rows(wd, OFF_BH),
                  halo_prev(OFF_BC), halo_prev(OFF_BH), halo_next(OFF_BC), halo_next(OFF_BH),
                  rows(wd), rows(wd), rows(wd), rows(wd), rows(wd), rows(wd), rows(wd),
                  const(w.shape), const(cw.shape), const(gn.shape),
                  pl.BlockSpec((None, 1, d), lambda bi, i: (bi, 0, 0))],
        out_specs=rows(d),
        out_shape=jax.ShapeDtypeStruct((b, s, d), F32),
        scratch_shapes=[pltpu.VMEM((tm + 16, wd), F32)],
        compiler_params=_cparams(("parallel", "arbitrary")),
        name="outproj",
    )(x, oa, qkv, qkv, qkv, qkv, qkv, qkv, qkv, oc, od[0], od[1], od[2], lse[0], lse[1], lse[2],
      w, cw, gn, gate)


def _ffn_body(x_ref, xp_ref, xn_ref, g_ref, sc_ref, sh_ref, gate_ref, gy_ref, wup_ref, cw_ref, wd_ref, out_ref,
              h_ref, u_ref, a_ref, *, tm, tn):
    i = pl.program_id(1)
    pad = BF16_ROWS
    ff = wd_ref.shape[0]

    def mod(v):
        return _rms(v, g_ref[...]) * (1.0 + sc_ref[...]) + sh_ref[...]

    first = (i > 0).astype(F32)
    final = (i < pl.num_programs(1) - 1).astype(F32)
    h_ref[0:pad, :] = (mod(xp_ref[...]) * first).astype(BF16)
    h_ref[pad:pad + tm, :] = mod(x_ref[...]).astype(BF16)
    h_ref[pad + tm:, :] = (mod(xn_ref[...]) * final).astype(BF16)
    h = h_ref[...]

    def conv(slot, half, c0):
        lanes = slice(half * tn, (half + 1) * tn)
        cw = cw_ref[:, c0:c0 + tn]
        return (u_ref[slot, pad - 1:pad - 1 + tm, lanes] * cw[0:1] + u_ref[slot, pad:pad + tm, lanes] * cw[1:2]
                + u_ref[slot, pad + 1:pad + 1 + tm, lanes] * cw[2:3])

    for j in range(ff // tn):
        slot = j % 2
        c0 = j * tn
        u_ref[slot, :, 0:tn] = jnp.dot(h, wup_ref[:, c0:c0 + tn], preferred_element_type=F32)
        u_ref[slot, :, tn:2 * tn] = jnp.dot(h, wup_ref[:, ff + c0:ff + c0 + tn], preferred_element_type=F32)
        a = jax.nn.gelu(conv(slot, 0, c0), approximate=True) * conv(slot, 1, ff + c0)
        a_ref[:, c0:c0 + tn] = a.astype(BF16)

    y = jnp.dot(a_ref[...], wd_ref[...], preferred_element_type=F32)
    out_ref[...] = x_ref[...] + gate_ref[...] * _rms(y, gy_ref[...])


def _ffn(x, g, sc, sh, gate, gy, w_up, cw, w_down, tm=512, tn=256):
    b, s, d = x.shape
    ff = w_down.shape[0]
    hb = tm // BF16_ROWS
    last = s // BF16_ROWS - 1

    def const(shape, **kw):
        return pl.BlockSpec(shape, lambda bi, i: (0,) * len(shape), **kw)

    def per_batch():
        return pl.BlockSpec((None, 1, d), lambda bi, i: (bi, 0, 0))

    resident = dict(pipeline_mode=pl.Buffered(1))
    return pl.pallas_call(
        functools.partial(_ffn_body, tm=tm, tn=tn),
        grid=(b, s // tm),
        in_specs=[pl.BlockSpec((None, tm, d), lambda bi, i: (bi, i, 0)),
                  pl.BlockSpec((None, BF16_ROWS, d), lambda bi, i: (bi, jnp.maximum(i * hb - 1, 0), 0)),
                  pl.BlockSpec((None, BF16_ROWS, d), lambda bi, i: (bi, jnp.minimum((i + 1) * hb, last), 0)),
                  const(g.shape), per_batch(), per_batch(), per_batch(), const(gy.shape),
                  const(w_up.shape, **resident), const(cw.shape), const(w_down.shape, **resident)],
        out_specs=pl.BlockSpec((None, tm, d), lambda bi, i: (bi, i, 0)),
        out_shape=jax.ShapeDtypeStruct((b, s, d), F32),
        scratch_shapes=[pltpu.VMEM((tm + 2 * BF16_ROWS, d), BF16),
                        pltpu.VMEM((2, tm + 2 * BF16_ROWS, 2 * tn), F32),
                        pltpu.VMEM((tm, ff), BF16)],
        compiler_params=_cparams(("parallel", "arbitrary")),
        name="ffn",
    )(x, x, x, g, sc, sh, gate, gy, w_up, cw, w_down)


def _pack_w_in(w_in):
    nl, d, _ = w_in.shape
    hd = HEAD_DIM
    aq = w_in[..., 0:256]
    aq = jnp.concatenate([aq[..., 0:hd], aq[..., 2 * hd:3 * hd], aq[..., hd:2 * hd], aq[..., 3 * hd:]], axis=-1)
    ckr = w_in[..., 1664:1696]
    ckr = jnp.concatenate([jnp.zeros((nl, d, C_NOPE_DIM), w_in.dtype), ckr,
                           jnp.zeros((nl, d, LANES - C_NOPE_DIM - C_ROPE_DIM), w_in.dtype)], axis=-1)
    return jnp.concatenate([aq, w_in[..., 256:1664], ckr, w_in[..., 1696:]], axis=-1).astype(BF16)


def _pack_w_uq(w):
    nl, r, _ = w.shape
    half = C_ROPE_DIM // 2
    w = w.reshape(nl, r, C_HEADS, C_NOPE_DIM + C_ROPE_DIM)
    nope, r1, r2 = w[..., :C_NOPE_DIM], w[..., C_NOPE_DIM:C_NOPE_DIM + half], w[..., C_NOPE_DIM + half:]
    z_tail = jnp.zeros((nl, r, C_HEADS, LANES - C_NOPE_DIM - C_ROPE_DIM), w.dtype)
    plain = jnp.concatenate([nope, r1, r2, z_tail], axis=-1).reshape(nl, r, C_HEADS * LANES)
    swapped = jnp.concatenate([jnp.zeros_like(nope), r2, r1, z_tail], axis=-1).reshape(nl, r, C_HEADS * LANES)
    return jnp.concatenate([plain, swapped], axis=-1).astype(BF16)


def _pack_w_ukv(w):
    nl, r, _ = w.shape
    w = w.reshape(nl, r, C_HEADS, C_NOPE_DIM + C_V_DIM)
    kn = jnp.concatenate([w[..., :C_NOPE_DIM], jnp.zeros((nl, r, C_HEADS, LANES - C_NOPE_DIM), w.dtype)], axis=-1)
    return jnp.concatenate([kn.reshape(nl, r, C_HEADS * LANES),
                            w[..., C_NOPE_DIM:].reshape(nl, r, C_HEADS * C_V_DIM)], axis=-1).astype(BF16)


def _pack_w_out(w_out):
    hd = HEAD_DIM
    return jnp.concatenate([w_out[:, 0:hd], w_out[:, 2 * hd:3 * hd], w_out[:, hd:2 * hd], w_out[:, 3 * hd:]],
                           axis=1).astype(BF16)


def _rope_tables(positions):
    half = C_ROPE_DIM // 2
    inv_freq = ROPE_THETA ** (-jnp.arange(half, dtype=jnp.float32) / half)
    ang = positions.astype(jnp.float32)[..., None] * inv_freq
    cos, sin = jnp.cos(ang), jnp.sin(ang)
    lead = positions.shape + (C_NOPE_DIM,)
    tail = positions.shape + (LANES - C_NOPE_DIM - C_ROPE_DIM,)
    cos_t = jnp.concatenate([jnp.ones(lead, F32), cos, cos, jnp.zeros(tail, F32)], axis=-1)
    sin_t = jnp.concatenate([jnp.zeros(lead, F32), -sin, sin, jnp.zeros(tail, F32)], axis=-1)
    return cos_t, sin_t


def _swap_matrix():
    half = C_ROPE_DIM // 2
    idx = jnp.arange(LANES)
    src = jnp.where((idx >= C_NOPE_DIM) & (idx < C_NOPE_DIM + half), idx + half,
                    jnp.where((idx >= C_NOPE_DIM + half) & (idx < C_NOPE_DIM + C_ROPE_DIM), idx - half, idx))
    return (idx[:, None] == src[None, :]).astype(BF16)


def kernel(x, c, positions, rel_bias, w_mod, b_mod, norm_g, w_in, a_sink, b_conv, c_norm_q, c_norm_kv,
           c_w_uq, c_w_ukv, w_out, w_up, ffn_conv, w_down):
    b, s, d = x.shape
    depth = w_in.shape[0]

    bias_a = _band_bias(rel_bias[:, :A_Q_HEADS], A_RADIUS, 1)
    bias_d = [_band_bias(rel_bias[:, A_Q_HEADS + i * D_HEADS:A_Q_HEADS + (i + 1) * D_HEADS], (w // 2) // dil, dil)
              for i, (w, dil) in enumerate(D_PATTERNS)]
    cos_t, sin_t = _rope_tables(positions)
    swap = _swap_matrix()

    w_in_p = _pack_w_in(w_in)
    w_uq_p = _pack_w_uq(c_w_uq)
    w_ukv_p = _pack_w_ukv(c_w_ukv)
    w_out_p = _pack_w_out(w_out)
    w_up_b = w_up.astype(BF16)
    w_down_b = w_down.astype(BF16)

    mod = _modulation(c, w_mod, b_mod)

    for l in range(depth):
        sh1, sc1, g1, sh2, sc2, g2 = (mod[l, k].reshape(b, 1, d) for k in range(6))
        gains = norm_g[l]

        qkv, qkv_d4, qkv_d16 = _inproj(x, gains[0:1], sc1, sh1, w_in_p[l])

        oa = _band_attention(qkv, bias_a, a_sink[l], dil=1, radius=A_RADIUS, tq=256, q_off=OFF_AQ, k_off=OFF_AK,
                             v_off=OFF_AV, kv_w=LANES, heads=((0, 2), (1, 3)), has_lse=False)[0]

        qc, kc, vc = _mla_prep(qkv, cos_t, sin_t, c_norm_q[l][None], c_norm_kv[l][None], w_uq_p[l], w_ukv_p[l],
                               swap)
        oc = _mla_attn(qc, kc, vc)

        od, lse = [], []
        for i, ((w, dil), src) in enumerate(zip(D_PATTERNS, (qkv, qkv_d4, qkv_d16))):
            off = OFF_D if dil == 1 else 0
            o_i, l_i = _band_attention(src, bias_d[i], None, dil=dil, radius=(w // 2) // dil, tq=256, q_off=off,
                                       k_off=off + 256, v_off=off + 512, kv_w=2 * LANES,
                                       heads=((0, 1), (2, 3)), has_lse=True)
            od.append(o_i)
            lse.append(l_i)

        x = _outproj(x, oa, qkv, oc, od, lse, w_out_p[l], b_conv[l], gains[1:2], g1)
        x = _ffn(x, gains[2:3], sc2, sh2, g2, gains[3:4], w_up_b[l], ffn_conv[l], w_down_b[l])
    return x
```

```python
import functools
import math

import jax
import jax.numpy as jnp
from jax import lax
from jax.experimental import pallas as pl
from jax.experimental.pallas import tpu as pltpu

D_MODEL = 1024
HEAD_DIM = 64
A_Q_HEADS = 4
A_KV_HEADS = 2
A_RADIUS = 128
B_WIDTH = 256
C_HEADS = 4
C_Q_RANK = 256
C_KV_RANK = 128
C_NOPE_DIM = 64
C_ROPE_DIM = 32
C_V_DIM = 64
ROPE_THETA = 10000.0
D_HEADS = 4
D_PATTERNS = ((128, 1), (512, 4), (2048, 16))
REL_BUCKETS = 32
REL_MAX_DISTANCE = 1024
D_FF = 2816
EPS = 1e-6
NEG = -1e30

LANES = 128
BF16_ROWS = 16
SUB_Q = 128
VMEM_LIMIT = 48 * 1024 * 1024

IN_COLS = 4096
OFF_AQ, OFF_AK, OFF_AV = 0, 256, 384
OFF_BB, OFF_BC, OFF_BH = 512, 768, 1024
OFF_CQ, OFF_CKV, OFF_CKR = 1280, 1536, 1664
OFF_D = 1792
A_HEAD_ORDER = (0, 2, 1, 3)
PATTERN_COLS = 3 * D_HEADS * HEAD_DIM
MAIN_COLS = OFF_D + PATTERN_COLS
COL_CHUNK = 512

F32 = jnp.float32
BF16 = jnp.bfloat16


def _cparams(sem):
    return pltpu.CompilerParams(dimension_semantics=sem, vmem_limit_bytes=VMEM_LIMIT)


def _rms(v, g):
    return v * lax.rsqrt(jnp.mean(v * v, axis=-1, keepdims=True) + EPS) * g


def _mod_body(c_ref, w_ref, b_ref, o_ref):
    c = c_ref[...]
    act = c / (1.0 + jnp.exp(-c))
    o_ref[...] = jnp.dot(act, w_ref[...], preferred_element_type=F32,
                         precision=lax.Precision.HIGHEST) + b_ref[...]


def _modulation(c, w_mod, b_mod):
    nl, d, _ = w_mod.shape
    b = c.shape[0]
    return pl.pallas_call(
        _mod_body,
        grid=(nl, 6),
        in_specs=[pl.BlockSpec((b, d), lambda l, k: (0, 0)),
                  pl.BlockSpec((None, d, d), lambda l, k: (l, 0, k)),
                  pl.BlockSpec((None, 1, d), lambda l, k: (l, 0, k))],
        out_specs=pl.BlockSpec((None, None, b, d), lambda l, k: (l, k, 0, 0)),
        out_shape=jax.ShapeDtypeStruct((nl, 6, b, d), F32),
        compiler_params=_cparams(("parallel", "parallel")),
        name="modulation",
    )(c, w_mod, b_mod.reshape(nl, 1, 6 * d))


def _inproj_body(x_ref, g_ref, sc_ref, sh_ref, w_ref, om_ref, o4_ref, o16_ref, hf_ref, *, tm):
    h = _rms(x_ref[...], g_ref[...]) * (1.0 + sc_ref[...]) + sh_ref[...]
    nc = h.shape[-1] // LANES
    for c in range(nc):
        hf_ref[c] = h[:, c * LANES:(c + 1) * LANES]
    hb = h.astype(BF16)
    for c0 in range(0, MAIN_COLS, COL_CHUNK):
        om_ref[:, c0:c0 + COL_CHUNK] = jnp.dot(hb, w_ref[:, c0:c0 + COL_CHUNK],
                                               preferred_element_type=F32).astype(BF16)
    for o_ref, dil, c0 in ((o4_ref, 4, MAIN_COLS), (o16_ref, 16, MAIN_COLS + PATTERN_COLS)):
        n = tm // dil
        hp = jnp.concatenate(
            [jnp.concatenate([hf_ref[c, pl.ds(r, n, stride=dil), :] for r in range(dil)], axis=0)
             for c in range(nc)], axis=1).astype(BF16)
        res = jnp.dot(hp, w_ref[:, c0:c0 + PATTERN_COLS], preferred_element_type=F32).astype(BF16)
        for r in range(dil):
            o_ref[r] = res[r * n:(r + 1) * n]


def _inproj(x, g, sc, sh, w, tm=512):
    b, s, d = x.shape
    d4, d16 = D_PATTERNS[1][1], D_PATTERNS[2][1]
    return pl.pallas_call(
        functools.partial(_inproj_body, tm=tm),
        grid=(b, s // tm),
        in_specs=[pl.BlockSpec((None, tm, d), lambda bi, i: (bi, i, 0)),
                  pl.BlockSpec((1, d), lambda bi, i: (0, 0)),
                  pl.BlockSpec((None, 1, d), lambda bi, i: (bi, 0, 0)),
                  pl.BlockSpec((None, 1, d), lambda bi, i: (bi, 0, 0)),
                  pl.BlockSpec(w.shape, lambda bi, i: (0, 0), pipeline_mode=pl.Buffered(1))],
        out_specs=[pl.BlockSpec((None, tm, MAIN_COLS), lambda bi, i: (bi, i, 0)),
                   pl.BlockSpec((None, d4, tm // d4, PATTERN_COLS), lambda bi, i: (bi, 0, i, 0)),
                   pl.BlockSpec((None, d16, tm // d16, PATTERN_COLS), lambda bi, i: (bi, 0, i, 0))],
        out_shape=[jax.ShapeDtypeStruct((b, s, MAIN_COLS), BF16),
                   jax.ShapeDtypeStruct((b, d4, s // d4, PATTERN_COLS), BF16),
                   jax.ShapeDtypeStruct((b, d16, s // d16, PATTERN_COLS), BF16)],
        scratch_shapes=[pltpu.VMEM((d // LANES, tm, LANES), F32)],
        compiler_params=_cparams(("parallel", "parallel")),
        name="inproj",
    )(x, g, sc, sh, w)


def _t5_bucket(rel):
    half = REL_BUCKETS // 2
    max_exact = half // 2
    n = jnp.abs(rel)
    n_f = jnp.maximum(n, 1).astype(jnp.float32)
    large = max_exact + (jnp.log(n_f / max_exact) / math.log(REL_MAX_DISTANCE / max_exact)
                         * (half - max_exact)).astype(jnp.int32)
    large = jnp.minimum(large, half - 1)
    return jnp.where(rel > 0, half, 0) + jnp.where(n < max_exact, n, large)


def _band_bias(table, radius, stride):
    qi = jnp.arange(SUB_Q)[:, None]
    kj = jnp.arange(SUB_Q + 2 * radius)[None, :]
    off = kj - radius - qi
    onehot = (_t5_bucket(off * stride)[..., None] == jnp.arange(REL_BUCKETS)).astype(F32)
    bias = jnp.einsum('qkb,bh->hqk', onehot, table.astype(F32), precision=lax.Precision.HIGHEST)
    bias = jnp.where((jnp.abs(off) <= radius)[None], bias, NEG)
    return bias.reshape(-1, bias.shape[-1])


def _band_body(*refs, radius, tq, seq, kv_w, has_sink, has_lse):
    q_ref, kp_ref, kc_ref, kn_ref, vp_ref, vc_ref, vn_ref, bias_ref = refs[:8]
    rest = refs[8:]
    if has_sink:
        sink_ref, rest = rest[0], rest[1:]
    o_ref = rest[0]
    lse_ref = rest[1] if has_lse else None

    i = pl.program_id(2)
    nk = SUB_Q + 2 * radius
    nh = 4
    per = kv_w // HEAD_DIM
    kcat = jnp.concatenate([kp_ref[...], kc_ref[...], kn_ref[...]], axis=0)
    vcat = jnp.concatenate([vp_ref[...], vc_ref[...], vn_ref[...]], axis=0)
    lane = lax.broadcasted_iota(jnp.int32, (1, kv_w), 1)
    hmask = [(lane >= HEAD_DIM * (h % per)) & (lane < HEAD_DIM * (h % per + 1)) for h in range(nh)]
    scale = HEAD_DIM ** -0.5

    def pick(rows):
        out = rows[(per - 1) * SUB_Q:per * SUB_Q]
        for h in range(per - 2, -1, -1):
            out = jnp.where(hmask[h], rows[h * SUB_Q:(h + 1) * SUB_Q], out)
        return out

    for sb in range(tq // SUB_Q):
        r0 = sb * SUB_Q
        qs = q_ref[r0:r0 + SUB_Q, :] * scale
        ks = kcat[r0:r0 + nk]
        vs = vcat[r0:r0 + nk]
        kpos = i * tq + (r0 - radius) + lax.broadcasted_iota(jnp.int32, (1, nk), 1)
        inb = (kpos >= 0) & (kpos < seq)
        parts = []
        for h in range(nh):
            src = qs[:, (h // per) * kv_w:(h // per + 1) * kv_w]
            parts.append(jnp.where(hmask[h], src, jnp.zeros_like(src)))
        qst = jnp.concatenate(parts, axis=0)
        sc = lax.dot_general(qst, ks, (((1,), (1,)), ((), ())), preferred_element_type=F32)
        sc = jnp.where(inb, sc + bias_ref[...], NEG)
        m = jnp.max(sc, axis=-1, keepdims=True)
        if has_sink:
            m = jnp.maximum(m, jnp.max(sink_ref[...], axis=-1, keepdims=True))
        p = jnp.exp(sc - m)
        den = jnp.sum(p, axis=-1, keepdims=True)
        if has_sink:
            den = den + jnp.max(jnp.exp(sink_ref[...] - m), axis=-1, keepdims=True)
        o = jnp.dot(p.astype(BF16), vs, preferred_element_type=F32) / den
        lse = jnp.broadcast_to(m + jnp.log(den), o.shape) if has_lse else None
        for g in range(nh // per):
            rows = slice(g * per * SUB_Q, (g + 1) * per * SUB_Q)
            o_ref[r0:r0 + SUB_Q, g * kv_w:(g + 1) * kv_w] = pick(o[rows]).astype(BF16)
            if has_lse:
                lse_ref[r0:r0 + SUB_Q, g * kv_w:(g + 1) * kv_w] = pick(lse[rows])


def _band_attention(qkv, bias, sink, *, dil, radius, tq, q_off, k_off, v_off, kv_w, has_lse):
    b = qkv.shape[0]
    seq = qkv.shape[-2]
    s = seq * dil
    tq = min(tq, seq)
    qw = 2 * LANES
    steps = tq // radius
    last = seq // radius - 1
    if dil == 1:
        qkv = qkv.reshape(b, 1, seq, qkv.shape[-1])

    def spec(rows, width, off, row_index):
        return pl.BlockSpec((None, None, rows, width), lambda bi, r, i: (bi, r, row_index(i), off // width))

    def cur(i):
        return i

    def prev(i):
        return jnp.maximum(i * steps - 1, 0)

    def nxt(i):
        return jnp.minimum((i + 1) * steps, last)

    in_specs = [spec(tq, qw, q_off, cur)]
    for off in (k_off, v_off):
        in_specs += [spec(radius, kv_w, off, prev), spec(tq, kv_w, off, cur), spec(radius, kv_w, off, nxt)]
    in_specs.append(pl.BlockSpec(bias.shape, lambda bi, r, i: (0, 0)))
    args = [qkv] * 7 + [bias]
    if sink is not None:
        in_specs.append(pl.BlockSpec(sink.shape, lambda bi, r, i: (0, 0)))
        args.append(sink)

    o_spec = pl.BlockSpec((None, tq, qw), lambda bi, r, i: (bi, i, r))
    out_shape = [jax.ShapeDtypeStruct((b, seq, dil * qw), BF16)]
    out_specs = [o_spec]
    if has_lse:
        out_shape.append(jax.ShapeDtypeStruct((b, seq, dil * qw), F32))
        out_specs.append(o_spec)

    outs = pl.pallas_call(
        functools.partial(_band_body, radius=radius, tq=tq, seq=seq, kv_w=kv_w,
                          has_sink=sink is not None, has_lse=has_lse),
        grid=(b, dil, seq // tq),
        in_specs=in_specs,
        out_specs=out_specs,
        out_shape=out_shape,
        compiler_params=_cparams(("parallel", "parallel", "parallel")),
        name=f"band_attention_d{dil}_r{radius}",
    )(*args)
    return [o.reshape(b, s, qw) for o in outs]


def _mla_prep_body(cq_ref, ckv_ref, ckr_ref, cos_ref, sin_ref, gq_ref, gkv_ref, wq_ref, wkv_ref, swap_ref,
                   q_ref, k_ref, v_ref):
    cs = cos_ref[...]
    sn = sin_ref[...]
    scale = (C_NOPE_DIM + C_ROPE_DIM) ** -0.5
    nq = _rms(cq_ref[...].astype(F32), gq_ref[...]).astype(BF16)
    qq = jnp.dot(nq, wq_ref[...], preferred_element_type=F32)
    nkv = _rms(ckv_ref[...].astype(F32), gkv_ref[...]).astype(BF16)
    kvv = jnp.dot(nkv, wkv_ref[...], preferred_element_type=F32)
    kr = ckr_ref[...]
    kr_sw = jnp.dot(kr, swap_ref[...], preferred_element_type=F32)
    k_rope = kr.astype(F32) * cs + kr_sw * sn
    hw = C_HEADS * LANES
    for h in range(C_HEADS):
        a = qq[:, h * LANES:(h + 1) * LANES]
        a_sw = qq[:, hw + h * LANES:hw + (h + 1) * LANES]
        q_ref[:, h * LANES:(h + 1) * LANES] = ((a * cs + a_sw * sn) * scale).astype(BF16)
        k_ref[:, h * LANES:(h + 1) * LANES] = (kvv[:, h * LANES:(h + 1) * LANES] + k_rope).astype(BF16)
    v_ref[...] = kvv[:, hw:].astype(BF16)


def _mla_prep(qkv, cos_t, sin_t, gq, gkv, wq, wkv, swap, tm=512):
    b, s, n = qkv.shape
    hw = C_HEADS * LANES
    vw = C_HEADS * C_V_DIM

    def const(shape):
        return pl.BlockSpec(shape, lambda bi, i: (0,) * len(shape))

    def rows(width, off=0):
        return pl.BlockSpec((None, tm, width), lambda bi, i: (bi, i, off // width))

    return pl.pallas_call(
        _mla_prep_body,
        grid=(b, s // tm),
        in_specs=[rows(C_Q_RANK, OFF_CQ), rows(LANES, OFF_CKV), rows(LANES, OFF_CKR),
                  rows(LANES), rows(LANES),
                  const(gq.shape), const(gkv.shape), const(wq.shape), const(wkv.shape), const(swap.shape)],
        out_specs=[rows(hw), rows(hw), rows(vw)],
        out_shape=[jax.ShapeDtypeStruct((b, s, hw), BF16), jax.ShapeDtypeStruct((b, s, hw), BF16),
                   jax.ShapeDtypeStruct((b, s, vw), BF16)],
        compiler_params=_cparams(("parallel", "parallel")),
        name="mla_prep",
    )(qkv, qkv, qkv, cos_t, sin_t, gq, gkv, wq, wkv, swap)


def _mla_attn_body(q_ref, k_ref, v_ref, o_ref):
    lo = lax.broadcasted_iota(jnp.int32, (1, LANES), 1) < C_V_DIM
    for g in range(C_HEADS // 2):
        acc = None
        for half in range(2):
            h = 2 * g + half
            q = q_ref[:, h * LANES:(h + 1) * LANES]
            k = k_ref[:, h * LANES:(h + 1) * LANES]
            sc = lax.dot_general(q, k, (((1,), (1,)), ((), ())), preferred_element_type=F32)
            m = jnp.max(sc, axis=-1, keepdims=True)
            p = jnp.exp(sc - m)
            den = jnp.sum(p, axis=-1, keepdims=True)
            o = jnp.dot(p.astype(BF16), v_ref[:, g * LANES:(g + 1) * LANES], preferred_element_type=F32) / den
            acc = o if acc is None else jnp.where(lo, acc, o)
        o_ref[:, g * LANES:(g + 1) * LANES] = acc.astype(BF16)


def _mla_attn(q, k, v, tq=256):
    b, s, hw = q.shape
    vw = v.shape[-1]
    return pl.pallas_call(
        _mla_attn_body,
        grid=(b, s // tq),
        in_specs=[pl.BlockSpec((None, tq, hw), lambda bi, i: (bi, i, 0)),
                  pl.BlockSpec((None, s, hw), lambda bi, i: (bi, 0, 0)),
                  pl.BlockSpec((None, s, vw), lambda bi, i: (bi, 0, 0))],
        out_specs=pl.BlockSpec((None, tq, vw), lambda bi, i: (bi, i, 0)),
        out_shape=jax.ShapeDtypeStruct((b, s, vw), BF16),
        compiler_params=_cparams(("parallel", "arbitrary")),
        name="mla_attention",
    )(q, k, v)


def _outproj_body(x_ref, oa_ref, bb_ref, bc_ref, bh_ref, bcp_ref, bhp_ref, bcn_ref, bhn_ref, oc_ref,
                  o0_ref, o1_ref, o2_ref, l0_ref, l1_ref, l2_ref, w_ref, cw_ref, gn_ref, gate_ref,
                  out_ref, conv_ref, *, tm):
    i = pl.program_id(1)
    pad = 8
    first = (i > 0).astype(F32)
    final = (i < pl.num_programs(1) - 1).astype(F32)
    conv_ref[pad:pad + tm, :] = bc_ref[...].astype(F32) * bh_ref[...].astype(F32)
    conv_ref[pad - 1:pad, :] = (bcp_ref[BF16_ROWS - 1:, :].astype(F32)
                                * bhp_ref[BF16_ROWS - 1:, :].astype(F32)) * first
    conv_ref[pad + tm:pad + tm + 1, :] = (bcn_ref[:1, :].astype(F32) * bhn_ref[:1, :].astype(F32)) * final
    cw = cw_ref[...]
    conv = (conv_ref[pad - 1:pad - 1 + tm, :] * cw[0:1] + conv_ref[pad:pad + tm, :] * cw[1:2]
            + conv_ref[pad + 1:pad + 1 + tm, :] * cw[2:3])
    ob = (bb_ref[...].astype(F32) * conv).astype(BF16)
    l0, l1, l2 = l0_ref[...], l1_ref[...], l2_ref[...]
    mx = jnp.maximum(jnp.maximum(l0, l1), l2)
    e0, e1, e2 = jnp.exp(l0 - mx), jnp.exp(l1 - mx), jnp.exp(l2 - mx)
    od = ((o0_ref[...].astype(F32) * e0 + o1_ref[...].astype(F32) * e1 + o2_ref[...].astype(F32) * e2)
          / (e0 + e1 + e2)).astype(BF16)
    wd = 2 * LANES
    y = jnp.dot(oa_ref[...], w_ref[0:wd, :], preferred_element_type=F32)
    y = y + jnp.dot(ob, w_ref[wd:2 * wd, :], preferred_element_type=F32)
    y = y + jnp.dot(oc_ref[...], w_ref[2 * wd:3 * wd, :], preferred_element_type=F32)
    y = y + jnp.dot(od, w_ref[3 * wd:4 * wd, :], preferred_element_type=F32)
    out_ref[...] = x_ref[...] + gate_ref[...] * _rms(y, gn_ref[...])


def _outproj(x, oa, qkv, oc, od, lse, w, cw, gn, gate, tm=512):
    b, s, d = x.shape
    wd = 2 * LANES
    hb = tm // BF16_ROWS
    last = s // BF16_ROWS - 1

    def rows(width, off=0):
        return pl.BlockSpec((None, tm, width), lambda bi, i: (bi, i, off // width))

    def halo_prev(off):
        return pl.BlockSpec((None, BF16_ROWS, wd), lambda bi, i: (bi, jnp.maximum(i * hb - 1, 0), off // wd))

    def halo_next(off):
        return pl.BlockSpec((None, BF16_ROWS, wd), lambda bi, i: (bi, jnp.minimum((i + 1) * hb, last), off // wd))

    def const(shape):
        return pl.BlockSpec(shape, lambda bi, i: (0,) * len(shape))

    return pl.pallas_call(
        functools.partial(_outproj_body, tm=tm),
        grid=(b, s // tm),
        in_specs=[rows(d), rows(wd), rows(wd, OFF_BB), rows(wd, OFF_BC), rows(wd, OFF_BH),
                  halo_prev(OFF_BC), halo_prev(OFF_BH), halo_next(OFF_BC), halo_next(OFF_BH),
                  rows(wd), rows(wd), rows(wd), rows(wd), rows(wd), rows(wd), rows(wd),
                  const(w.shape), const(cw.shape), const(gn.shape),
                  pl.BlockSpec((None, 1, d), lambda bi, i: (bi, 0, 0))],
        out_specs=rows(d),
        out_shape=jax.ShapeDtypeStruct((b, s, d), F32),
        scratch_shapes=[pltpu.VMEM((tm + 16, wd), F32)],
        compiler_params=_cparams(("parallel", "arbitrary")),
        name="outproj",
    )(x, oa, qkv, qkv, qkv, qkv, qkv, qkv, qkv, oc, od[0], od[1], od[2], lse[0], lse[1], lse[2],
      w, cw, gn, gate)


def _ffn_body(x_ref, xp_ref, xn_ref, g_ref, sc_ref, sh_ref, gate_ref, gy_ref, wup_ref, cw_ref, wd_ref, out_ref,
              h_ref, u_ref, a_ref, *, tm, tn):
    i = pl.program_id(1)
    pad = BF16_ROWS
    ff = wd_ref.shape[0]

    def mod(v):
        return _rms(v, g_ref[...]) * (1.0 + sc_ref[...]) + sh_ref[...]

    first = (i > 0).astype(F32)
    final = (i < pl.num_programs(1) - 1).astype(F32)
    h_ref[0:pad, :] = (mod(xp_ref[...]) * first).astype(BF16)
    h_ref[pad:pad + tm, :] = mod(x_ref[...]).astype(BF16)
    h_ref[pad + tm:, :] = (mod(xn_ref[...]) * final).astype(BF16)
    h = h_ref[...]

    def conv(slot, half, c0):
        lanes = slice(half * tn, (half + 1) * tn)
        cw = cw_ref[:, c0:c0 + tn]
        return (u_ref[slot, pad - 1:pad - 1 + tm, lanes] * cw[0:1] + u_ref[slot, pad:pad + tm, lanes] * cw[1:2]
                + u_ref[slot, pad + 1:pad + 1 + tm, lanes] * cw[2:3])

    for j in range(ff // tn):
        slot = j % 2
        c0 = j * tn
        u_ref[slot, :, 0:tn] = jnp.dot(h, wup_ref[:, c0:c0 + tn], preferred_element_type=F32)
        u_ref[slot, :, tn:2 * tn] = jnp.dot(h, wup_ref[:, ff + c0:ff + c0 + tn], preferred_element_type=F32)
        a = jax.nn.gelu(conv(slot, 0, c0), approximate=True) * conv(slot, 1, ff + c0)
        a_ref[:, c0:c0 + tn] = a.astype(BF16)

    y = jnp.dot(a_ref[...], wd_ref[...], preferred_element_type=F32)
    out_ref[...] = x_ref[...] + gate_ref[...] * _rms(y, gy_ref[...])


def _ffn(x, g, sc, sh, gate, gy, w_up, cw, w_down, tm=512, tn=256):
    b, s, d = x.shape
    ff = w_down.shape[0]
    hb = tm // BF16_ROWS
    last = s // BF16_ROWS - 1

    def const(shape, **kw):
        return pl.BlockSpec(shape, lambda bi, i: (0,) * len(shape), **kw)

    def per_batch():
        return pl.BlockSpec((None, 1, d), lambda bi, i: (bi, 0, 0))

    resident = dict(pipeline_mode=pl.Buffered(1))
    return pl.pallas_call(
        functools.partial(_ffn_body, tm=tm, tn=tn),
        grid=(b, s // tm),
        in_specs=[pl.BlockSpec((None, tm, d), lambda bi, i: (bi, i, 0)),
                  pl.BlockSpec((None, BF16_ROWS, d), lambda bi, i: (bi, jnp.maximum(i * hb - 1, 0), 0)),
                  pl.BlockSpec((None, BF16_ROWS, d), lambda bi, i: (bi, jnp.minimum((i + 1) * hb, last), 0)),
                  const(g.shape), per_batch(), per_batch(), per_batch(), const(gy.shape),
                  const(w_up.shape, **resident), const(cw.shape), const(w_down.shape, **resident)],
        out_specs=pl.BlockSpec((None, tm, d), lambda bi, i: (bi, i, 0)),
        out_shape=jax.ShapeDtypeStruct((b, s, d), F32),
        scratch_shapes=[pltpu.VMEM((tm + 2 * BF16_ROWS, d), BF16),
                        pltpu.VMEM((2, tm + 2 * BF16_ROWS, 2 * tn), F32),
                        pltpu.VMEM((tm, ff), BF16)],
        compiler_params=_cparams(("parallel", "arbitrary")),
        name="ffn",
    )(x, x, x, g, sc, sh, gate, gy, w_up, cw, w_down)


def _pack_w_in(w_in):
    nl, d, _ = w_in.shape
    hd = HEAD_DIM
    aq = w_in[..., 0:256]
    aq = jnp.concatenate([aq[..., 0:hd], aq[..., 2 * hd:3 * hd], aq[..., hd:2 * hd], aq[..., 3 * hd:]], axis=-1)
    ckr = w_in[..., 1664:1696]
    ckr = jnp.concatenate([jnp.zeros((nl, d, C_NOPE_DIM), w_in.dtype), ckr,
                           jnp.zeros((nl, d, LANES - C_NOPE_DIM - C_ROPE_DIM), w_in.dtype)], axis=-1)
    return jnp.concatenate([aq, w_in[..., 256:1664], ckr, w_in[..., 1696:]], axis=-1).astype(BF16)


def _pack_w_uq(w):
    nl, r, _ = w.shape
    half = C_ROPE_DIM // 2
    w = w.reshape(nl, r, C_HEADS, C_NOPE_DIM + C_ROPE_DIM)
    nope, r1, r2 = w[..., :C_NOPE_DIM], w[..., C_NOPE_DIM:C_NOPE_DIM + half], w[..., C_NOPE_DIM + half:]
    z_tail = jnp.zeros((nl, r, C_HEADS, LANES - C_NOPE_DIM - C_ROPE_DIM), w.dtype)
    plain = jnp.concatenate([nope, r1, r2, z_tail], axis=-1).reshape(nl, r, C_HEADS * LANES)
    swapped = jnp.concatenate([jnp.zeros_like(nope), r2, r1, z_tail], axis=-1).reshape(nl, r, C_HEADS * LANES)
    return jnp.concatenate([plain, swapped], axis=-1).astype(BF16)


def _pack_w_ukv(w):
    nl, r, _ = w.shape
    w = w.reshape(nl, r, C_HEADS, C_NOPE_DIM + C_V_DIM)
    kn = jnp.concatenate([w[..., :C_NOPE_DIM], jnp.zeros((nl, r, C_HEADS, LANES - C_NOPE_DIM), w.dtype)], axis=-1)
    return jnp.concatenate([kn.reshape(nl, r, C_HEADS * LANES),
                            w[..., C_NOPE_DIM:].reshape(nl, r, C_HEADS * C_V_DIM)], axis=-1).astype(BF16)


def _pack_w_out(w_out):
    hd = HEAD_DIM
    return jnp.concatenate([w_out[:, 0:hd], w_out[:, 2 * hd:3 * hd], w_out[:, hd:2 * hd], w_out[:, 3 * hd:]],
                           axis=1).astype(BF16)


def _rope_tables(positions):
    half = C_ROPE_DIM // 2
    inv_freq = ROPE_THETA ** (-jnp.arange(half, dtype=jnp.float32) / half)
    ang = positions.astype(jnp.float32)[..., None] * inv_freq
    cos, sin = jnp.cos(ang), jnp.sin(ang)
    lead = positions.shape + (C_NOPE_DIM,)
    tail = positions.shape + (LANES - C_NOPE_DIM - C_ROPE_DIM,)
    cos_t = jnp.concatenate([jnp.ones(lead, F32), cos, cos, jnp.zeros(tail, F32)], axis=-1)
    sin_t = jnp.concatenate([jnp.zeros(lead, F32), -sin, sin, jnp.zeros(tail, F32)], axis=-1)
    return cos_t, sin_t


def _swap_matrix():
    half = C_ROPE_DIM // 2
    idx = jnp.arange(LANES)
    src = jnp.where((idx >= C_NOPE_DIM) & (idx < C_NOPE_DIM + half), idx + half,
                    jnp.where((idx >= C_NOPE_DIM + half) & (idx < C_NOPE_DIM + C_ROPE_DIM), idx - half, idx))
    return (idx[:, None] == src[None, :]).astype(BF16)


def kernel(x, c, positions, rel_bias, w_mod, b_mod, norm_g, w_in, a_sink, b_conv, c_norm_q, c_norm_kv,
           c_w_uq, c_w_ukv, w_out, w_up, ffn_conv, w_down):
    b, s, d = x.shape
    depth = w_in.shape[0]

    a_order = jnp.array(A_HEAD_ORDER)
    bias_a = _band_bias(rel_bias[:, :A_Q_HEADS][:, a_order], A_RADIUS, 1)
    bias_d = [_band_bias(rel_bias[:, A_Q_HEADS + i * D_HEADS:A_Q_HEADS + (i + 1) * D_HEADS], (w // 2) // dil, dil)
              for i, (w, dil) in enumerate(D_PATTERNS)]
    cos_t, sin_t = _rope_tables(positions)
    swap = _swap_matrix()

    w_in_p = _pack_w_in(w_in)
    w_uq_p = _pack_w_uq(c_w_uq)
    w_ukv_p = _pack_w_ukv(c_w_ukv)
    w_out_p = _pack_w_out(w_out)
    w_up_b = w_up.astype(BF16)
    w_down_b = w_down.astype(BF16)

    mod = _modulation(c, w_mod, b_mod)

    for l in range(depth):
        sh1, sc1, g1, sh2, sc2, g2 = (mod[l, k].reshape(b, 1, d) for k in range(6))
        gains = norm_g[l]

        qkv, qkv_d4, qkv_d16 = _inproj(x, gains[0:1], sc1, sh1, w_in_p[l])

        sink = jnp.broadcast_to(jnp.repeat(a_sink[l][a_order], SUB_Q)[:, None], (A_Q_HEADS * SUB_Q, LANES))
        oa = _band_attention(qkv, bias_a, sink, dil=1, radius=A_RADIUS, tq=512, q_off=OFF_AQ, k_off=OFF_AK,
                             v_off=OFF_AV, kv_w=LANES, has_lse=False)[0]

        qc, kc, vc = _mla_prep(qkv, cos_t, sin_t, c_norm_q[l][None], c_norm_kv[l][None], w_uq_p[l], w_ukv_p[l],
                               swap)
        oc = _mla_attn(qc, kc, vc)

        od, lse = [], []
        for i, ((w, dil), src) in enumerate(zip(D_PATTERNS, (qkv, qkv_d4, qkv_d16))):
            off = OFF_D if dil == 1 else 0
            o_i, l_i = _band_attention(src, bias_d[i], None, dil=dil, radius=(w // 2) // dil, tq=512, q_off=off,
                                       k_off=off + 256, v_off=off + 512, kv_w=2 * LANES, has_lse=True)
            od.append(o_i)
            lse.append(l_i)

        x = _outproj(x, oa, qkv, oc, od, lse, w_out_p[l], b_conv[l], gains[1:2], g1)
        x = _ffn(x, gains[2:3], sc2, sh2, g2, gains[3:4], w_up_b[l], ffn_conv[l], w_down_b[l])
    return x
```

```python
import functools
import math

import jax
import jax.numpy as jnp
from jax import lax
from jax.experimental import pallas as pl
from jax.experimental.pallas import tpu as pltpu

D_MODEL = 1024
HEAD_DIM = 64
A_Q_HEADS = 4
A_KV_HEADS = 2
A_RADIUS = 128
B_WIDTH = 256
C_HEADS = 4
C_Q_RANK = 256
C_KV_RANK = 128
C_NOPE_DIM = 64
C_ROPE_DIM = 32
C_V_DIM = 64
ROPE_THETA = 10000.0
D_HEADS = 4
D_PATTERNS = ((128, 1), (512, 4), (2048, 16))
REL_BUCKETS = 32
REL_MAX_DISTANCE = 1024
D_FF = 2816
EPS = 1e-6
NEG = -1e30

LANES = 128
BF16_ROWS = 16
A_SUB_Q = 128
D_SUB_Q = 128
VMEM_LIMIT = 48 * 1024 * 1024

IN_COLS = 4096
OFF_AQ, OFF_AK, OFF_AV = 0, 256, 384
OFF_BB, OFF_BC, OFF_BH = 512, 768, 1024
OFF_CQ, OFF_CKV, OFF_CKR = 1280, 1536, 1664
OFF_D = 1792
A_HEAD_ORDER = (0, 2, 1, 3)
PATTERN_COLS = 3 * D_HEADS * HEAD_DIM
MAIN_COLS = OFF_D + PATTERN_COLS
COL_CHUNK = 512

F32 = jnp.float32
BF16 = jnp.bfloat16


def _cparams(sem):
    return pltpu.CompilerParams(dimension_semantics=sem, vmem_limit_bytes=VMEM_LIMIT)


def _rms(v, g):
    return v * lax.rsqrt(jnp.mean(v * v, axis=-1, keepdims=True) + EPS) * g


def _mod_body(c_ref, w_ref, b_ref, o_ref):
    c = c_ref[...]
    act = c / (1.0 + jnp.exp(-c))
    o_ref[...] = jnp.dot(act, w_ref[...], preferred_element_type=F32,
                         precision=lax.Precision.HIGHEST) + b_ref[...]


def _modulation(c, w_mod, b_mod):
    nl, d, _ = w_mod.shape
    b = c.shape[0]
    return pl.pallas_call(
        _mod_body,
        grid=(nl, 6),
        in_specs=[pl.BlockSpec((b, d), lambda l, k: (0, 0)),
                  pl.BlockSpec((None, d, d), lambda l, k: (l, 0, k)),
                  pl.BlockSpec((None, 1, d), lambda l, k: (l, 0, k))],
        out_specs=pl.BlockSpec((None, None, b, d), lambda l, k: (l, k, 0, 0)),
        out_shape=jax.ShapeDtypeStruct((nl, 6, b, d), F32),
        compiler_params=_cparams(("parallel", "parallel")),
        name="modulation",
    )(c, w_mod, b_mod.reshape(nl, 1, 6 * d))


def _inproj_body(x_ref, g_ref, sc_ref, sh_ref, w_ref, om_ref, o4_ref, o16_ref, hf_ref, *, tm):
    h = _rms(x_ref[...], g_ref[...]) * (1.0 + sc_ref[...]) + sh_ref[...]
    nc = h.shape[-1] // LANES
    for c in range(nc):
        hf_ref[c] = h[:, c * LANES:(c + 1) * LANES]
    hb = h.astype(BF16)
    for c0 in range(0, MAIN_COLS, COL_CHUNK):
        om_ref[:, c0:c0 + COL_CHUNK] = jnp.dot(hb, w_ref[:, c0:c0 + COL_CHUNK],
                                               preferred_element_type=F32).astype(BF16)
    for o_ref, dil, c0 in ((o4_ref, 4, MAIN_COLS), (o16_ref, 16, MAIN_COLS + PATTERN_COLS)):
        n = tm // dil
        hp = jnp.concatenate(
            [jnp.concatenate([hf_ref[c, pl.ds(r, n, stride=dil), :] for r in range(dil)], axis=0)
             for c in range(nc)], axis=1).astype(BF16)
        res = jnp.dot(hp, w_ref[:, c0:c0 + PATTERN_COLS], preferred_element_type=F32).astype(BF16)
        for r in range(dil):
            o_ref[r] = res[r * n:(r + 1) * n]


def _inproj(x, g, sc, sh, w, tm=1024):
    b, s, d = x.shape
    d4, d16 = D_PATTERNS[1][1], D_PATTERNS[2][1]
    return pl.pallas_call(
        functools.partial(_inproj_body, tm=tm),
        grid=(b, s // tm),
        in_specs=[pl.BlockSpec((None, tm, d), lambda bi, i: (bi, i, 0)),
                  pl.BlockSpec((1, d), lambda bi, i: (0, 0)),
                  pl.BlockSpec((None, 1, d), lambda bi, i: (bi, 0, 0)),
                  pl.BlockSpec((None, 1, d), lambda bi, i: (bi, 0, 0)),
                  pl.BlockSpec(w.shape, lambda bi, i: (0, 0), pipeline_mode=pl.Buffered(1))],
        out_specs=[pl.BlockSpec((None, tm, MAIN_COLS), lambda bi, i: (bi, i, 0)),
                   pl.BlockSpec((None, d4, tm // d4, PATTERN_COLS), lambda bi, i: (bi, 0, i, 0)),
                   pl.BlockSpec((None, d16, tm // d16, PATTERN_COLS), lambda bi, i: (bi, 0, i, 0))],
        out_shape=[jax.ShapeDtypeStruct((b, s, MAIN_COLS), BF16),
                   jax.ShapeDtypeStruct((b, d4, s // d4, PATTERN_COLS), BF16),
                   jax.ShapeDtypeStruct((b, d16, s // d16, PATTERN_COLS), BF16)],
        scratch_shapes=[pltpu.VMEM((d // LANES, tm, LANES), F32)],
        compiler_params=_cparams(("parallel", "parallel")),
        name="inproj",
    )(x, g, sc, sh, w)


def _t5_bucket(rel):
    half = REL_BUCKETS // 2
    max_exact = half // 2
    n = jnp.abs(rel)
    n_f = jnp.maximum(n, 1).astype(jnp.float32)
    large = max_exact + (jnp.log(n_f / max_exact) / math.log(REL_MAX_DISTANCE / max_exact)
                         * (half - max_exact)).astype(jnp.int32)
    large = jnp.minimum(large, half - 1)
    return jnp.where(rel > 0, half, 0) + jnp.where(n < max_exact, n, large)


def _band_bias(table, radius, stride, sub_q):
    qi = jnp.arange(sub_q)[:, None]
    kj = jnp.arange(sub_q + 2 * radius)[None, :]
    off = kj - radius - qi
    onehot = (_t5_bucket(off * stride)[..., None] == jnp.arange(REL_BUCKETS)).astype(F32)
    bias = jnp.einsum('qkb,bh->hqk', onehot, table.astype(F32), precision=lax.Precision.HIGHEST)
    bias = jnp.where((jnp.abs(off) <= radius)[None], bias, NEG)
    return bias.reshape(-1, bias.shape[-1])


def _band_body(*refs, radius, tq, seq, kv_w, has_sink, has_lse):
    q_ref, kp_ref, kc_ref, kn_ref, vp_ref, vc_ref, vn_ref, bias_ref = refs[:8]
    rest = refs[8:]
    if has_sink:
        sink_ref, rest = rest[0], rest[1:]
    o_ref = rest[0]
    lse_ref = rest[1] if has_lse else None

    i = pl.program_id(2)
    nh = 4
    sub_q = bias_ref.shape[0] // nh
    nk = sub_q + 2 * radius
    qw = q_ref.shape[-1]
    per = kv_w // HEAD_DIM
    lane = lax.broadcasted_iota(jnp.int32, (1, kv_w), 1)
    hmask = [(lane >= HEAD_DIM * (h % per)) & (lane < HEAD_DIM * (h % per + 1)) for h in range(nh)]
    scale = HEAD_DIM ** -0.5

    def pick(rows):
        out = rows[(per - 1) * sub_q:per * sub_q]
        for h in range(per - 2, -1, -1):
            out = jnp.where(hmask[h], rows[h * sub_q:(h + 1) * sub_q], out)
        return out

    for rr in range(q_ref.shape[0]):
        kcat = jnp.concatenate([kp_ref[rr], kc_ref[rr], kn_ref[rr]], axis=0)
        vcat = jnp.concatenate([vp_ref[rr], vc_ref[rr], vn_ref[rr]], axis=0)
        for sb in range(tq // sub_q):
            r0 = sb * sub_q
            qs = q_ref[rr, r0:r0 + sub_q, :] * scale
            ks = kcat[r0:r0 + nk]
            vs = vcat[r0:r0 + nk]
            kpos = i * tq + (r0 - radius) + lax.broadcasted_iota(jnp.int32, (1, nk), 1)
            inb = (kpos >= 0) & (kpos < seq)
            parts = []
            for h in range(nh):
                src = qs[:, (h // per) * kv_w:(h // per + 1) * kv_w]
                parts.append(jnp.where(hmask[h], src, jnp.zeros_like(src)))
            qst = jnp.concatenate(parts, axis=0)
            sc = lax.dot_general(qst, ks, (((1,), (1,)), ((), ())), preferred_element_type=F32)
            sc = jnp.where(inb, sc + bias_ref[...], NEG)
            m = jnp.max(sc, axis=-1, keepdims=True)
            if has_sink:
                m = jnp.maximum(m, jnp.max(sink_ref[...], axis=-1, keepdims=True))
            p = jnp.exp(sc - m)
            den = jnp.sum(p, axis=-1, keepdims=True)
            if has_sink:
                den = den + jnp.max(jnp.exp(sink_ref[...] - m), axis=-1, keepdims=True)
            o = jnp.dot(p.astype(BF16), vs, preferred_element_type=F32) / den
            lse = jnp.broadcast_to(m + jnp.log(den), o.shape) if has_lse else None
            for g in range(nh // per):
                rows = slice(g * per * sub_q, (g + 1) * per * sub_q)
                cols = slice(rr * qw + g * kv_w, rr * qw + (g + 1) * kv_w)
                o_ref[r0:r0 + sub_q, cols] = pick(o[rows]).astype(BF16)
                if has_lse:
                    lse_ref[r0:r0 + sub_q, cols] = pick(lse[rows])


def _band_attention(qkv, bias, sink, *, dil, radius, tq, q_off, k_off, v_off, kv_w, has_lse, group=1):
    b = qkv.shape[0]
    seq = qkv.shape[-2]
    s = seq * dil
    tq = min(tq, seq)
    qw = 2 * LANES
    steps = tq // radius
    last = seq // radius - 1
    if dil == 1:
        qkv = qkv.reshape(b, 1, seq, qkv.shape[-1])

    def spec(rows, width, off, row_index):
        return pl.BlockSpec((None, group, rows, width), lambda bi, r, i: (bi, r, row_index(i), off // width))

    def cur(i):
        return i

    def prev(i):
        return jnp.maximum(i * steps - 1, 0)

    def nxt(i):
        return jnp.minimum((i + 1) * steps, last)

    in_specs = [spec(tq, qw, q_off, cur)]
    for off in (k_off, v_off):
        in_specs += [spec(radius, kv_w, off, prev), spec(tq, kv_w, off, cur), spec(radius, kv_w, off, nxt)]
    in_specs.append(pl.BlockSpec(bias.shape, lambda bi, r, i: (0, 0)))
    args = [qkv] * 7 + [bias]
    if sink is not None:
        in_specs.append(pl.BlockSpec(sink.shape, lambda bi, r, i: (0, 0)))
        args.append(sink)

    o_spec = pl.BlockSpec((None, tq, group * qw), lambda bi, r, i: (bi, i, r))
    out_shape = [jax.ShapeDtypeStruct((b, seq, dil * qw), BF16)]
    out_specs = [o_spec]
    if has_lse:
        out_shape.append(jax.ShapeDtypeStruct((b, seq, dil * qw), F32))
        out_specs.append(o_spec)

    outs = pl.pallas_call(
        functools.partial(_band_body, radius=radius, tq=tq, seq=seq, kv_w=kv_w,
                          has_sink=sink is not None, has_lse=has_lse),
        grid=(b, dil // group, seq // tq),
        in_specs=in_specs,
        out_specs=out_specs,
        out_shape=out_shape,
        compiler_params=_cparams(("parallel", "parallel", "parallel")),
        name=f"band_attention_d{dil}_r{radius}",
    )(*args)
    return [o.reshape(b, s, qw) for o in outs]


def _mla_prep_body(cq_ref, ckv_ref, ckr_ref, cos_ref, sin_ref, gq_ref, gkv_ref, wq_ref, wkv_ref, swap_ref,
                   q_ref, k_ref, v_ref):
    cs = cos_ref[...]
    sn = sin_ref[...]
    scale = (C_NOPE_DIM + C_ROPE_DIM) ** -0.5
    nq = _rms(cq_ref[...].astype(F32), gq_ref[...]).astype(BF16)
    qq = jnp.dot(nq, wq_ref[...], preferred_element_type=F32)
    nkv = _rms(ckv_ref[...].astype(F32), gkv_ref[...]).astype(BF16)
    kvv = jnp.dot(nkv, wkv_ref[...], preferred_element_type=F32)
    kr = ckr_ref[...]
    kr_sw = jnp.dot(kr, swap_ref[...], preferred_element_type=F32)
    k_rope = kr.astype(F32) * cs + kr_sw * sn
    hw = C_HEADS * LANES
    for h in range(C_HEADS):
        a = qq[:, h * LANES:(h + 1) * LANES]
        a_sw = qq[:, hw + h * LANES:hw + (h + 1) * LANES]
        q_ref[:, h * LANES:(h + 1) * LANES] = ((a * cs + a_sw * sn) * scale).astype(BF16)
        k_ref[:, h * LANES:(h + 1) * LANES] = (kvv[:, h * LANES:(h + 1) * LANES] + k_rope).astype(BF16)
    v_ref[...] = kvv[:, hw:].astype(BF16)


def _mla_prep(qkv, cos_t, sin_t, gq, gkv, wq, wkv, swap, tm=512):
    b, s, n = qkv.shape
    hw = C_HEADS * LANES
    vw = C_HEADS * C_V_DIM

    def const(shape):
        return pl.BlockSpec(shape, lambda bi, i: (0,) * len(shape))

    def rows(width, off=0):
        return pl.BlockSpec((None, tm, width), lambda bi, i: (bi, i, off // width))

    return pl.pallas_call(
        _mla_prep_body,
        grid=(b, s // tm),
        in_specs=[rows(C_Q_RANK, OFF_CQ), rows(LANES, OFF_CKV), rows(LANES, OFF_CKR),
                  rows(LANES), rows(LANES),
                  const(gq.shape), const(gkv.shape), const(wq.shape), const(wkv.shape), const(swap.shape)],
        out_specs=[rows(hw), rows(hw), rows(vw)],
        out_shape=[jax.ShapeDtypeStruct((b, s, hw), BF16), jax.ShapeDtypeStruct((b, s, hw), BF16),
                   jax.ShapeDtypeStruct((b, s, vw), BF16)],
        compiler_params=_cparams(("parallel", "parallel")),
        name="mla_prep",
    )(qkv, qkv, qkv, cos_t, sin_t, gq, gkv, wq, wkv, swap)


def _mla_attn_body(q_ref, k_ref, v_ref, o_ref, *, rows):
    lo = lax.broadcasted_iota(jnp.int32, (1, LANES), 1) < C_V_DIM
    for r0 in range(0, q_ref.shape[0], rows):
        for g in range(C_HEADS // 2):
            acc = None
            for half in range(2):
                h = 2 * g + half
                q = q_ref[r0:r0 + rows, h * LANES:(h + 1) * LANES]
                k = k_ref[:, h * LANES:(h + 1) * LANES]
                sc = lax.dot_general(q, k, (((1,), (1,)), ((), ())), preferred_element_type=F32)
                m = jnp.max(sc, axis=-1, keepdims=True)
                p = jnp.exp(sc - m)
                den = jnp.sum(p, axis=-1, keepdims=True)
                o = jnp.dot(p.astype(BF16), v_ref[:, g * LANES:(g + 1) * LANES],
                            preferred_element_type=F32) / den
                acc = o if acc is None else jnp.where(lo, acc, o)
            o_ref[r0:r0 + rows, g * LANES:(g + 1) * LANES] = acc.astype(BF16)


def _mla_attn(q, k, v, tq=512, rows=256):
    b, s, hw = q.shape
    vw = v.shape[-1]
    return pl.pallas_call(
        functools.partial(_mla_attn_body, rows=rows),
        grid=(b, s // tq),
        in_specs=[pl.BlockSpec((None, tq, hw), lambda bi, i: (bi, i, 0)),
                  pl.BlockSpec((None, s, hw), lambda bi, i: (bi, 0, 0)),
                  pl.BlockSpec((None, s, vw), lambda bi, i: (bi, 0, 0))],
        out_specs=pl.BlockSpec((None, tq, vw), lambda bi, i: (bi, i, 0)),
        out_shape=jax.ShapeDtypeStruct((b, s, vw), BF16),
        compiler_params=_cparams(("parallel", "arbitrary")),
        name="mla_attention",
    )(q, k, v)


def _outproj_body(x_ref, oa_ref, bb_ref, bc_ref, bh_ref, bcp_ref, bhp_ref, bcn_ref, bhn_ref, oc_ref,
                  o0_ref, o1_ref, o2_ref, l0_ref, l1_ref, l2_ref, w_ref, cw_ref, gn_ref, gate_ref,
                  out_ref, conv_ref, *, tm):
    i = pl.program_id(1)
    pad = 8
    first = (i > 0).astype(F32)
    final = (i < pl.num_programs(1) - 1).astype(F32)
    conv_ref[pad:pad + tm, :] = bc_ref[...].astype(F32) * bh_ref[...].astype(F32)
    conv_ref[pad - 1:pad, :] = (bcp_ref[BF16_ROWS - 1:, :].astype(F32)
                                * bhp_ref[BF16_ROWS - 1:, :].astype(F32)) * first
    conv_ref[pad + tm:pad + tm + 1, :] = (bcn_ref[:1, :].astype(F32) * bhn_ref[:1, :].astype(F32)) * final
    cw = cw_ref[...]
    conv = (conv_ref[pad - 1:pad - 1 + tm, :] * cw[0:1] + conv_ref[pad:pad + tm, :] * cw[1:2]
            + conv_ref[pad + 1:pad + 1 + tm, :] * cw[2:3])
    ob = (bb_ref[...].astype(F32) * conv).astype(BF16)
    l0, l1, l2 = l0_ref[...], l1_ref[...], l2_ref[...]
    mx = jnp.maximum(jnp.maximum(l0, l1), l2)
    e0, e1, e2 = jnp.exp(l0 - mx), jnp.exp(l1 - mx), jnp.exp(l2 - mx)
    od = ((o0_ref[...].astype(F32) * e0 + o1_ref[...].astype(F32) * e1 + o2_ref[...].astype(F32) * e2)
          / (e0 + e1 + e2)).astype(BF16)
    wd = 2 * LANES
    y = jnp.dot(oa_ref[...], w_ref[0:wd, :], preferred_element_type=F32)
    y = y + jnp.dot(ob, w_ref[wd:2 * wd, :], preferred_element_type=F32)
    y = y + jnp.dot(oc_ref[...], w_ref[2 * wd:3 * wd, :], preferred_element_type=F32)
    y = y + jnp.dot(od, w_ref[3 * wd:4 * wd, :], preferred_element_type=F32)
    out_ref[...] = x_ref[...] + gate_ref[...] * _rms(y, gn_ref[...])


def _outproj(x, oa, qkv, oc, od, lse, w, cw, gn, gate, tm=512):
    b, s, d = x.shape
    wd = 2 * LANES
    hb = tm // BF16_ROWS
    last = s // BF16_ROWS - 1

    def rows(width, off=0):
        return pl.BlockSpec((None, tm, width), lambda bi, i: (bi, i, off // width))

    def halo_prev(off):
        return pl.BlockSpec((None, BF16_ROWS, wd), lambda bi, i: (bi, jnp.maximum(i * hb - 1, 0), off // wd))

    def halo_next(off):
        return pl.BlockSpec((None, BF16_ROWS, wd), lambda bi, i: (bi, jnp.minimum((i + 1) * hb, last), off // wd))

    def const(shape):
        return pl.BlockSpec(shape, lambda bi, i: (0,) * len(shape))

    return pl.pallas_call(
        functools.partial(_outproj_body, tm=tm),
        grid=(b, s // tm),
        in_specs=[rows(d), rows(wd), rows(wd, OFF_BB), rows(wd, OFF_BC), rows(wd, OFF_BH),
                  halo_prev(OFF_BC), halo_prev(OFF_BH), halo_next(OFF_BC), halo_next(OFF_BH),
                  rows(wd), rows(wd), rows(wd), rows(wd), rows(wd), rows(wd), rows(wd),
                  const(w.shape), const(cw.shape), const(gn.shape),
                  pl.BlockSpec((None, 1, d), lambda bi, i: (bi, 0, 0))],
        out_specs=rows(d),
        out_shape=jax.ShapeDtypeStruct((b, s, d), F32),
        scratch_shapes=[pltpu.VMEM((tm + 16, wd), F32)],
        compiler_params=_cparams(("parallel", "arbitrary")),
        name="outproj",
    )(x, oa, qkv, qkv, qkv, qkv, qkv, qkv, qkv, oc, od[0], od[1], od[2], lse[0], lse[1], lse[2],
      w, cw, gn, gate)


def _ffn_body(x_ref, xp_ref, xn_ref, g_ref, sc_ref, sh_ref, gate_ref, gy_ref, wup_ref, cw_ref, wd_ref, out_ref,
              h_ref, u_ref, a_ref, *, tm, tn):
    i = pl.program_id(1)
    pad = BF16_ROWS
    ff = wd_ref.shape[0]

    def mod(v):
        return _rms(v, g_ref[...]) * (1.0 + sc_ref[...]) + sh_ref[...]

    first = (i > 0).astype(F32)
    final = (i < pl.num_programs(1) - 1).astype(F32)
    h_ref[0:pad, :] = (mod(xp_ref[...]) * first).astype(BF16)
    h_ref[pad:pad + tm, :] = mod(x_ref[...]).astype(BF16)
    h_ref[pad + tm:, :] = (mod(xn_ref[...]) * final).astype(BF16)
    h = h_ref[...]

    def conv(slot, half, c0):
        lanes = slice(half * tn, (half + 1) * tn)
        cw = cw_ref[:, c0:c0 + tn]
        return (u_ref[slot, pad - 1:pad - 1 + tm, lanes] * cw[0:1] + u_ref[slot, pad:pad + tm, lanes] * cw[1:2]
                + u_ref[slot, pad + 1:pad + 1 + tm, lanes] * cw[2:3])

    for j in range(ff // tn):
        slot = j % 2
        c0 = j * tn
        u_ref[slot, :, 0:tn] = jnp.dot(h, wup_ref[:, c0:c0 + tn], preferred_element_type=F32)
        u_ref[slot, :, tn:2 * tn] = jnp.dot(h, wup_ref[:, ff + c0:ff + c0 + tn], preferred_element_type=F32)
        a = jax.nn.gelu(conv(slot, 0, c0), approximate=True) * conv(slot, 1, ff + c0)
        a_ref[:, c0:c0 + tn] = a.astype(BF16)

    y = jnp.dot(a_ref[...], wd_ref[...], preferred_element_type=F32)
    out_ref[...] = x_ref[...] + gate_ref[...] * _rms(y, gy_ref[...])


def _ffn(x, g, sc, sh, gate, gy, w_up, cw, w_down, tm=512, tn=256):
    b, s, d = x.shape
    ff = w_down.shape[0]
    hb = tm // BF16_ROWS
    last = s // BF16_ROWS - 1

    def const(shape, **kw):
        return pl.BlockSpec(shape, lambda bi, i: (0,) * len(shape), **kw)

    def per_batch():
        return pl.BlockSpec((None, 1, d), lambda bi, i: (bi, 0, 0))

    resident = dict(pipeline_mode=pl.Buffered(1))
    return pl.pallas_call(
        functools.partial(_ffn_body, tm=tm, tn=tn),
        grid=(b, s // tm),
        in_specs=[pl.BlockSpec((None, tm, d), lambda bi, i: (bi, i, 0)),
                  pl.BlockSpec((None, BF16_ROWS, d), lambda bi, i: (bi, jnp.maximum(i * hb - 1, 0), 0)),
                  pl.BlockSpec((None, BF16_ROWS, d), lambda bi, i: (bi, jnp.minimum((i + 1) * hb, last), 0)),
                  const(g.shape), per_batch(), per_batch(), per_batch(), const(gy.shape),
                  const(w_up.shape, **resident), const(cw.shape), const(w_down.shape, **resident)],
        out_specs=pl.BlockSpec((None, tm, d), lambda bi, i: (bi, i, 0)),
        out_shape=jax.ShapeDtypeStruct((b, s, d), F32),
        scratch_shapes=[pltpu.VMEM((tm + 2 * BF16_ROWS, d), BF16),
                        pltpu.VMEM((2, tm + 2 * BF16_ROWS, 2 * tn), F32),
                        pltpu.VMEM((tm, ff), BF16)],
        compiler_params=_cparams(("parallel", "arbitrary")),
        name="ffn",
    )(x, x, x, g, sc, sh, gate, gy, w_up, cw, w_down)


def _pack_w_in(w_in):
    nl, d, _ = w_in.shape
    hd = HEAD_DIM
    aq = w_in[..., 0:256]
    aq = jnp.concatenate([aq[..., 0:hd], aq[..., 2 * hd:3 * hd], aq[..., hd:2 * hd], aq[..., 3 * hd:]], axis=-1)
    ckr = w_in[..., 1664:1696]
    ckr = jnp.concatenate([jnp.zeros((nl, d, C_NOPE_DIM), w_in.dtype), ckr,
                           jnp.zeros((nl, d, LANES - C_NOPE_DIM - C_ROPE_DIM), w_in.dtype)], axis=-1)
    return jnp.concatenate([aq, w_in[..., 256:1664], ckr, w_in[..., 1696:]], axis=-1).astype(BF16)


def _pack_w_uq(w):
    nl, r, _ = w.shape
    half = C_ROPE_DIM // 2
    w = w.reshape(nl, r, C_HEADS, C_NOPE_DIM + C_ROPE_DIM)
    nope, r1, r2 = w[..., :C_NOPE_DIM], w[..., C_NOPE_DIM:C_NOPE_DIM + half], w[..., C_NOPE_DIM + half:]
    z_tail = jnp.zeros((nl, r, C_HEADS, LANES - C_NOPE_DIM - C_ROPE_DIM), w.dtype)
    plain = jnp.concatenate([nope, r1, r2, z_tail], axis=-1).reshape(nl, r, C_HEADS * LANES)
    swapped = jnp.concatenate([jnp.zeros_like(nope), r2, r1, z_tail], axis=-1).reshape(nl, r, C_HEADS * LANES)
    return jnp.concatenate([plain, swapped], axis=-1).astype(BF16)


def _pack_w_ukv(w):
    nl, r, _ = w.shape
    w = w.reshape(nl, r, C_HEADS, C_NOPE_DIM + C_V_DIM)
    kn = jnp.concatenate([w[..., :C_NOPE_DIM], jnp.zeros((nl, r, C_HEADS, LANES - C_NOPE_DIM), w.dtype)], axis=-1)
    return jnp.concatenate([kn.reshape(nl, r, C_HEADS * LANES),
                            w[..., C_NOPE_DIM:].reshape(nl, r, C_HEADS * C_V_DIM)], axis=-1).astype(BF16)


def _pack_w_out(w_out):
    hd = HEAD_DIM
    return jnp.concatenate([w_out[:, 0:hd], w_out[:, 2 * hd:3 * hd], w_out[:, hd:2 * hd], w_out[:, 3 * hd:]],
                           axis=1).astype(BF16)


def _rope_tables(positions):
    half = C_ROPE_DIM // 2
    inv_freq = ROPE_THETA ** (-jnp.arange(half, dtype=jnp.float32) / half)
    ang = positions.astype(jnp.float32)[..., None] * inv_freq
    cos, sin = jnp.cos(ang), jnp.sin(ang)
    lead = positions.shape + (C_NOPE_DIM,)
    tail = positions.shape + (LANES - C_NOPE_DIM - C_ROPE_DIM,)
    cos_t = jnp.concatenate([jnp.ones(lead, F32), cos, cos, jnp.zeros(tail, F32)], axis=-1)
    sin_t = jnp.concatenate([jnp.zeros(lead, F32), -sin, sin, jnp.zeros(tail, F32)], axis=-1)
    return cos_t, sin_t


def _swap_matrix():
    half = C_ROPE_DIM // 2
    idx = jnp.arange(LANES)
    src = jnp.where((idx >= C_NOPE_DIM) & (idx < C_NOPE_DIM + half), idx + half,
                    jnp.where((idx >= C_NOPE_DIM + half) & (idx < C_NOPE_DIM + C_ROPE_DIM), idx - half, idx))
    return (idx[:, None] == src[None, :]).astype(BF16)


def kernel(x, c, positions, rel_bias, w_mod, b_mod, norm_g, w_in, a_sink, b_conv, c_norm_q, c_norm_kv,
           c_w_uq, c_w_ukv, w_out, w_up, ffn_conv, w_down):
    b, s, d = x.shape
    depth = w_in.shape[0]

    a_order = jnp.array(A_HEAD_ORDER)
    bias_a = _band_bias(rel_bias[:, :A_Q_HEADS][:, a_order], A_RADIUS, 1, A_SUB_Q)
    bias_d = [_band_bias(rel_bias[:, A_Q_HEADS + i * D_HEADS:A_Q_HEADS + (i + 1) * D_HEADS], (w // 2) // dil, dil,
                         D_SUB_Q)
              for i, (w, dil) in enumerate(D_PATTERNS)]
    cos_t, sin_t = _rope_tables(positions)
    swap = _swap_matrix()

    w_in_p = _pack_w_in(w_in)
    w_uq_p = _pack_w_uq(c_w_uq)
    w_ukv_p = _pack_w_ukv(c_w_ukv)
    w_out_p = _pack_w_out(w_out)
    w_up_b = w_up.astype(BF16)
    w_down_b = w_down.astype(BF16)

    mod = _modulation(c, w_mod, b_mod)

    for l in range(depth):
        sh1, sc1, g1, sh2, sc2, g2 = (mod[l, k].reshape(b, 1, d) for k in range(6))
        gains = norm_g[l]

        qkv, qkv_d4, qkv_d16 = _inproj(x, gains[0:1], sc1, sh1, w_in_p[l])

        sink = jnp.broadcast_to(jnp.repeat(a_sink[l][a_order], A_SUB_Q)[:, None], (A_Q_HEADS * A_SUB_Q, LANES))
        oa = _band_attention(qkv, bias_a, sink, dil=1, radius=A_RADIUS, tq=512, q_off=OFF_AQ, k_off=OFF_AK,
                             v_off=OFF_AV, kv_w=LANES, has_lse=False)[0]

        qc, kc, vc = _mla_prep(qkv, cos_t, sin_t, c_norm_q[l][None], c_norm_kv[l][None], w_uq_p[l], w_ukv_p[l],
                               swap)
        oc = _mla_attn(qc, kc, vc)

        od, lse = [], []
        for i, ((w, dil), src) in enumerate(zip(D_PATTERNS, (qkv, qkv_d4, qkv_d16))):
            off = OFF_D if dil == 1 else 0
            o_i, l_i = _band_attention(src, bias_d[i], None, dil=dil, radius=(w // 2) // dil, tq=512, q_off=off,
                                       k_off=off + 256, v_off=off + 512, kv_w=2 * LANES, has_lse=True,
                                       group=2 if dil == 16 else 1)
            od.append(o_i)
            lse.append(l_i)

        x = _outproj(x, oa, qkv, oc, od, lse, w_out_p[l], b_conv[l], gains[1:2], g1)
        x = _ffn(x, gains[2:3], sc2, sh2, g2, gains[3:4], w_up_b[l], ffn_conv[l], w_down_b[l])
    return x
```

```python
import functools
import math

import jax
import jax.numpy as jnp
from jax import lax
from jax.experimental import pallas as pl
from jax.experimental.pallas import tpu as pltpu

D_MODEL = 1024
HEAD_DIM = 64
A_Q_HEADS = 4
A_KV_HEADS = 2
A_RADIUS = 128
B_WIDTH = 256
C_HEADS = 4
C_Q_RANK = 256
C_KV_RANK = 128
C_NOPE_DIM = 64
C_ROPE_DIM = 32
C_V_DIM = 64
ROPE_THETA = 10000.0
D_HEADS = 4
D_PATTERNS = ((128, 1), (512, 4), (2048, 16))
REL_BUCKETS = 32
REL_MAX_DISTANCE = 1024
D_FF = 2816
EPS = 1e-6
NEG = -1e30

LANES = 128
BF16_ROWS = 16
A_SUB_Q = 128
D_SUB_Q = 128
VMEM_LIMIT = 48 * 1024 * 1024

IN_COLS = 4096
OFF_AQ, OFF_AK, OFF_AV = 0, 256, 384
OFF_BB, OFF_BC, OFF_BH = 512, 768, 1024
OFF_CQ, OFF_CKV, OFF_CKR = 1280, 1536, 1664
OFF_D = 1792
A_HEAD_ORDER = (0, 2, 1, 3)
PATTERN_COLS = 3 * D_HEADS * HEAD_DIM
MAIN_COLS = OFF_D + PATTERN_COLS
COL_CHUNK = 512

F32 = jnp.float32
BF16 = jnp.bfloat16


def _cparams(sem):
    return pltpu.CompilerParams(dimension_semantics=sem, vmem_limit_bytes=VMEM_LIMIT)


def _rms(v, g):
    return v * lax.rsqrt(jnp.mean(v * v, axis=-1, keepdims=True) + EPS) * g


def _mod_body(c_ref, w_ref, b_ref, o_ref):
    c = c_ref[...]
    act = c / (1.0 + jnp.exp(-c))
    o_ref[...] = jnp.dot(act, w_ref[...], preferred_element_type=F32,
                         precision=lax.Precision.HIGHEST) + b_ref[...]


def _modulation(c, w_mod, b_mod):
    nl, d, _ = w_mod.shape
    b = c.shape[0]
    return pl.pallas_call(
        _mod_body,
        grid=(nl, 6),
        in_specs=[pl.BlockSpec((b, d), lambda l, k: (0, 0)),
                  pl.BlockSpec((None, d, d), lambda l, k: (l, 0, k)),
                  pl.BlockSpec((None, 1, d), lambda l, k: (l, 0, k))],
        out_specs=pl.BlockSpec((None, None, b, d), lambda l, k: (l, k, 0, 0)),
        out_shape=jax.ShapeDtypeStruct((nl, 6, b, d), F32),
        compiler_params=_cparams(("parallel", "parallel")),
        name="modulation",
    )(c, w_mod, b_mod.reshape(nl, 1, 6 * d))


def _inproj_body(x_ref, g_ref, sc_ref, sh_ref, w_ref, om_ref, o4_ref, o16_ref, hf_ref, *, tm):
    h = _rms(x_ref[...], g_ref[...]) * (1.0 + sc_ref[...]) + sh_ref[...]
    nc = h.shape[-1] // LANES
    for c in range(nc):
        hf_ref[c] = h[:, c * LANES:(c + 1) * LANES]
    hb = h.astype(BF16)
    for c0 in range(0, MAIN_COLS, COL_CHUNK):
        om_ref[:, c0:c0 + COL_CHUNK] = jnp.dot(hb, w_ref[:, c0:c0 + COL_CHUNK],
                                               preferred_element_type=F32).astype(BF16)
    for o_ref, dil, c0 in ((o4_ref, 4, MAIN_COLS), (o16_ref, 16, MAIN_COLS + PATTERN_COLS)):
        n = tm // dil
        hp = jnp.concatenate(
            [jnp.concatenate([hf_ref[c, pl.ds(r, n, stride=dil), :] for r in range(dil)], axis=0)
             for c in range(nc)], axis=1).astype(BF16)
        res = jnp.dot(hp, w_ref[:, c0:c0 + PATTERN_COLS], preferred_element_type=F32).astype(BF16)
        for r in range(dil):
            o_ref[r] = res[r * n:(r + 1) * n]


def _inproj(x, g, sc, sh, w, tm=1024):
    b, s, d = x.shape
    d4, d16 = D_PATTERNS[1][1], D_PATTERNS[2][1]
    return pl.pallas_call(
        functools.partial(_inproj_body, tm=tm),
        grid=(b, s // tm),
        in_specs=[pl.BlockSpec((None, tm, d), lambda bi, i: (bi, i, 0)),
                  pl.BlockSpec((1, d), lambda bi, i: (0, 0)),
                  pl.BlockSpec((None, 1, d), lambda bi, i: (bi, 0, 0)),
                  pl.BlockSpec((None, 1, d), lambda bi, i: (bi, 0, 0)),
                  pl.BlockSpec(w.shape, lambda bi, i: (0, 0), pipeline_mode=pl.Buffered(1))],
        out_specs=[pl.BlockSpec((None, tm, MAIN_COLS), lambda bi, i: (bi, i, 0)),
                   pl.BlockSpec((None, d4, tm // d4, PATTERN_COLS), lambda bi, i: (bi, 0, i, 0)),
                   pl.BlockSpec((None, d16, tm // d16, PATTERN_COLS), lambda bi, i: (bi, 0, i, 0))],
        out_shape=[jax.ShapeDtypeStruct((b, s, MAIN_COLS), BF16),
                   jax.ShapeDtypeStruct((b, d4, s // d4, PATTERN_COLS), BF16),
                   jax.ShapeDtypeStruct((b, d16, s // d16, PATTERN_COLS), BF16)],
        scratch_shapes=[pltpu.VMEM((d // LANES, tm, LANES), F32)],
        compiler_params=_cparams(("parallel", "parallel")),
        name="inproj",
    )(x, g, sc, sh, w)


def _t5_bucket(rel):
    half = REL_BUCKETS // 2
    max_exact = half // 2
    n = jnp.abs(rel)
    n_f = jnp.maximum(n, 1).astype(jnp.float32)
    large = max_exact + (jnp.log(n_f / max_exact) / math.log(REL_MAX_DISTANCE / max_exact)
                         * (half - max_exact)).astype(jnp.int32)
    large = jnp.minimum(large, half - 1)
    return jnp.where(rel > 0, half, 0) + jnp.where(n < max_exact, n, large)


def _band_bias(table, radius, stride, sub_q):
    qi = jnp.arange(sub_q)[:, None]
    kj = jnp.arange(sub_q + 2 * radius)[None, :]
    off = kj - radius - qi
    onehot = (_t5_bucket(off * stride)[..., None] == jnp.arange(REL_BUCKETS)).astype(F32)
    bias = jnp.einsum('qkb,bh->hqk', onehot, table.astype(F32), precision=lax.Precision.HIGHEST)
    bias = jnp.where((jnp.abs(off) <= radius)[None], bias, NEG)
    return bias.reshape(-1, bias.shape[-1])


def _band_body(*refs, radius, tq, seq, kv_w, has_sink, has_lse):
    q_ref, kp_ref, kc_ref, kn_ref, vp_ref, vc_ref, vn_ref, bias_ref = refs[:8]
    rest = refs[8:]
    if has_sink:
        sink_ref, rest = rest[0], rest[1:]
    o_ref = rest[0]
    lse_ref = rest[1] if has_lse else None

    i = pl.program_id(2)
    nh = 4
    sub_q = bias_ref.shape[0] // nh
    nk = sub_q + 2 * radius
    per = kv_w // HEAD_DIM
    lane = lax.broadcasted_iota(jnp.int32, (1, kv_w), 1)
    hmask = [(lane >= HEAD_DIM * (h % per)) & (lane < HEAD_DIM * (h % per + 1)) for h in range(nh)]
    scale = HEAD_DIM ** -0.5

    def pick(rows):
        out = rows[(per - 1) * sub_q:per * sub_q]
        for h in range(per - 2, -1, -1):
            out = jnp.where(hmask[h], rows[h * sub_q:(h + 1) * sub_q], out)
        return out

    for rr in range(q_ref.shape[0]):
        kcat = jnp.concatenate([kp_ref[rr], kc_ref[rr], kn_ref[rr]], axis=0)
        vcat = jnp.concatenate([vp_ref[rr], vc_ref[rr], vn_ref[rr]], axis=0)
        for sb in range(tq // sub_q):
            r0 = sb * sub_q
            qs = q_ref[rr, r0:r0 + sub_q, :] * scale
            ks = kcat[r0:r0 + nk]
            vs = vcat[r0:r0 + nk]
            kpos = i * tq + (r0 - radius) + lax.broadcasted_iota(jnp.int32, (1, nk), 1)
            inb = (kpos >= 0) & (kpos < seq)
            parts = []
            for h in range(nh):
                src = qs[:, (h // per) * kv_w:(h // per + 1) * kv_w]
                parts.append(jnp.where(hmask[h], src, jnp.zeros_like(src)))
            qst = jnp.concatenate(parts, axis=0)
            sc = lax.dot_general(qst, ks, (((1,), (1,)), ((), ())), preferred_element_type=F32)
            sc = jnp.where(inb, sc + bias_ref[...], NEG)
            m = jnp.max(sc, axis=-1, keepdims=True)
            if has_sink:
                m = jnp.maximum(m, jnp.max(sink_ref[...], axis=-1, keepdims=True))
            p = jnp.exp(sc - m)
            den = jnp.sum(p, axis=-1, keepdims=True)
            if has_sink:
                den = den + jnp.max(jnp.exp(sink_ref[...] - m), axis=-1, keepdims=True)
            o = jnp.dot(p.astype(BF16), vs, preferred_element_type=F32) / den
            lse = jnp.broadcast_to(m + jnp.log(den), o.shape) if has_lse else None
            for g in range(nh // per):
                rows = slice(g * per * sub_q, (g + 1) * per * sub_q)
                cols = slice(g * kv_w, (g + 1) * kv_w)
                o_ref[rr, r0:r0 + sub_q, cols] = pick(o[rows]).astype(BF16)
                if has_lse:
                    lse_ref[rr, r0:r0 + sub_q, cols] = pick(lse[rows])


def _band_attention(qkv, bias, sink, *, dil, radius, tq, q_off, k_off, v_off, kv_w, has_lse, group=1):
    b = qkv.shape[0]
    seq = qkv.shape[-2]
    s = seq * dil
    tq = min(tq, seq)
    qw = 2 * LANES
    steps = tq // radius
    last = seq // radius - 1
    if dil == 1:
        qkv = qkv.reshape(b, 1, seq, qkv.shape[-1])

    def spec(rows, width, off, row_index):
        return pl.BlockSpec((None, group, rows, width), lambda bi, r, i: (bi, r, row_index(i), off // width))

    def cur(i):
        return i

    def prev(i):
        return jnp.maximum(i * steps - 1, 0)

    def nxt(i):
        return jnp.minimum((i + 1) * steps, last)

    in_specs = [spec(tq, qw, q_off, cur)]
    for off in (k_off, v_off):
        in_specs += [spec(radius, kv_w, off, prev), spec(tq, kv_w, off, cur), spec(radius, kv_w, off, nxt)]
    in_specs.append(pl.BlockSpec(bias.shape, lambda bi, r, i: (0, 0)))
    args = [qkv] * 7 + [bias]
    if sink is not None:
        in_specs.append(pl.BlockSpec(sink.shape, lambda bi, r, i: (0, 0)))
        args.append(sink)

    o_spec = pl.BlockSpec((None, group, tq, qw), lambda bi, r, i: (bi, r, i, 0))
    out_shape = [jax.ShapeDtypeStruct((b, dil, seq, qw), BF16)]
    out_specs = [o_spec]
    if has_lse:
        out_shape.append(jax.ShapeDtypeStruct((b, dil, seq, qw), F32))
        out_specs.append(o_spec)

    outs = pl.pallas_call(
        functools.partial(_band_body, radius=radius, tq=tq, seq=seq, kv_w=kv_w,
                          has_sink=sink is not None, has_lse=has_lse),
        grid=(b, dil // group, seq // tq),
        in_specs=in_specs,
        out_specs=out_specs,
        out_shape=out_shape,
        compiler_params=_cparams(("parallel", "parallel", "parallel")),
        name=f"band_attention_d{dil}_r{radius}",
    )(*args)
    return outs


def _mla_prep_body(cq_ref, ckv_ref, ckr_ref, cos_ref, sin_ref, gq_ref, gkv_ref, wq_ref, wkv_ref, swap_ref,
                   q_ref, k_ref, v_ref):
    cs = cos_ref[...]
    sn = sin_ref[...]
    scale = (C_NOPE_DIM + C_ROPE_DIM) ** -0.5
    nq = _rms(cq_ref[...].astype(F32), gq_ref[...]).astype(BF16)
    qq = jnp.dot(nq, wq_ref[...], preferred_element_type=F32)
    nkv = _rms(ckv_ref[...].astype(F32), gkv_ref[...]).astype(BF16)
    kvv = jnp.dot(nkv, wkv_ref[...], preferred_element_type=F32)
    kr = ckr_ref[...]
    kr_sw = jnp.dot(kr, swap_ref[...], preferred_element_type=F32)
    k_rope = kr.astype(F32) * cs + kr_sw * sn
    hw = C_HEADS * LANES
    for h in range(C_HEADS):
        a = qq[:, h * LANES:(h + 1) * LANES]
        a_sw = qq[:, hw + h * LANES:hw + (h + 1) * LANES]
        q_ref[:, h * LANES:(h + 1) * LANES] = ((a * cs + a_sw * sn) * scale).astype(BF16)
        k_ref[:, h * LANES:(h + 1) * LANES] = (kvv[:, h * LANES:(h + 1) * LANES] + k_rope).astype(BF16)
    v_ref[...] = kvv[:, hw:].astype(BF16)


def _mla_prep(qkv, cos_t, sin_t, gq, gkv, wq, wkv, swap, tm=512):
    b, s, n = qkv.shape
    hw = C_HEADS * LANES
    vw = C_HEADS * C_V_DIM

    def const(shape):
        return pl.BlockSpec(shape, lambda bi, i: (0,) * len(shape))

    def rows(width, off=0):
        return pl.BlockSpec((None, tm, width), lambda bi, i: (bi, i, off // width))

    return pl.pallas_call(
        _mla_prep_body,
        grid=(b, s // tm),
        in_specs=[rows(C_Q_RANK, OFF_CQ), rows(LANES, OFF_CKV), rows(LANES, OFF_CKR),
                  rows(LANES), rows(LANES),
                  const(gq.shape), const(gkv.shape), const(wq.shape), const(wkv.shape), const(swap.shape)],
        out_specs=[rows(hw), rows(hw), rows(vw)],
        out_shape=[jax.ShapeDtypeStruct((b, s, hw), BF16), jax.ShapeDtypeStruct((b, s, hw), BF16),
                   jax.ShapeDtypeStruct((b, s, vw), BF16)],
        compiler_params=_cparams(("parallel", "parallel")),
        name="mla_prep",
    )(qkv, qkv, qkv, cos_t, sin_t, gq, gkv, wq, wkv, swap)


def _mla_attn_body(q_ref, k_ref, v_ref, o_ref, *, rows):
    lo = lax.broadcasted_iota(jnp.int32, (1, LANES), 1) < C_V_DIM
    for r0 in range(0, q_ref.shape[0], rows):
        for g in range(C_HEADS // 2):
            acc = None
            for half in range(2):
                h = 2 * g + half
                q = q_ref[r0:r0 + rows, h * LANES:(h + 1) * LANES]
                k = k_ref[:, h * LANES:(h + 1) * LANES]
                sc = lax.dot_general(q, k, (((1,), (1,)), ((), ())), preferred_element_type=F32)
                m = jnp.max(sc, axis=-1, keepdims=True)
                p = jnp.exp(sc - m)
                den = jnp.sum(p, axis=-1, keepdims=True)
                o = jnp.dot(p.astype(BF16), v_ref[:, g * LANES:(g + 1) * LANES],
                            preferred_element_type=F32) / den
                acc = o if acc is None else jnp.where(lo, acc, o)
            o_ref[r0:r0 + rows, g * LANES:(g + 1) * LANES] = acc.astype(BF16)


def _mla_attn(q, k, v, tq=512, rows=256):
    b, s, hw = q.shape
    vw = v.shape[-1]
    return pl.pallas_call(
        functools.partial(_mla_attn_body, rows=rows),
        grid=(b, s // tq),
        in_specs=[pl.BlockSpec((None, tq, hw), lambda bi, i: (bi, i, 0)),
                  pl.BlockSpec((None, s, hw), lambda bi, i: (bi, 0, 0)),
                  pl.BlockSpec((None, s, vw), lambda bi, i: (bi, 0, 0))],
        out_specs=pl.BlockSpec((None, tq, vw), lambda bi, i: (bi, i, 0)),
        out_shape=jax.ShapeDtypeStruct((b, s, vw), BF16),
        compiler_params=_cparams(("parallel", "arbitrary")),
        name="mla_attention",
    )(q, k, v)


def _outproj_body(x_ref, oa_ref, bb_ref, bc_ref, bh_ref, bcp_ref, bhp_ref, bcn_ref, bhn_ref, oc_ref,
                  o0_ref, o1_ref, o2_ref, l0_ref, l1_ref, l2_ref, w_ref, cw_ref, gn_ref, gate_ref,
                  out_ref, conv_ref, *perm_refs, tm):
    i = pl.program_id(1)

    def natural(ref, s_ref):
        dil, n, width = ref.shape
        for r in range(dil):
            blk = ref[r].astype(F32)
            for c in range(width // LANES):
                s_ref[c, pl.ds(r, n, stride=dil), :] = blk[:, c * LANES:(c + 1) * LANES]
        return jnp.concatenate([s_ref[c] for c in range(width // LANES)], axis=1)

    pad = 8
    first = (i > 0).astype(F32)
    final = (i < pl.num_programs(1) - 1).astype(F32)
    conv_ref[pad:pad + tm, :] = bc_ref[...].astype(F32) * bh_ref[...].astype(F32)
    conv_ref[pad - 1:pad, :] = (bcp_ref[BF16_ROWS - 1:, :].astype(F32)
                                * bhp_ref[BF16_ROWS - 1:, :].astype(F32)) * first
    conv_ref[pad + tm:pad + tm + 1, :] = (bcn_ref[:1, :].astype(F32) * bhn_ref[:1, :].astype(F32)) * final
    cw = cw_ref[...]
    conv = (conv_ref[pad - 1:pad - 1 + tm, :] * cw[0:1] + conv_ref[pad:pad + tm, :] * cw[1:2]
            + conv_ref[pad + 1:pad + 1 + tm, :] * cw[2:3])
    ob = (bb_ref[...].astype(F32) * conv).astype(BF16)
    l0 = l0_ref[0]
    l1, l2 = natural(l1_ref, perm_refs[0]), natural(l2_ref, perm_refs[1])
    o1, o2 = natural(o1_ref, perm_refs[2]), natural(o2_ref, perm_refs[3])
    mx = jnp.maximum(jnp.maximum(l0, l1), l2)
    e0, e1, e2 = jnp.exp(l0 - mx), jnp.exp(l1 - mx), jnp.exp(l2 - mx)
    od = ((o0_ref[0].astype(F32) * e0 + o1 * e1 + o2 * e2) / (e0 + e1 + e2)).astype(BF16)
    wd = 2 * LANES
    y = jnp.dot(oa_ref[...], w_ref[0:wd, :], preferred_element_type=F32)
    y = y + jnp.dot(ob, w_ref[wd:2 * wd, :], preferred_element_type=F32)
    y = y + jnp.dot(oc_ref[...], w_ref[2 * wd:3 * wd, :], preferred_element_type=F32)
    y = y + jnp.dot(od, w_ref[3 * wd:4 * wd, :], preferred_element_type=F32)
    out_ref[...] = x_ref[...] + gate_ref[...] * _rms(y, gn_ref[...])


def _outproj(x, oa, qkv, oc, od, lse, w, cw, gn, gate, tm=512):
    b, s, d = x.shape
    wd = 2 * LANES
    hb = tm // BF16_ROWS
    last = s // BF16_ROWS - 1

    def rows(width, off=0):
        return pl.BlockSpec((None, tm, width), lambda bi, i: (bi, i, off // width))

    def halo_prev(off):
        return pl.BlockSpec((None, BF16_ROWS, wd), lambda bi, i: (bi, jnp.maximum(i * hb - 1, 0), off // wd))

    def halo_next(off):
        return pl.BlockSpec((None, BF16_ROWS, wd), lambda bi, i: (bi, jnp.minimum((i + 1) * hb, last), off // wd))

    def const(shape):
        return pl.BlockSpec(shape, lambda bi, i: (0,) * len(shape))

    def grouped(arr):
        dil = arr.shape[1]
        return pl.BlockSpec((None, dil, tm // dil, wd), lambda bi, i: (bi, 0, i, 0))

    return pl.pallas_call(
        functools.partial(_outproj_body, tm=tm),
        grid=(b, s // tm),
        in_specs=[rows(d), rows(wd), rows(wd, OFF_BB), rows(wd, OFF_BC), rows(wd, OFF_BH),
                  halo_prev(OFF_BC), halo_prev(OFF_BH), halo_next(OFF_BC), halo_next(OFF_BH),
                  rows(wd)] + [grouped(a) for a in od] + [grouped(a) for a in lse] + [
                  const(w.shape), const(cw.shape), const(gn.shape),
                  pl.BlockSpec((None, 1, d), lambda bi, i: (bi, 0, 0))],
        out_specs=rows(d),
        out_shape=jax.ShapeDtypeStruct((b, s, d), F32),
        scratch_shapes=[pltpu.VMEM((tm + 16, wd), F32)] + [pltpu.VMEM((wd // LANES, tm, LANES), F32)] * 4,
        compiler_params=_cparams(("parallel", "arbitrary")),
        name="outproj",
    )(x, oa, qkv, qkv, qkv, qkv, qkv, qkv, qkv, oc, od[0], od[1], od[2], lse[0], lse[1], lse[2],
      w, cw, gn, gate)


def _ffn_body(x_ref, xp_ref, xn_ref, g_ref, sc_ref, sh_ref, gate_ref, gy_ref, wup_ref, cw_ref, wd_ref, out_ref,
              h_ref, u_ref, a_ref, *, tm, tn):
    i = pl.program_id(1)
    pad = BF16_ROWS
    ff = wd_ref.shape[0]

    def mod(v):
        return _rms(v, g_ref[...]) * (1.0 + sc_ref[...]) + sh_ref[...]

    first = (i > 0).astype(F32)
    final = (i < pl.num_programs(1) - 1).astype(F32)
    h_ref[0:pad, :] = (mod(xp_ref[...]) * first).astype(BF16)
    h_ref[pad:pad + tm, :] = mod(x_ref[...]).astype(BF16)
    h_ref[pad + tm:, :] = (mod(xn_ref[...]) * final).astype(BF16)
    h = h_ref[...]

    def conv(slot, half, c0):
        lanes = slice(half * tn, (half + 1) * tn)
        cw = cw_ref[:, c0:c0 + tn]
        return (u_ref[slot, pad - 1:pad - 1 + tm, lanes] * cw[0:1] + u_ref[slot, pad:pad + tm, lanes] * cw[1:2]
                + u_ref[slot, pad + 1:pad + 1 + tm, lanes] * cw[2:3])

    def up(j):
        c0 = j * tn
        u_ref[j % 2, :, 0:tn] = jnp.dot(h, wup_ref[:, c0:c0 + tn], preferred_element_type=F32)
        u_ref[j % 2, :, tn:2 * tn] = jnp.dot(h, wup_ref[:, ff + c0:ff + c0 + tn], preferred_element_type=F32)

    nj = ff // tn
    up(0)
    for j in range(nj):
        if j + 1 < nj:
            up(j + 1)
        c0 = j * tn
        a = jax.nn.gelu(conv(j % 2, 0, c0), approximate=True) * conv(j % 2, 1, ff + c0)
        a_ref[:, c0:c0 + tn] = a.astype(BF16)

    y = jnp.dot(a_ref[...], wd_ref[...], preferred_element_type=F32)
    out_ref[...] = x_ref[...] + gate_ref[...] * _rms(y, gy_ref[...])


def _ffn(x, g, sc, sh, gate, gy, w_up, cw, w_down, tm=512, tn=256):
    b, s, d = x.shape
    ff = w_down.shape[0]
    hb = tm // BF16_ROWS
    last = s // BF16_ROWS - 1

    def const(shape, **kw):
        return pl.BlockSpec(shape, lambda bi, i: (0,) * len(shape), **kw)

    def per_batch():
        return pl.BlockSpec((None, 1, d), lambda bi, i: (bi, 0, 0))

    resident = dict(pipeline_mode=pl.Buffered(1))
    return pl.pallas_call(
        functools.partial(_ffn_body, tm=tm, tn=tn),
        grid=(b, s // tm),
        in_specs=[pl.BlockSpec((None, tm, d), lambda bi, i: (bi, i, 0)),
                  pl.BlockSpec((None, BF16_ROWS, d), lambda bi, i: (bi, jnp.maximum(i * hb - 1, 0), 0)),
                  pl.BlockSpec((None, BF16_ROWS, d), lambda bi, i: (bi, jnp.minimum((i + 1) * hb, last), 0)),
                  const(g.shape), per_batch(), per_batch(), per_batch(), const(gy.shape),
                  const(w_up.shape, **resident), const(cw.shape), const(w_down.shape, **resident)],
        out_specs=pl.BlockSpec((None, tm, d), lambda bi, i: (bi, i, 0)),
        out_shape=jax.ShapeDtypeStruct((b, s, d), F32),
        scratch_shapes=[pltpu.VMEM((tm + 2 * BF16_ROWS, d), BF16),
                        pltpu.VMEM((2, tm + 2 * BF16_ROWS, 2 * tn), F32),
                        pltpu.VMEM((tm, ff), BF16)],
        compiler_params=_cparams(("parallel", "arbitrary")),
        name="ffn",
    )(x, x, x, g, sc, sh, gate, gy, w_up, cw, w_down)


def _pack_w_in(w_in):
    nl, d, _ = w_in.shape
    hd = HEAD_DIM
    aq = w_in[..., 0:256]
    aq = jnp.concatenate([aq[..., 0:hd], aq[..., 2 * hd:3 * hd], aq[..., hd:2 * hd], aq[..., 3 * hd:]], axis=-1)
    ckr = w_in[..., 1664:1696]
    ckr = jnp.concatenate([jnp.zeros((nl, d, C_NOPE_DIM), w_in.dtype), ckr,
                           jnp.zeros((nl, d, LANES - C_NOPE_DIM - C_ROPE_DIM), w_in.dtype)], axis=-1)
    return jnp.concatenate([aq, w_in[..., 256:1664], ckr, w_in[..., 1696:]], axis=-1).astype(BF16)


def _pack_w_uq(w):
    nl, r, _ = w.shape
    half = C_ROPE_DIM // 2
    w = w.reshape(nl, r, C_HEADS, C_NOPE_DIM + C_ROPE_DIM)
    nope, r1, r2 = w[..., :C_NOPE_DIM], w[..., C_NOPE_DIM:C_NOPE_DIM + half], w[..., C_NOPE_DIM + half:]
    z_tail = jnp.zeros((nl, r, C_HEADS, LANES - C_NOPE_DIM - C_ROPE_DIM), w.dtype)
    plain = jnp.concatenate([nope, r1, r2, z_tail], axis=-1).reshape(nl, r, C_HEADS * LANES)
    swapped = jnp.concatenate([jnp.zeros_like(nope), r2, r1, z_tail], axis=-1).reshape(nl, r, C_HEADS * LANES)
    return jnp.concatenate([plain, swapped], axis=-1).astype(BF16)


def _pack_w_ukv(w):
    nl, r, _ = w.shape
    w = w.reshape(nl, r, C_HEADS, C_NOPE_DIM + C_V_DIM)
    kn = jnp.concatenate([w[..., :C_NOPE_DIM], jnp.zeros((nl, r, C_HEADS, LANES - C_NOPE_DIM), w.dtype)], axis=-1)
    return jnp.concatenate([kn.reshape(nl, r, C_HEADS * LANES),
                            w[..., C_NOPE_DIM:].reshape(nl, r, C_HEADS * C_V_DIM)], axis=-1).astype(BF16)


def _pack_w_out(w_out):
    hd = HEAD_DIM
    return jnp.concatenate([w_out[:, 0:hd], w_out[:, 2 * hd:3 * hd], w_out[:, hd:2 * hd], w_out[:, 3 * hd:]],
                           axis=1).astype(BF16)


def _rope_tables(positions):
    half = C_ROPE_DIM // 2
    inv_freq = ROPE_THETA ** (-jnp.arange(half, dtype=jnp.float32) / half)
    ang = positions.astype(jnp.float32)[..., None] * inv_freq
    cos, sin = jnp.cos(ang), jnp.sin(ang)
    lead = positions.shape + (C_NOPE_DIM,)
    tail = positions.shape + (LANES - C_NOPE_DIM - C_ROPE_DIM,)
    cos_t = jnp.concatenate([jnp.ones(lead, F32), cos, cos, jnp.zeros(tail, F32)], axis=-1)
    sin_t = jnp.concatenate([jnp.zeros(lead, F32), -sin, sin, jnp.zeros(tail, F32)], axis=-1)
    return cos_t, sin_t


def _swap_matrix():
    half = C_ROPE_DIM // 2
    idx = jnp.arange(LANES)
    src = jnp.where((idx >= C_NOPE_DIM) & (idx < C_NOPE_DIM + half), idx + half,
                    jnp.where((idx >= C_NOPE_DIM + half) & (idx < C_NOPE_DIM + C_ROPE_DIM), idx - half, idx))
    return (idx[:, None] == src[None, :]).astype(BF16)


def kernel(x, c, positions, rel_bias, w_mod, b_mod, norm_g, w_in, a_sink, b_conv, c_norm_q, c_norm_kv,
           c_w_uq, c_w_ukv, w_out, w_up, ffn_conv, w_down):
    b, s, d = x.shape
    depth = w_in.shape[0]

    a_order = jnp.array(A_HEAD_ORDER)
    bias_a = _band_bias(rel_bias[:, :A_Q_HEADS][:, a_order], A_RADIUS, 1, A_SUB_Q)
    bias_d = [_band_bias(rel_bias[:, A_Q_HEADS + i * D_HEADS:A_Q_HEADS + (i + 1) * D_HEADS], (w // 2) // dil, dil,
                         D_SUB_Q)
              for i, (w, dil) in enumerate(D_PATTERNS)]
    cos_t, sin_t = _rope_tables(positions)
    swap = _swap_matrix()

    w_in_p = _pack_w_in(w_in)
    w_uq_p = _pack_w_uq(c_w_uq)
    w_ukv_p = _pack_w_ukv(c_w_ukv)
    w_out_p = _pack_w_out(w_out)
    w_up_b = w_up.astype(BF16)
    w_down_b = w_down.astype(BF16)

    mod = _modulation(c, w_mod, b_mod)

    for l in range(depth):
        sh1, sc1, g1, sh2, sc2, g2 = (mod[l, k].reshape(b, 1, d) for k in range(6))
        gains = norm_g[l]

        qkv, qkv_d4, qkv_d16 = _inproj(x, gains[0:1], sc1, sh1, w_in_p[l])

        sink = jnp.broadcast_to(jnp.repeat(a_sink[l][a_order], A_SUB_Q)[:, None], (A_Q_HEADS * A_SUB_Q, LANES))
        oa = _band_attention(qkv, bias_a, sink, dil=1, radius=A_RADIUS, tq=512, q_off=OFF_AQ, k_off=OFF_AK,
                             v_off=OFF_AV, kv_w=LANES, has_lse=False)[0].reshape(b, s, 2 * LANES)

        qc, kc, vc = _mla_prep(qkv, cos_t, sin_t, c_norm_q[l][None], c_norm_kv[l][None], w_uq_p[l], w_ukv_p[l],
                               swap)
        oc = _mla_attn(qc, kc, vc)

        od, lse = [], []
        for i, ((w, dil), src) in enumerate(zip(D_PATTERNS, (qkv, qkv_d4, qkv_d16))):
            off = OFF_D if dil == 1 else 0
            o_i, l_i = _band_attention(src, bias_d[i], None, dil=dil, radius=(w // 2) // dil, tq=512, q_off=off,
                                       k_off=off + 256, v_off=off + 512, kv_w=2 * LANES, has_lse=True,
                                       group=2 if dil == 16 else 1)
            od.append(o_i)
            lse.append(l_i)

        x = _outproj(x, oa, qkv, oc, od, lse, w_out_p[l], b_conv[l], gains[1:2], g1)
        x = _ffn(x, gains[2:3], sc2, sh2, g2, gains[3:4], w_up_b[l], ffn_conv[l], w_down_b[l])
    return x
```

```python
import functools
import math

import jax
import jax.numpy as jnp
from jax import lax
from jax.experimental import pallas as pl
from jax.experimental.pallas import tpu as pltpu

D_MODEL = 1024
HEAD_DIM = 64
A_Q_HEADS = 4
A_KV_HEADS = 2
A_RADIUS = 128
B_WIDTH = 256
C_HEADS = 4
C_Q_RANK = 256
C_KV_RANK = 128
C_NOPE_DIM = 64
C_ROPE_DIM = 32
C_V_DIM = 64
ROPE_THETA = 10000.0
D_HEADS = 4
D_PATTERNS = ((128, 1), (512, 4), (2048, 16))
REL_BUCKETS = 32
REL_MAX_DISTANCE = 1024
D_FF = 2816
EPS = 1e-6
NEG = -1e30

LANES = 128
BF16_ROWS = 16
A_SUB_Q = 128
D_SUB_Q = 128
VMEM_LIMIT = 48 * 1024 * 1024

IN_COLS = 4096
OFF_AQ, OFF_AK, OFF_AV = 0, 256, 384
OFF_BB, OFF_BC, OFF_BH = 512, 768, 1024
OFF_CQ, OFF_CKV, OFF_CKR = 1280, 1536, 1664
OFF_D = 1792
A_HEAD_ORDER = (0, 2, 1, 3)
PATTERN_COLS = 3 * D_HEADS * HEAD_DIM
MAIN_COLS = OFF_D + PATTERN_COLS
COL_CHUNK = 512

F32 = jnp.float32
BF16 = jnp.bfloat16


def _cparams(sem):
    return pltpu.CompilerParams(dimension_semantics=sem, vmem_limit_bytes=VMEM_LIMIT)


def _rms(v, g):
    return v * lax.rsqrt(jnp.mean(v * v, axis=-1, keepdims=True) + EPS) * g


def _mod_body(c_ref, w_ref, b_ref, o_ref):
    c = c_ref[...]
    act = c / (1.0 + jnp.exp(-c))
    o_ref[...] = jnp.dot(act, w_ref[...], preferred_element_type=F32,
                         precision=lax.Precision.HIGHEST) + b_ref[...]


def _modulation(c, w_mod, b_mod):
    nl, d, _ = w_mod.shape
    b = c.shape[0]
    return pl.pallas_call(
        _mod_body,
        grid=(nl, 6),
        in_specs=[pl.BlockSpec((b, d), lambda l, k: (0, 0)),
                  pl.BlockSpec((None, d, d), lambda l, k: (l, 0, k)),
                  pl.BlockSpec((None, 1, d), lambda l, k: (l, 0, k))],
        out_specs=pl.BlockSpec((None, None, b, d), lambda l, k: (l, k, 0, 0)),
        out_shape=jax.ShapeDtypeStruct((nl, 6, b, d), F32),
        compiler_params=_cparams(("parallel", "parallel")),
        name="modulation",
    )(c, w_mod, b_mod.reshape(nl, 1, 6 * d))


def _inproj_body(x_ref, g_ref, sc_ref, sh_ref, w_ref, om_ref, o4_ref, o16_ref, hf_ref, *, tm):
    h = _rms(x_ref[...], g_ref[...]) * (1.0 + sc_ref[...]) + sh_ref[...]
    nc = h.shape[-1] // LANES
    for c in range(nc):
        hf_ref[c] = h[:, c * LANES:(c + 1) * LANES]
    hb = h.astype(BF16)
    for c0 in range(0, MAIN_COLS, COL_CHUNK):
        om_ref[:, c0:c0 + COL_CHUNK] = jnp.dot(hb, w_ref[:, c0:c0 + COL_CHUNK],
                                               preferred_element_type=F32).astype(BF16)
    for o_ref, dil, c0 in ((o4_ref, 4, MAIN_COLS), (o16_ref, 16, MAIN_COLS + PATTERN_COLS)):
        n = tm // dil
        hp = jnp.concatenate(
            [jnp.concatenate([hf_ref[c, pl.ds(r, n, stride=dil), :] for r in range(dil)], axis=0)
             for c in range(nc)], axis=1).astype(BF16)
        res = jnp.dot(hp, w_ref[:, c0:c0 + PATTERN_COLS], preferred_element_type=F32).astype(BF16)
        for r in range(dil):
            o_ref[r] = res[r * n:(r + 1) * n]


def _inproj(x, g, sc, sh, w, tm=1024):
    b, s, d = x.shape
    d4, d16 = D_PATTERNS[1][1], D_PATTERNS[2][1]
    return pl.pallas_call(
        functools.partial(_inproj_body, tm=tm),
        grid=(b, s // tm),
        in_specs=[pl.BlockSpec((None, tm, d), lambda bi, i: (bi, i, 0)),
                  pl.BlockSpec((1, d), lambda bi, i: (0, 0)),
                  pl.BlockSpec((None, 1, d), lambda bi, i: (bi, 0, 0)),
                  pl.BlockSpec((None, 1, d), lambda bi, i: (bi, 0, 0)),
                  pl.BlockSpec(w.shape, lambda bi, i: (0, 0), pipeline_mode=pl.Buffered(1))],
        out_specs=[pl.BlockSpec((None, tm, MAIN_COLS), lambda bi, i: (bi, i, 0)),
                   pl.BlockSpec((None, d4, tm // d4, PATTERN_COLS), lambda bi, i: (bi, 0, i, 0)),
                   pl.BlockSpec((None, d16, tm // d16, PATTERN_COLS), lambda bi, i: (bi, 0, i, 0))],
        out_shape=[jax.ShapeDtypeStruct((b, s, MAIN_COLS), BF16),
                   jax.ShapeDtypeStruct((b, d4, s // d4, PATTERN_COLS), BF16),
                   jax.ShapeDtypeStruct((b, d16, s // d16, PATTERN_COLS), BF16)],
        scratch_shapes=[pltpu.VMEM((d // LANES, tm, LANES), F32)],
        compiler_params=_cparams(("parallel", "parallel")),
        name="inproj",
    )(x, g, sc, sh, w)


def _t5_bucket(rel):
    half = REL_BUCKETS // 2
    max_exact = half // 2
    n = jnp.abs(rel)
    n_f = jnp.maximum(n, 1).astype(jnp.float32)
    large = max_exact + (jnp.log(n_f / max_exact) / math.log(REL_MAX_DISTANCE / max_exact)
                         * (half - max_exact)).astype(jnp.int32)
    large = jnp.minimum(large, half - 1)
    return jnp.where(rel > 0, half, 0) + jnp.where(n < max_exact, n, large)


def _band_bias(table, radius, stride, sub_q):
    qi = jnp.arange(sub_q)[:, None]
    kj = jnp.arange(sub_q + 2 * radius)[None, :]
    off = kj - radius - qi
    onehot = (_t5_bucket(off * stride)[..., None] == jnp.arange(REL_BUCKETS)).astype(F32)
    bias = jnp.einsum('qkb,bh->hqk', onehot, table.astype(F32), precision=lax.Precision.HIGHEST)
    bias = jnp.where((jnp.abs(off) <= radius)[None], bias, NEG)
    return bias.reshape(-1, bias.shape[-1])


def _band_body(*refs, radius, tq, seq, kv_w, has_sink, has_lse):
    q_ref, kp_ref, kc_ref, kn_ref, vp_ref, vc_ref, vn_ref, bias_ref = refs[:8]
    rest = refs[8:]
    if has_sink:
        sink_ref, rest = rest[0], rest[1:]
    o_ref = rest[0]
    lse_ref = rest[1] if has_lse else None

    i = pl.program_id(2)
    nh = 4
    sub_q = bias_ref.shape[0] // nh
    nk = sub_q + 2 * radius
    per = kv_w // HEAD_DIM
    lane = lax.broadcasted_iota(jnp.int32, (1, kv_w), 1)
    hmask = [(lane >= HEAD_DIM * (h % per)) & (lane < HEAD_DIM * (h % per + 1)) for h in range(nh)]
    scale = HEAD_DIM ** -0.5

    def pick(rows):
        out = rows[(per - 1) * sub_q:per * sub_q]
        for h in range(per - 2, -1, -1):
            out = jnp.where(hmask[h], rows[h * sub_q:(h + 1) * sub_q], out)
        return out

    for rr in range(q_ref.shape[0]):
        kcat = jnp.concatenate([kp_ref[rr], kc_ref[rr], kn_ref[rr]], axis=0)
        vcat = jnp.concatenate([vp_ref[rr], vc_ref[rr], vn_ref[rr]], axis=0)
        for sb in range(tq // sub_q):
            r0 = sb * sub_q
            qs = q_ref[rr, r0:r0 + sub_q, :] * scale
            ks = kcat[r0:r0 + nk]
            vs = vcat[r0:r0 + nk]
            at_edge = r0 - radius < 0 or r0 + sub_q + radius > tq
            kpos = i * tq + (r0 - radius) + lax.broadcasted_iota(jnp.int32, (1, nk), 1)
            inb = (kpos >= 0) & (kpos < seq)
            parts = []
            for h in range(nh):
                src = qs[:, (h // per) * kv_w:(h // per + 1) * kv_w]
                parts.append(jnp.where(hmask[h], src, jnp.zeros_like(src)))
            qst = jnp.concatenate(parts, axis=0)
            sc = lax.dot_general(qst, ks, (((1,), (1,)), ((), ())), preferred_element_type=F32)
            sc = sc + bias_ref[...]
            if at_edge:
                sc = jnp.where(inb, sc, NEG)
            m = jnp.max(sc, axis=-1, keepdims=True)
            if has_sink:
                m = jnp.maximum(m, jnp.max(sink_ref[...], axis=-1, keepdims=True))
            p = jnp.exp(sc - m)
            den = jnp.sum(p, axis=-1, keepdims=True)
            if has_sink:
                den = den + jnp.max(jnp.exp(sink_ref[...] - m), axis=-1, keepdims=True)
            o = jnp.dot(p.astype(BF16), vs, preferred_element_type=F32) / den
            lse = jnp.broadcast_to(m + jnp.log(den), o.shape) if has_lse else None
            for g in range(nh // per):
                rows = slice(g * per * sub_q, (g + 1) * per * sub_q)
                cols = slice(g * kv_w, (g + 1) * kv_w)
                o_ref[rr, r0:r0 + sub_q, cols] = pick(o[rows]).astype(BF16)
                if has_lse:
                    lse_ref[rr, r0:r0 + sub_q, cols] = pick(lse[rows])


def _band_attention(qkv, bias, sink, *, dil, radius, tq, q_off, k_off, v_off, kv_w, has_lse, group=1):
    b = qkv.shape[0]
    seq = qkv.shape[-2]
    s = seq * dil
    tq = min(tq, seq)
    qw = 2 * LANES
    steps = tq // radius
    last = seq // radius - 1
    if dil == 1:
        qkv = qkv.reshape(b, 1, seq, qkv.shape[-1])

    def spec(rows, width, off, row_index):
        return pl.BlockSpec((None, group, rows, width), lambda bi, r, i: (bi, r, row_index(i), off // width))

    def cur(i):
        return i

    def prev(i):
        return jnp.maximum(i * steps - 1, 0)

    def nxt(i):
        return jnp.minimum((i + 1) * steps, last)

    in_specs = [spec(tq, qw, q_off, cur)]
    for off in (k_off, v_off):
        in_specs += [spec(radius, kv_w, off, prev), spec(tq, kv_w, off, cur), spec(radius, kv_w, off, nxt)]
    in_specs.append(pl.BlockSpec(bias.shape, lambda bi, r, i: (0, 0)))
    args = [qkv] * 7 + [bias]
    if sink is not None:
        in_specs.append(pl.BlockSpec(sink.shape, lambda bi, r, i: (0, 0)))
        args.append(sink)

    o_spec = pl.BlockSpec((None, group, tq, qw), lambda bi, r, i: (bi, r, i, 0))
    out_shape = [jax.ShapeDtypeStruct((b, dil, seq, qw), BF16)]
    out_specs = [o_spec]
    if has_lse:
        out_shape.append(jax.ShapeDtypeStruct((b, dil, seq, qw), F32))
        out_specs.append(o_spec)

    outs = pl.pallas_call(
        functools.partial(_band_body, radius=radius, tq=tq, seq=seq, kv_w=kv_w,
                          has_sink=sink is not None, has_lse=has_lse),
        grid=(b, dil // group, seq // tq),
        in_specs=in_specs,
        out_specs=out_specs,
        out_shape=out_shape,
        compiler_params=_cparams(("parallel", "parallel", "parallel")),
        name=f"band_attention_d{dil}_r{radius}",
    )(*args)
    return outs


def _mla_prep_body(cq_ref, ckv_ref, ckr_ref, cos_ref, sin_ref, gq_ref, gkv_ref, wq_ref, wkv_ref, swap_ref,
                   q_ref, k_ref, v_ref):
    cs = cos_ref[...]
    sn = sin_ref[...]
    scale = (C_NOPE_DIM + C_ROPE_DIM) ** -0.5
    nq = _rms(cq_ref[...].astype(F32), gq_ref[...]).astype(BF16)
    qq = jnp.dot(nq, wq_ref[...], preferred_element_type=F32)
    nkv = _rms(ckv_ref[...].astype(F32), gkv_ref[...]).astype(BF16)
    kvv = jnp.dot(nkv, wkv_ref[...], preferred_element_type=F32)
    kr = ckr_ref[...]
    kr_sw = jnp.dot(kr, swap_ref[...], preferred_element_type=F32)
    k_rope = kr.astype(F32) * cs + kr_sw * sn
    hw = C_HEADS * LANES
    for h in range(C_HEADS):
        a = qq[:, h * LANES:(h + 1) * LANES]
        a_sw = qq[:, hw + h * LANES:hw + (h + 1) * LANES]
        q_ref[:, h * LANES:(h + 1) * LANES] = ((a * cs + a_sw * sn) * scale).astype(BF16)
        k_ref[:, h * LANES:(h + 1) * LANES] = (kvv[:, h * LANES:(h + 1) * LANES] + k_rope).astype(BF16)
    v_ref[...] = kvv[:, hw:].astype(BF16)


def _mla_prep(qkv, cos_t, sin_t, gq, gkv, wq, wkv, swap, tm=512):
    b, s, n = qkv.shape
    hw = C_HEADS * LANES
    vw = C_HEADS * C_V_DIM

    def const(shape):
        return pl.BlockSpec(shape, lambda bi, i: (0,) * len(shape))

    def rows(width, off=0):
        return pl.BlockSpec((None, tm, width), lambda bi, i: (bi, i, off // width))

    return pl.pallas_call(
        _mla_prep_body,
        grid=(b, s // tm),
        in_specs=[rows(C_Q_RANK, OFF_CQ), rows(LANES, OFF_CKV), rows(LANES, OFF_CKR),
                  rows(LANES), rows(LANES),
                  const(gq.shape), const(gkv.shape), const(wq.shape), const(wkv.shape), const(swap.shape)],
        out_specs=[rows(hw), rows(hw), rows(vw)],
        out_shape=[jax.ShapeDtypeStruct((b, s, hw), BF16), jax.ShapeDtypeStruct((b, s, hw), BF16),
                   jax.ShapeDtypeStruct((b, s, vw), BF16)],
        compiler_params=_cparams(("parallel", "parallel")),
        name="mla_prep",
    )(qkv, qkv, qkv, cos_t, sin_t, gq, gkv, wq, wkv, swap)


def _mla_attn_body(q_ref, k_ref, v_ref, o_ref, *, rows):
    lo = lax.broadcasted_iota(jnp.int32, (1, LANES), 1) < C_V_DIM
    for r0 in range(0, q_ref.shape[0], rows):
        for g in range(C_HEADS // 2):
            acc = None
            for half in range(2):
                h = 2 * g + half
                q = q_ref[r0:r0 + rows, h * LANES:(h + 1) * LANES]
                k = k_ref[:, h * LANES:(h + 1) * LANES]
                sc = lax.dot_general(q, k, (((1,), (1,)), ((), ())), preferred_element_type=F32)
                m = jnp.max(sc, axis=-1, keepdims=True)
                p = jnp.exp(sc - m)
                den = jnp.sum(p, axis=-1, keepdims=True)
                o = jnp.dot(p.astype(BF16), v_ref[:, g * LANES:(g + 1) * LANES],
                            preferred_element_type=F32) / den
                acc = o if acc is None else jnp.where(lo, acc, o)
            o_ref[r0:r0 + rows, g * LANES:(g + 1) * LANES] = acc.astype(BF16)


def _mla_attn(q, k, v, tq=512, rows=256):
    b, s, hw = q.shape
    vw = v.shape[-1]
    return pl.pallas_call(
        functools.partial(_mla_attn_body, rows=rows),
        grid=(b, s // tq),
        in_specs=[pl.BlockSpec((None, tq, hw), lambda bi, i: (bi, i, 0)),
                  pl.BlockSpec((None, s, hw), lambda bi, i: (bi, 0, 0)),
                  pl.BlockSpec((None, s, vw), lambda bi, i: (bi, 0, 0))],
        out_specs=pl.BlockSpec((None, tq, vw), lambda bi, i: (bi, i, 0)),
        out_shape=jax.ShapeDtypeStruct((b, s, vw), BF16),
        compiler_params=_cparams(("parallel", "arbitrary")),
        name="mla_attention",
    )(q, k, v)


def _outproj_body(x_ref, oa_ref, bb_ref, bc_ref, bh_ref, bcp_ref, bhp_ref, bcn_ref, bhn_ref, oc_ref,
                  o0_ref, o1_ref, o2_ref, l0_ref, l1_ref, l2_ref, w_ref, cw_ref, gn_ref, gate_ref,
                  out_ref, conv_ref, *perm_refs, tm):
    i = pl.program_id(1)

    def natural(ref, s_ref):
        dil, n, width = ref.shape
        for r in range(dil):
            blk = ref[r].astype(F32)
            for c in range(width // LANES):
                s_ref[c, pl.ds(r, n, stride=dil), :] = blk[:, c * LANES:(c + 1) * LANES]
        return jnp.concatenate([s_ref[c] for c in range(width // LANES)], axis=1)

    pad = 8
    first = (i > 0).astype(F32)
    final = (i < pl.num_programs(1) - 1).astype(F32)
    conv_ref[pad:pad + tm, :] = bc_ref[...].astype(F32) * bh_ref[...].astype(F32)
    conv_ref[pad - 1:pad, :] = (bcp_ref[BF16_ROWS - 1:, :].astype(F32)
                                * bhp_ref[BF16_ROWS - 1:, :].astype(F32)) * first
    conv_ref[pad + tm:pad + tm + 1, :] = (bcn_ref[:1, :].astype(F32) * bhn_ref[:1, :].astype(F32)) * final
    cw = cw_ref[...]
    conv = (conv_ref[pad - 1:pad - 1 + tm, :] * cw[0:1] + conv_ref[pad:pad + tm, :] * cw[1:2]
            + conv_ref[pad + 1:pad + 1 + tm, :] * cw[2:3])
    ob = (bb_ref[...].astype(F32) * conv).astype(BF16)
    l0 = l0_ref[0]
    l1, l2 = natural(l1_ref, perm_refs[0]), natural(l2_ref, perm_refs[1])
    o1, o2 = natural(o1_ref, perm_refs[2]), natural(o2_ref, perm_refs[3])
    mx = jnp.maximum(jnp.maximum(l0, l1), l2)
    e0, e1, e2 = jnp.exp(l0 - mx), jnp.exp(l1 - mx), jnp.exp(l2 - mx)
    od = ((o0_ref[0].astype(F32) * e0 + o1 * e1 + o2 * e2) / (e0 + e1 + e2)).astype(BF16)
    wd = 2 * LANES
    y = jnp.dot(oa_ref[...], w_ref[0:wd, :], preferred_element_type=F32)
    y = y + jnp.dot(ob, w_ref[wd:2 * wd, :], preferred_element_type=F32)
    y = y + jnp.dot(oc_ref[...], w_ref[2 * wd:3 * wd, :], preferred_element_type=F32)
    y = y + jnp.dot(od, w_ref[3 * wd:4 * wd, :], preferred_element_type=F32)
    out_ref[...] = x_ref[...] + gate_ref[...] * _rms(y, gn_ref[...])


def _outproj(x, oa, qkv, oc, od, lse, w, cw, gn, gate, tm=512):
    b, s, d = x.shape
    wd = 2 * LANES
    hb = tm // BF16_ROWS
    last = s // BF16_ROWS - 1

    def rows(width, off=0):
        return pl.BlockSpec((None, tm, width), lambda bi, i: (bi, i, off // width))

    def halo_prev(off):
        return pl.BlockSpec((None, BF16_ROWS, wd), lambda bi, i: (bi, jnp.maximum(i * hb - 1, 0), off // wd))

    def halo_next(off):
        return pl.BlockSpec((None, BF16_ROWS, wd), lambda bi, i: (bi, jnp.minimum((i + 1) * hb, last), off // wd))

    def const(shape):
        return pl.BlockSpec(shape, lambda bi, i: (0,) * len(shape))

    def grouped(arr):
        dil = arr.shape[1]
        return pl.BlockSpec((None, dil, tm // dil, wd), lambda bi, i: (bi, 0, i, 0))

    return pl.pallas_call(
        functools.partial(_outproj_body, tm=tm),
        grid=(b, s // tm),
        in_specs=[rows(d), rows(wd), rows(wd, OFF_BB), rows(wd, OFF_BC), rows(wd, OFF_BH),
                  halo_prev(OFF_BC), halo_prev(OFF_BH), halo_next(OFF_BC), halo_next(OFF_BH),
                  rows(wd)] + [grouped(a) for a in od] + [grouped(a) for a in lse] + [
                  const(w.shape), const(cw.shape), const(gn.shape),
                  pl.BlockSpec((None, 1, d), lambda bi, i: (bi, 0, 0))],
        out_specs=rows(d),
        out_shape=jax.ShapeDtypeStruct((b, s, d), F32),
        scratch_shapes=[pltpu.VMEM((tm + 16, wd), F32)] + [pltpu.VMEM((wd // LANES, tm, LANES), F32)] * 4,
        compiler_params=_cparams(("parallel", "arbitrary")),
        name="outproj",
    )(x, oa, qkv, qkv, qkv, qkv, qkv, qkv, qkv, oc, od[0], od[1], od[2], lse[0], lse[1], lse[2],
      w, cw, gn, gate)


def _ffn_body(x_ref, xp_ref, xn_ref, g_ref, sc_ref, sh_ref, gate_ref, gy_ref, wup_ref, cw_ref, wd_ref, out_ref,
              hf_ref, u_ref, a_ref, *, tm, tn):
    i = pl.program_id(1)
    ff = wd_ref.shape[0]
    d = x_ref.shape[-1]
    nc = d // LANES
    sub = 8
    seg = tm // sub

    def mod(v):
        return _rms(v, g_ref[...]) * (1.0 + sc_ref[...]) + sh_ref[...]

    def to_tile_order(ref):
        return jnp.concatenate(
            [jnp.concatenate([ref[c, pl.ds(a, sub, stride=seg), :] for a in range(seg)], axis=0)
             for c in range(nc)], axis=1)

    def to_seq_order(ref):
        return jnp.concatenate(
            [jnp.concatenate([ref[c, pl.ds(b, seg, stride=sub), :] for b in range(sub)], axis=0)
             for c in range(nc)], axis=1)

    hm = mod(x_ref[...])
    for c in range(nc):
        hf_ref[c] = hm[:, c * LANES:(c + 1) * LANES]
    first = (i > 0).astype(F32)
    final = (i < pl.num_programs(1) - 1).astype(F32)
    edge = jnp.concatenate([mod(xp_ref[sub:, :]) * first, mod(xn_ref[:sub, :]) * final], axis=0)
    h = jnp.concatenate([to_tile_order(hf_ref), edge], axis=0).astype(BF16)

    row = lax.broadcasted_iota(jnp.int32, (sub, 1), 0)

    def conv(slot, half, c0):
        lanes = slice(half * tn, (half + 1) * tn)
        cw = cw_ref[:, c0:c0 + tn]
        mid = u_ref[slot, 0:tm, lanes]
        top = jnp.where(row == 0, pltpu.roll(u_ref[slot, tm:tm + sub, lanes], 1, axis=0),
                        pltpu.roll(u_ref[slot, tm - sub:tm, lanes], 1, axis=0))
        prev = jnp.concatenate([top, u_ref[slot, 0:tm - sub, lanes]], axis=0)
        bot = jnp.where(row == sub - 1, pltpu.roll(u_ref[slot, tm + sub:tm + 2 * sub, lanes], sub - 1, axis=0),
                        pltpu.roll(u_ref[slot, 0:sub, lanes], sub - 1, axis=0))
        nxt = jnp.concatenate([u_ref[slot, sub:tm, lanes], bot], axis=0)
        return prev * cw[0:1] + mid * cw[1:2] + nxt * cw[2:3]

    for j in range(ff // tn):
        slot = j % 2
        c0 = j * tn
        u_ref[slot, :, 0:tn] = jnp.dot(h, wup_ref[:, c0:c0 + tn], preferred_element_type=F32)
        u_ref[slot, :, tn:2 * tn] = jnp.dot(h, wup_ref[:, ff + c0:ff + c0 + tn], preferred_element_type=F32)
        a = jax.nn.gelu(conv(slot, 0, c0), approximate=True) * conv(slot, 1, ff + c0)
        a_ref[:, c0:c0 + tn] = a.astype(BF16)

    y = jnp.dot(a_ref[...], wd_ref[...], preferred_element_type=F32)
    for c in range(nc):
        hf_ref[c] = y[:, c * LANES:(c + 1) * LANES]
    out_ref[...] = x_ref[...] + gate_ref[...] * _rms(to_seq_order(hf_ref), gy_ref[...])


def _ffn(x, g, sc, sh, gate, gy, w_up, cw, w_down, tm=512, tn=256):
    b, s, d = x.shape
    ff = w_down.shape[0]
    hb = tm // BF16_ROWS
    last = s // BF16_ROWS - 1

    def const(shape, **kw):
        return pl.BlockSpec(shape, lambda bi, i: (0,) * len(shape), **kw)

    def per_batch():
        return pl.BlockSpec((None, 1, d), lambda bi, i: (bi, 0, 0))

    resident = dict(pipeline_mode=pl.Buffered(1))
    return pl.pallas_call(
        functools.partial(_ffn_body, tm=tm, tn=tn),
        grid=(b, s // tm),
        in_specs=[pl.BlockSpec((None, tm, d), lambda bi, i: (bi, i, 0)),
                  pl.BlockSpec((None, BF16_ROWS, d), lambda bi, i: (bi, jnp.maximum(i * hb - 1, 0), 0)),
                  pl.BlockSpec((None, BF16_ROWS, d), lambda bi, i: (bi, jnp.minimum((i + 1) * hb, last), 0)),
                  const(g.shape), per_batch(), per_batch(), per_batch(), const(gy.shape),
                  const(w_up.shape, **resident), const(cw.shape), const(w_down.shape, **resident)],
        out_specs=pl.BlockSpec((None, tm, d), lambda bi, i: (bi, i, 0)),
        out_shape=jax.ShapeDtypeStruct((b, s, d), F32),
        scratch_shapes=[pltpu.VMEM((d // LANES, tm, LANES), F32),
                        pltpu.VMEM((2, tm + BF16_ROWS, 2 * tn), F32),
                        pltpu.VMEM((tm, ff), BF16)],
        compiler_params=_cparams(("parallel", "arbitrary")),
        name="ffn",
    )(x, x, x, g, sc, sh, gate, gy, w_up, cw, w_down)


def _pack_w_in(w_in):
    nl, d, _ = w_in.shape
    hd = HEAD_DIM
    aq = w_in[..., 0:256]
    aq = jnp.concatenate([aq[..., 0:hd], aq[..., 2 * hd:3 * hd], aq[..., hd:2 * hd], aq[..., 3 * hd:]], axis=-1)
    ckr = w_in[..., 1664:1696]
    ckr = jnp.concatenate([jnp.zeros((nl, d, C_NOPE_DIM), w_in.dtype), ckr,
                           jnp.zeros((nl, d, LANES - C_NOPE_DIM - C_ROPE_DIM), w_in.dtype)], axis=-1)
    return jnp.concatenate([aq, w_in[..., 256:1664], ckr, w_in[..., 1696:]], axis=-1).astype(BF16)


def _pack_w_uq(w):
    nl, r, _ = w.shape
    half = C_ROPE_DIM // 2
    w = w.reshape(nl, r, C_HEADS, C_NOPE_DIM + C_ROPE_DIM)
    nope, r1, r2 = w[..., :C_NOPE_DIM], w[..., C_NOPE_DIM:C_NOPE_DIM + half], w[..., C_NOPE_DIM + half:]
    z_tail = jnp.zeros((nl, r, C_HEADS, LANES - C_NOPE_DIM - C_ROPE_DIM), w.dtype)
    plain = jnp.concatenate([nope, r1, r2, z_tail], axis=-1).reshape(nl, r, C_HEADS * LANES)
    swapped = jnp.concatenate([jnp.zeros_like(nope), r2, r1, z_tail], axis=-1).reshape(nl, r, C_HEADS * LANES)
    return jnp.concatenate([plain, swapped], axis=-1).astype(BF16)


def _pack_w_ukv(w):
    nl, r, _ = w.shape
    w = w.reshape(nl, r, C_HEADS, C_NOPE_DIM + C_V_DIM)
    kn = jnp.concatenate([w[..., :C_NOPE_DIM], jnp.zeros((nl, r, C_HEADS, LANES - C_NOPE_DIM), w.dtype)], axis=-1)
    return jnp.concatenate([kn.reshape(nl, r, C_HEADS * LANES),
                            w[..., C_NOPE_DIM:].reshape(nl, r, C_HEADS * C_V_DIM)], axis=-1).astype(BF16)


def _pack_w_out(w_out):
    hd = HEAD_DIM
    return jnp.concatenate([w_out[:, 0:hd], w_out[:, 2 * hd:3 * hd], w_out[:, hd:2 * hd], w_out[:, 3 * hd:]],
                           axis=1).astype(BF16)


def _rope_tables(positions):
    half = C_ROPE_DIM // 2
    inv_freq = ROPE_THETA ** (-jnp.arange(half, dtype=jnp.float32) / half)
    ang = positions.astype(jnp.float32)[..., None] * inv_freq
    cos, sin = jnp.cos(ang), jnp.sin(ang)
    lead = positions.shape + (C_NOPE_DIM,)
    tail = positions.shape + (LANES - C_NOPE_DIM - C_ROPE_DIM,)
    cos_t = jnp.concatenate([jnp.ones(lead, F32), cos, cos, jnp.zeros(tail, F32)], axis=-1)
    sin_t = jnp.concatenate([jnp.zeros(lead, F32), -sin, sin, jnp.zeros(tail, F32)], axis=-1)
    return cos_t, sin_t


def _swap_matrix():
    half = C_ROPE_DIM // 2
    idx = jnp.arange(LANES)
    src = jnp.where((idx >= C_NOPE_DIM) & (idx < C_NOPE_DIM + half), idx + half,
                    jnp.where((idx >= C_NOPE_DIM + half) & (idx < C_NOPE_DIM + C_ROPE_DIM), idx - half, idx))
    return (idx[:, None] == src[None, :]).astype(BF16)


def kernel(x, c, positions, rel_bias, w_mod, b_mod, norm_g, w_in, a_sink, b_conv, c_norm_q, c_norm_kv,
           c_w_uq, c_w_ukv, w_out, w_up, ffn_conv, w_down):
    b, s, d = x.shape
    depth = w_in.shape[0]

    a_order = jnp.array(A_HEAD_ORDER)
    bias_a = _band_bias(rel_bias[:, :A_Q_HEADS][:, a_order], A_RADIUS, 1, A_SUB_Q)
    bias_d = [_band_bias(rel_bias[:, A_Q_HEADS + i * D_HEADS:A_Q_HEADS + (i + 1) * D_HEADS], (w // 2) // dil, dil,
                         D_SUB_Q)
              for i, (w, dil) in enumerate(D_PATTERNS)]
    cos_t, sin_t = _rope_tables(positions)
    swap = _swap_matrix()

    w_in_p = _pack_w_in(w_in)
    w_uq_p = _pack_w_uq(c_w_uq)
    w_ukv_p = _pack_w_ukv(c_w_ukv)
    w_out_p = _pack_w_out(w_out)
    w_up_b = w_up.astype(BF16)
    w_down_b = w_down.astype(BF16)

    mod = _modulation(c, w_mod, b_mod)

    for l in range(depth):
        sh1, sc1, g1, sh2, sc2, g2 = (mod[l, k].reshape(b, 1, d) for k in range(6))
        gains = norm_g[l]

        qkv, qkv_d4, qkv_d16 = _inproj(x, gains[0:1], sc1, sh1, w_in_p[l])

        sink = jnp.broadcast_to(jnp.repeat(a_sink[l][a_order], A_SUB_Q)[:, None], (A_Q_HEADS * A_SUB_Q, LANES))
        oa = _band_attention(qkv, bias_a, sink, dil=1, radius=A_RADIUS, tq=512, q_off=OFF_AQ, k_off=OFF_AK,
                             v_off=OFF_AV, kv_w=LANES, has_lse=False)[0].reshape(b, s, 2 * LANES)

        qc, kc, vc = _mla_prep(qkv, cos_t, sin_t, c_norm_q[l][None], c_norm_kv[l][None], w_uq_p[l], w_ukv_p[l],
                               swap)
        oc = _mla_attn(qc, kc, vc)

        od, lse = [], []
        for i, ((w, dil), src) in enumerate(zip(D_PATTERNS, (qkv, qkv_d4, qkv_d16))):
            off = OFF_D if dil == 1 else 0
            o_i, l_i = _band_attention(src, bias_d[i], None, dil=dil, radius=(w // 2) // dil, tq=512, q_off=off,
                                       k_off=off + 256, v_off=off + 512, kv_w=2 * LANES, has_lse=True,
                                       group=2 if dil == 16 else 1)
            od.append(o_i)
            lse.append(l_i)

        x = _outproj(x, oa, qkv, oc, od, lse, w_out_p[l], b_conv[l], gains[1:2], g1)
        x = _ffn(x, gains[2:3], sc2, sh2, g2, gains[3:4], w_up_b[l], ffn_conv[l], w_down_b[l])
    return x
```

```python
import functools
import math

import jax
import jax.numpy as jnp
from jax import lax
from jax.experimental import pallas as pl
from jax.experimental.pallas import tpu as pltpu

D_MODEL = 1024
HEAD_DIM = 64
A_Q_HEADS = 4
A_KV_HEADS = 2
A_RADIUS = 128
B_WIDTH = 256
C_HEADS = 4
C_Q_RANK = 256
C_KV_RANK = 128
C_NOPE_DIM = 64
C_ROPE_DIM = 32
C_V_DIM = 64
ROPE_THETA = 10000.0
D_HEADS = 4
D_PATTERNS = ((128, 1), (512, 4), (2048, 16))
REL_BUCKETS = 32
REL_MAX_DISTANCE = 1024
D_FF = 2816
EPS = 1e-6
NEG = -1e30

LANES = 128
BF16_ROWS = 16
A_SUB_Q = 128
D_SUB_Q = 128
VMEM_LIMIT = 48 * 1024 * 1024

IN_COLS = 4096
OFF_AQ, OFF_AK, OFF_AV = 0, 256, 384
OFF_BB, OFF_BC, OFF_BH = 512, 768, 1024
OFF_CQ, OFF_CKV, OFF_CKR = 1280, 1536, 1664
OFF_D = 1792
A_HEAD_ORDER = (0, 2, 1, 3)
PATTERN_COLS = 3 * D_HEADS * HEAD_DIM
MAIN_COLS = OFF_D + PATTERN_COLS
COL_CHUNK = 512

F32 = jnp.float32
BF16 = jnp.bfloat16


def _layer_spec(w, layer, **kw):
    return pl.BlockSpec((None,) + w.shape[1:], lambda *_: (layer,) + (0,) * (w.ndim - 1), **kw)


def _cparams(sem):
    return pltpu.CompilerParams(dimension_semantics=sem, vmem_limit_bytes=VMEM_LIMIT)


def _rms(v, g):
    return v * lax.rsqrt(jnp.mean(v * v, axis=-1, keepdims=True) + EPS) * g


def _mod_body(c_ref, w_ref, b_ref, o_ref):
    c = c_ref[...]
    act = c / (1.0 + jnp.exp(-c))
    o_ref[...] = jnp.dot(act, w_ref[...], preferred_element_type=F32,
                         precision=lax.Precision.HIGHEST) + b_ref[...]


def _modulation(c, w_mod, b_mod):
    nl, d, _ = w_mod.shape
    b = c.shape[0]
    return pl.pallas_call(
        _mod_body,
        grid=(nl, 6),
        in_specs=[pl.BlockSpec((b, d), lambda l, k: (0, 0)),
                  pl.BlockSpec((None, d, d), lambda l, k: (l, 0, k)),
                  pl.BlockSpec((None, 1, d), lambda l, k: (l, 0, k))],
        out_specs=pl.BlockSpec((None, None, b, d), lambda l, k: (l, k, 0, 0)),
        out_shape=jax.ShapeDtypeStruct((nl, 6, b, d), F32),
        compiler_params=_cparams(("parallel", "parallel")),
        name="modulation",
    )(c, w_mod, b_mod.reshape(nl, 1, 6 * d))


def _inproj_body(x_ref, g_ref, sc_ref, sh_ref, w_ref, om_ref, o4_ref, o16_ref, hf_ref, *, tm):
    h = _rms(x_ref[...], g_ref[...]) * (1.0 + sc_ref[...]) + sh_ref[...]
    nc = h.shape[-1] // LANES
    for c in range(nc):
        hf_ref[c] = h[:, c * LANES:(c + 1) * LANES]
    hb = h.astype(BF16)
    for c0 in range(0, MAIN_COLS, COL_CHUNK):
        om_ref[:, c0:c0 + COL_CHUNK] = jnp.dot(hb, w_ref[:, c0:c0 + COL_CHUNK],
                                               preferred_element_type=F32).astype(BF16)
    for o_ref, dil, c0 in ((o4_ref, 4, MAIN_COLS), (o16_ref, 16, MAIN_COLS + PATTERN_COLS)):
        n = tm // dil
        hp = jnp.concatenate(
            [jnp.concatenate([hf_ref[c, pl.ds(r, n, stride=dil), :] for r in range(dil)], axis=0)
             for c in range(nc)], axis=1).astype(BF16)
        res = jnp.dot(hp, w_ref[:, c0:c0 + PATTERN_COLS], preferred_element_type=F32).astype(BF16)
        for r in range(dil):
            o_ref[r] = res[r * n:(r + 1) * n]


def _inproj(x, g, sc, sh, w, layer, tm=1024):
    b, s, d = x.shape
    d4, d16 = D_PATTERNS[1][1], D_PATTERNS[2][1]
    return pl.pallas_call(
        functools.partial(_inproj_body, tm=tm),
        grid=(b, s // tm),
        in_specs=[pl.BlockSpec((None, tm, d), lambda bi, i: (bi, i, 0)),
                  pl.BlockSpec((1, d), lambda bi, i: (0, 0)),
                  pl.BlockSpec((None, 1, d), lambda bi, i: (bi, 0, 0)),
                  pl.BlockSpec((None, 1, d), lambda bi, i: (bi, 0, 0)),
                  _layer_spec(w, layer, pipeline_mode=pl.Buffered(1))],
        out_specs=[pl.BlockSpec((None, tm, MAIN_COLS), lambda bi, i: (bi, i, 0)),
                   pl.BlockSpec((None, d4, tm // d4, PATTERN_COLS), lambda bi, i: (bi, 0, i, 0)),
                   pl.BlockSpec((None, d16, tm // d16, PATTERN_COLS), lambda bi, i: (bi, 0, i, 0))],
        out_shape=[jax.ShapeDtypeStruct((b, s, MAIN_COLS), BF16),
                   jax.ShapeDtypeStruct((b, d4, s // d4, PATTERN_COLS), BF16),
                   jax.ShapeDtypeStruct((b, d16, s // d16, PATTERN_COLS), BF16)],
        scratch_shapes=[pltpu.VMEM((d // LANES, tm, LANES), F32)],
        compiler_params=_cparams(("parallel", "parallel")),
        name="inproj",
    )(x, g, sc, sh, w)


def _t5_bucket(rel):
    half = REL_BUCKETS // 2
    max_exact = half // 2
    n = jnp.abs(rel)
    n_f = jnp.maximum(n, 1).astype(jnp.float32)
    large = max_exact + (jnp.log(n_f / max_exact) / math.log(REL_MAX_DISTANCE / max_exact)
                         * (half - max_exact)).astype(jnp.int32)
    large = jnp.minimum(large, half - 1)
    return jnp.where(rel > 0, half, 0) + jnp.where(n < max_exact, n, large)


def _band_bias(table, radius, stride, sub_q):
    qi = jnp.arange(sub_q)[:, None]
    kj = jnp.arange(sub_q + 2 * radius)[None, :]
    off = kj - radius - qi
    onehot = (_t5_bucket(off * stride)[..., None] == jnp.arange(REL_BUCKETS)).astype(F32)
    bias = jnp.einsum('qkb,bh->hqk', onehot, table.astype(F32), precision=lax.Precision.HIGHEST)
    bias = jnp.where((jnp.abs(off) <= radius)[None], bias, NEG)
    return bias.reshape(-1, bias.shape[-1])


def _band_body(*refs, radius, tq, seq, kv_w, has_sink, has_lse):
    q_ref, kp_ref, kc_ref, kn_ref, vp_ref, vc_ref, vn_ref, bias_ref = refs[:8]
    rest = refs[8:]
    if has_sink:
        sink_ref, rest = rest[0], rest[1:]
    o_ref = rest[0]
    lse_ref = rest[1] if has_lse else None

    i = pl.program_id(2)
    nh = 4
    sub_q = bias_ref.shape[0] // nh
    nk = sub_q + 2 * radius
    per = kv_w // HEAD_DIM
    lane = lax.broadcasted_iota(jnp.int32, (1, kv_w), 1)
    hmask = [(lane >= HEAD_DIM * (h % per)) & (lane < HEAD_DIM * (h % per + 1)) for h in range(nh)]
    scale = HEAD_DIM ** -0.5

    def pick(rows):
        out = rows[(per - 1) * sub_q:per * sub_q]
        for h in range(per - 2, -1, -1):
            out = jnp.where(hmask[h], rows[h * sub_q:(h + 1) * sub_q], out)
        return out

    for rr in range(q_ref.shape[0]):
        kcat = jnp.concatenate([kp_ref[rr], kc_ref[rr], kn_ref[rr]], axis=0)
        vcat = jnp.concatenate([vp_ref[rr], vc_ref[rr], vn_ref[rr]], axis=0)
        for sb in range(tq // sub_q):
            r0 = sb * sub_q
            qs = q_ref[rr, r0:r0 + sub_q, :] * scale
            ks = kcat[r0:r0 + nk]
            vs = vcat[r0:r0 + nk]
            at_edge = r0 - radius < 0 or r0 + sub_q + radius > tq
            kpos = i * tq + (r0 - radius) + lax.broadcasted_iota(jnp.int32, (1, nk), 1)
            inb = (kpos >= 0) & (kpos < seq)
            parts = []
            for h in range(nh):
                src = qs[:, (h // per) * kv_w:(h // per + 1) * kv_w]
                parts.append(jnp.where(hmask[h], src, jnp.zeros_like(src)))
            qst = jnp.concatenate(parts, axis=0)
            sc = lax.dot_general(qst, ks, (((1,), (1,)), ((), ())), preferred_element_type=F32)
            sc = sc + bias_ref[...]
            if at_edge:
                sc = jnp.where(inb, sc, NEG)
            m = jnp.max(sc, axis=-1, keepdims=True)
            if has_sink:
                m = jnp.maximum(m, jnp.max(sink_ref[...], axis=-1, keepdims=True))
            p = jnp.exp(sc - m)
            den = jnp.sum(p, axis=-1, keepdims=True)
            if has_sink:
                den = den + jnp.max(jnp.exp(sink_ref[...] - m), axis=-1, keepdims=True)
            o = jnp.dot(p.astype(BF16), vs, preferred_element_type=F32) / den
            lse = jnp.broadcast_to(m + jnp.log(den), o.shape) if has_lse else None
            for g in range(nh // per):
                rows = slice(g * per * sub_q, (g + 1) * per * sub_q)
                cols = slice(g * kv_w, (g + 1) * kv_w)
                o_ref[rr, r0:r0 + sub_q, cols] = pick(o[rows]).astype(BF16)
                if has_lse:
                    lse_ref[rr, r0:r0 + sub_q, cols] = pick(lse[rows])


def _band_attention(qkv, bias, sink, *, dil, radius, tq, q_off, k_off, v_off, kv_w, has_lse, group=1):
    b = qkv.shape[0]
    seq = qkv.shape[-2]
    s = seq * dil
    tq = min(tq, seq)
    qw = 2 * LANES
    steps = tq // radius
    last = seq // radius - 1
    if dil == 1:
        qkv = qkv.reshape(b, 1, seq, qkv.shape[-1])

    def spec(rows, width, off, row_index):
        return pl.BlockSpec((None, group, rows, width), lambda bi, r, i: (bi, r, row_index(i), off // width))

    def cur(i):
        return i

    def prev(i):
        return jnp.maximum(i * steps - 1, 0)

    def nxt(i):
        return jnp.minimum((i + 1) * steps, last)

    in_specs = [spec(tq, qw, q_off, cur)]
    for off in (k_off, v_off):
        in_specs += [spec(radius, kv_w, off, prev), spec(tq, kv_w, off, cur), spec(radius, kv_w, off, nxt)]
    in_specs.append(pl.BlockSpec(bias.shape, lambda bi, r, i: (0, 0)))
    args = [qkv] * 7 + [bias]
    if sink is not None:
        in_specs.append(pl.BlockSpec(sink.shape, lambda bi, r, i: (0, 0)))
        args.append(sink)

    o_spec = pl.BlockSpec((None, group, tq, qw), lambda bi, r, i: (bi, r, i, 0))
    out_shape = [jax.ShapeDtypeStruct((b, dil, seq, qw), BF16)]
    out_specs = [o_spec]
    if has_lse:
        out_shape.append(jax.ShapeDtypeStruct((b, dil, seq, qw), F32))
        out_specs.append(o_spec)

    outs = pl.pallas_call(
        functools.partial(_band_body, radius=radius, tq=tq, seq=seq, kv_w=kv_w,
                          has_sink=sink is not None, has_lse=has_lse),
        grid=(b, dil // group, seq // tq),
        in_specs=in_specs,
        out_specs=out_specs,
        out_shape=out_shape,
        compiler_params=_cparams(("parallel", "parallel", "parallel")),
        name=f"band_attention_d{dil}_r{radius}",
    )(*args)
    return outs


def _mla_prep_body(cq_ref, ckv_ref, ckr_ref, cos_ref, sin_ref, gq_ref, gkv_ref, wq_ref, wkv_ref, swap_ref,
                   q_ref, k_ref, v_ref):
    cs = cos_ref[...]
    sn = sin_ref[...]
    scale = (C_NOPE_DIM + C_ROPE_DIM) ** -0.5
    nq = _rms(cq_ref[...].astype(F32), gq_ref[...]).astype(BF16)
    qq = jnp.dot(nq, wq_ref[...], preferred_element_type=F32)
    nkv = _rms(ckv_ref[...].astype(F32), gkv_ref[...]).astype(BF16)
    kvv = jnp.dot(nkv, wkv_ref[...], preferred_element_type=F32)
    kr = ckr_ref[...]
    kr_sw = jnp.dot(kr, swap_ref[...], preferred_element_type=F32)
    k_rope = kr.astype(F32) * cs + kr_sw * sn
    hw = C_HEADS * LANES
    for h in range(C_HEADS):
        a = qq[:, h * LANES:(h + 1) * LANES]
        a_sw = qq[:, hw + h * LANES:hw + (h + 1) * LANES]
        q_ref[:, h * LANES:(h + 1) * LANES] = ((a * cs + a_sw * sn) * scale).astype(BF16)
        k_ref[:, h * LANES:(h + 1) * LANES] = (kvv[:, h * LANES:(h + 1) * LANES] + k_rope).astype(BF16)
    v_ref[...] = kvv[:, hw:].astype(BF16)


def _mla_prep(qkv, cos_t, sin_t, gq, gkv, wq, wkv, swap, layer, tm=512):
    b, s, n = qkv.shape
    hw = C_HEADS * LANES
    vw = C_HEADS * C_V_DIM

    def const(shape):
        return pl.BlockSpec(shape, lambda bi, i: (0,) * len(shape))

    def rows(width, off=0):
        return pl.BlockSpec((None, tm, width), lambda bi, i: (bi, i, off // width))

    return pl.pallas_call(
        _mla_prep_body,
        grid=(b, s // tm),
        in_specs=[rows(C_Q_RANK, OFF_CQ), rows(LANES, OFF_CKV), rows(LANES, OFF_CKR),
                  rows(LANES), rows(LANES),
                  const(gq.shape), const(gkv.shape), _layer_spec(wq, layer), _layer_spec(wkv, layer),
                  const(swap.shape)],
        out_specs=[rows(hw), rows(hw), rows(vw)],
        out_shape=[jax.ShapeDtypeStruct((b, s, hw), BF16), jax.ShapeDtypeStruct((b, s, hw), BF16),
                   jax.ShapeDtypeStruct((b, s, vw), BF16)],
        compiler_params=_cparams(("parallel", "parallel")),
        name="mla_prep",
    )(qkv, qkv, qkv, cos_t, sin_t, gq, gkv, wq, wkv, swap)


def _mla_attn_body(q_ref, k_ref, v_ref, o_ref, *, rows):
    lo = lax.broadcasted_iota(jnp.int32, (1, LANES), 1) < C_V_DIM
    for r0 in range(0, q_ref.shape[0], rows):
        for g in range(C_HEADS // 2):
            acc = None
            for half in range(2):
                h = 2 * g + half
                q = q_ref[r0:r0 + rows, h * LANES:(h + 1) * LANES]
                k = k_ref[:, h * LANES:(h + 1) * LANES]
                sc = lax.dot_general(q, k, (((1,), (1,)), ((), ())), preferred_element_type=F32)
                m = jnp.max(sc, axis=-1, keepdims=True)
                p = jnp.exp(sc - m)
                den = jnp.sum(p, axis=-1, keepdims=True)
                o = jnp.dot(p.astype(BF16), v_ref[:, g * LANES:(g + 1) * LANES],
                            preferred_element_type=F32) / den
                acc = o if acc is None else jnp.where(lo, acc, o)
            o_ref[r0:r0 + rows, g * LANES:(g + 1) * LANES] = acc.astype(BF16)


def _mla_attn(q, k, v, tq=1024, rows=256):
    b, s, hw = q.shape
    vw = v.shape[-1]
    return pl.pallas_call(
        functools.partial(_mla_attn_body, rows=rows),
        grid=(b, s // tq),
        in_specs=[pl.BlockSpec((None, tq, hw), lambda bi, i: (bi, i, 0)),
                  pl.BlockSpec((None, s, hw), lambda bi, i: (bi, 0, 0)),
                  pl.BlockSpec((None, s, vw), lambda bi, i: (bi, 0, 0))],
        out_specs=pl.BlockSpec((None, tq, vw), lambda bi, i: (bi, i, 0)),
        out_shape=jax.ShapeDtypeStruct((b, s, vw), BF16),
        compiler_params=_cparams(("parallel", "arbitrary")),
        name="mla_attention",
    )(q, k, v)


def _outproj_body(x_ref, oa_ref, bb_ref, bc_ref, bh_ref, bcp_ref, bhp_ref, bcn_ref, bhn_ref, oc_ref,
                  o0_ref, o1_ref, o2_ref, l0_ref, l1_ref, l2_ref, w_ref, cw_ref, gn_ref, gate_ref,
                  out_ref, conv_ref, *perm_refs, tm):
    i = pl.program_id(1)

    def natural(ref, s_ref):
        dil, n, width = ref.shape
        for r in range(dil):
            blk = ref[r].astype(F32)
            for c in range(width // LANES):
                s_ref[c, pl.ds(r, n, stride=dil), :] = blk[:, c * LANES:(c + 1) * LANES]
        return jnp.concatenate([s_ref[c] for c in range(width // LANES)], axis=1)

    pad = 8
    first = (i > 0).astype(F32)
    final = (i < pl.num_programs(1) - 1).astype(F32)
    conv_ref[pad:pad + tm, :] = bc_ref[...].astype(F32) * bh_ref[...].astype(F32)
    conv_ref[pad - 1:pad, :] = (bcp_ref[BF16_ROWS - 1:, :].astype(F32)
                                * bhp_ref[BF16_ROWS - 1:, :].astype(F32)) * first
    conv_ref[pad + tm:pad + tm + 1, :] = (bcn_ref[:1, :].astype(F32) * bhn_ref[:1, :].astype(F32)) * final
    cw = cw_ref[...]
    conv = (conv_ref[pad - 1:pad - 1 + tm, :] * cw[0:1] + conv_ref[pad:pad + tm, :] * cw[1:2]
            + conv_ref[pad + 1:pad + 1 + tm, :] * cw[2:3])
    ob = (bb_ref[...].astype(F32) * conv).astype(BF16)
    l0 = l0_ref[0]
    l1, l2 = natural(l1_ref, perm_refs[0]), natural(l2_ref, perm_refs[1])
    o1, o2 = natural(o1_ref, perm_refs[2]), natural(o2_ref, perm_refs[3])
    mx = jnp.maximum(jnp.maximum(l0, l1), l2)
    e0, e1, e2 = jnp.exp(l0 - mx), jnp.exp(l1 - mx), jnp.exp(l2 - mx)
    od = ((o0_ref[0].astype(F32) * e0 + o1 * e1 + o2 * e2) / (e0 + e1 + e2)).astype(BF16)
    wd = 2 * LANES
    y = jnp.dot(oa_ref[...], w_ref[0:wd, :], preferred_element_type=F32)
    y = y + jnp.dot(ob, w_ref[wd:2 * wd, :], preferred_element_type=F32)
    y = y + jnp.dot(oc_ref[...], w_ref[2 * wd:3 * wd, :], preferred_element_type=F32)
    y = y + jnp.dot(od, w_ref[3 * wd:4 * wd, :], preferred_element_type=F32)
    out_ref[...] = x_ref[...] + gate_ref[...] * _rms(y, gn_ref[...])


def _outproj(x, oa, qkv, oc, od, lse, w, cw, gn, gate, layer, tm=512):
    b, s, d = x.shape
    wd = 2 * LANES
    hb = tm // BF16_ROWS
    last = s // BF16_ROWS - 1

    def rows(width, off=0):
        return pl.BlockSpec((None, tm, width), lambda bi, i: (bi, i, off // width))

    def halo_prev(off):
        return pl.BlockSpec((None, BF16_ROWS, wd), lambda bi, i: (bi, jnp.maximum(i * hb - 1, 0), off // wd))

    def halo_next(off):
        return pl.BlockSpec((None, BF16_ROWS, wd), lambda bi, i: (bi, jnp.minimum((i + 1) * hb, last), off // wd))

    def const(shape):
        return pl.BlockSpec(shape, lambda bi, i: (0,) * len(shape))

    def grouped(arr):
        dil = arr.shape[1]
        return pl.BlockSpec((None, dil, tm // dil, wd), lambda bi, i: (bi, 0, i, 0))

    return pl.pallas_call(
        functools.partial(_outproj_body, tm=tm),
        grid=(b, s // tm),
        in_specs=[rows(d), rows(wd), rows(wd, OFF_BB), rows(wd, OFF_BC), rows(wd, OFF_BH),
                  halo_prev(OFF_BC), halo_prev(OFF_BH), halo_next(OFF_BC), halo_next(OFF_BH),
                  rows(wd)] + [grouped(a) for a in od] + [grouped(a) for a in lse] + [
                  _layer_spec(w, layer), _layer_spec(cw, layer), const(gn.shape),
                  pl.BlockSpec((None, 1, d), lambda bi, i: (bi, 0, 0))],
        out_specs=rows(d),
        out_shape=jax.ShapeDtypeStruct((b, s, d), F32),
        scratch_shapes=[pltpu.VMEM((tm + 16, wd), F32)] + [pltpu.VMEM((wd // LANES, tm, LANES), F32)] * 4,
        compiler_params=_cparams(("parallel", "arbitrary")),
        name="outproj",
    )(x, oa, qkv, qkv, qkv, qkv, qkv, qkv, qkv, oc, od[0], od[1], od[2], lse[0], lse[1], lse[2],
      w, cw, gn, gate)


def _ffn_body(x_ref, xp_ref, xn_ref, g_ref, sc_ref, sh_ref, gate_ref, gy_ref, wup_ref, cw_ref, wd_ref, out_ref,
              hf_ref, u_ref, a_ref, *, tm, tn):
    i = pl.program_id(1)
    ff = wd_ref.shape[0]
    d = x_ref.shape[-1]
    nc = d // LANES
    sub = 8
    seg = tm // sub

    def mod(v):
        return _rms(v, g_ref[...]) * (1.0 + sc_ref[...]) + sh_ref[...]

    def to_tile_order(ref):
        return jnp.concatenate(
            [jnp.concatenate([ref[c, pl.ds(a, sub, stride=seg), :] for a in range(seg)], axis=0)
             for c in range(nc)], axis=1)

    def to_seq_order(ref):
        return jnp.concatenate(
            [jnp.concatenate([ref[c, pl.ds(b, seg, stride=sub), :] for b in range(sub)], axis=0)
             for c in range(nc)], axis=1)

    hm = mod(x_ref[...])
    for c in range(nc):
        hf_ref[c] = hm[:, c * LANES:(c + 1) * LANES]
    first = (i > 0).astype(F32)
    final = (i < pl.num_programs(1) - 1).astype(F32)
    edge = jnp.concatenate([mod(xp_ref[sub:, :]) * first, mod(xn_ref[:sub, :]) * final], axis=0)
    h = jnp.concatenate([to_tile_order(hf_ref), edge], axis=0).astype(BF16)

    row = lax.broadcasted_iota(jnp.int32, (sub, 1), 0)

    def conv(slot, half, c0):
        lanes = slice(half * tn, (half + 1) * tn)
        cw = cw_ref[:, c0:c0 + tn]
        mid = u_ref[slot, 0:tm, lanes]
        top = jnp.where(row == 0, pltpu.roll(u_ref[slot, tm:tm + sub, lanes], 1, axis=0),
                        pltpu.roll(u_ref[slot, tm - sub:tm, lanes], 1, axis=0))
        prev = jnp.concatenate([top, u_ref[slot, 0:tm - sub, lanes]], axis=0)
        bot = jnp.where(row == sub - 1, pltpu.roll(u_ref[slot, tm + sub:tm + 2 * sub, lanes], sub - 1, axis=0),
                        pltpu.roll(u_ref[slot, 0:sub, lanes], sub - 1, axis=0))
        nxt = jnp.concatenate([u_ref[slot, sub:tm, lanes], bot], axis=0)
        return prev * cw[0:1] + mid * cw[1:2] + nxt * cw[2:3]

    for j in range(ff // tn):
        slot = j % 2
        c0 = j * tn
        u_ref[slot, :, 0:tn] = jnp.dot(h, wup_ref[:, c0:c0 + tn], preferred_element_type=F32)
        u_ref[slot, :, tn:2 * tn] = jnp.dot(h, wup_ref[:, ff + c0:ff + c0 + tn], preferred_element_type=F32)
        a = jax.nn.gelu(conv(slot, 0, c0), approximate=True) * conv(slot, 1, ff + c0)
        a_ref[:, c0:c0 + tn] = a.astype(BF16)

    y = jnp.dot(a_ref[...], wd_ref[...], preferred_element_type=F32)
    for c in range(nc):
        hf_ref[c] = y[:, c * LANES:(c + 1) * LANES]
    out_ref[...] = x_ref[...] + gate_ref[...] * _rms(to_seq_order(hf_ref), gy_ref[...])


def _ffn(x, g, sc, sh, gate, gy, w_up, cw, w_down, layer, tm=512, tn=256):
    b, s, d = x.shape
    ff = w_down.shape[1]
    hb = tm // BF16_ROWS
    last = s // BF16_ROWS - 1

    def const(shape, **kw):
        return pl.BlockSpec(shape, lambda bi, i: (0,) * len(shape), **kw)

    def per_batch():
        return pl.BlockSpec((None, 1, d), lambda bi, i: (bi, 0, 0))

    resident = dict(pipeline_mode=pl.Buffered(1))
    return pl.pallas_call(
        functools.partial(_ffn_body, tm=tm, tn=tn),
        grid=(b, s // tm),
        in_specs=[pl.BlockSpec((None, tm, d), lambda bi, i: (bi, i, 0)),
                  pl.BlockSpec((None, BF16_ROWS, d), lambda bi, i: (bi, jnp.maximum(i * hb - 1, 0), 0)),
                  pl.BlockSpec((None, BF16_ROWS, d), lambda bi, i: (bi, jnp.minimum((i + 1) * hb, last), 0)),
                  const(g.shape), per_batch(), per_batch(), per_batch(), const(gy.shape),
                  _layer_spec(w_up, layer, **resident), _layer_spec(cw, layer),
                  _layer_spec(w_down, layer, **resident)],
        out_specs=pl.BlockSpec((None, tm, d), lambda bi, i: (bi, i, 0)),
        out_shape=jax.ShapeDtypeStruct((b, s, d), F32),
        scratch_shapes=[pltpu.VMEM((d // LANES, tm, LANES), F32),
                        pltpu.VMEM((2, tm + BF16_ROWS, 2 * tn), F32),
                        pltpu.VMEM((tm, ff), BF16)],
        compiler_params=_cparams(("parallel", "arbitrary")),
        name="ffn",
    )(x, x, x, g, sc, sh, gate, gy, w_up, cw, w_down)


def _pack_w_in(w_in):
    nl, d, _ = w_in.shape
    hd = HEAD_DIM
    aq = w_in[..., 0:256]
    aq = jnp.concatenate([aq[..., 0:hd], aq[..., 2 * hd:3 * hd], aq[..., hd:2 * hd], aq[..., 3 * hd:]], axis=-1)
    ckr = w_in[..., 1664:1696]
    ckr = jnp.concatenate([jnp.zeros((nl, d, C_NOPE_DIM), w_in.dtype), ckr,
                           jnp.zeros((nl, d, LANES - C_NOPE_DIM - C_ROPE_DIM), w_in.dtype)], axis=-1)
    return jnp.concatenate([aq, w_in[..., 256:1664], ckr, w_in[..., 1696:]], axis=-1).astype(BF16)


def _pack_w_uq(w):
    nl, r, _ = w.shape
    half = C_ROPE_DIM // 2
    w = w.reshape(nl, r, C_HEADS, C_NOPE_DIM + C_ROPE_DIM)
    nope, r1, r2 = w[..., :C_NOPE_DIM], w[..., C_NOPE_DIM:C_NOPE_DIM + half], w[..., C_NOPE_DIM + half:]
    z_tail = jnp.zeros((nl, r, C_HEADS, LANES - C_NOPE_DIM - C_ROPE_DIM), w.dtype)
    plain = jnp.concatenate([nope, r1, r2, z_tail], axis=-1).reshape(nl, r, C_HEADS * LANES)
    swapped = jnp.concatenate([jnp.zeros_like(nope), r2, r1, z_tail], axis=-1).reshape(nl, r, C_HEADS * LANES)
    return jnp.concatenate([plain, swapped], axis=-1).astype(BF16)


def _pack_w_ukv(w):
    nl, r, _ = w.shape
    w = w.reshape(nl, r, C_HEADS, C_NOPE_DIM + C_V_DIM)
    kn = jnp.concatenate([w[..., :C_NOPE_DIM], jnp.zeros((nl, r, C_HEADS, LANES - C_NOPE_DIM), w.dtype)], axis=-1)
    return jnp.concatenate([kn.reshape(nl, r, C_HEADS * LANES),
                            w[..., C_NOPE_DIM:].reshape(nl, r, C_HEADS * C_V_DIM)], axis=-1).astype(BF16)


def _pack_w_out(w_out):
    hd = HEAD_DIM
    return jnp.concatenate([w_out[:, 0:hd], w_out[:, 2 * hd:3 * hd], w_out[:, hd:2 * hd], w_out[:, 3 * hd:]],
                           axis=1).astype(BF16)


def _rope_tables(positions):
    half = C_ROPE_DIM // 2
    inv_freq = ROPE_THETA ** (-jnp.arange(half, dtype=jnp.float32) / half)
    ang = positions.astype(jnp.float32)[..., None] * inv_freq
    cos, sin = jnp.cos(ang), jnp.sin(ang)
    lead = positions.shape + (C_NOPE_DIM,)
    tail = positions.shape + (LANES - C_NOPE_DIM - C_ROPE_DIM,)
    cos_t = jnp.concatenate([jnp.ones(lead, F32), cos, cos, jnp.zeros(tail, F32)], axis=-1)
    sin_t = jnp.concatenate([jnp.zeros(lead, F32), -sin, sin, jnp.zeros(tail, F32)], axis=-1)
    return cos_t, sin_t


def _swap_matrix():
    half = C_ROPE_DIM // 2
    idx = jnp.arange(LANES)
    src = jnp.where((idx >= C_NOPE_DIM) & (idx < C_NOPE_DIM + half), idx + half,
                    jnp.where((idx >= C_NOPE_DIM + half) & (idx < C_NOPE_DIM + C_ROPE_DIM), idx - half, idx))
    return (idx[:, None] == src[None, :]).astype(BF16)


def kernel(x, c, positions, rel_bias, w_mod, b_mod, norm_g, w_in, a_sink, b_conv, c_norm_q, c_norm_kv,
           c_w_uq, c_w_ukv, w_out, w_up, ffn_conv, w_down):
    b, s, d = x.shape
    depth = w_in.shape[0]

    a_order = jnp.array(A_HEAD_ORDER)
    bias_a = _band_bias(rel_bias[:, :A_Q_HEADS][:, a_order], A_RADIUS, 1, A_SUB_Q)
    bias_d = [_band_bias(rel_bias[:, A_Q_HEADS + i * D_HEADS:A_Q_HEADS + (i + 1) * D_HEADS], (w // 2) // dil, dil,
                         D_SUB_Q)
              for i, (w, dil) in enumerate(D_PATTERNS)]
    cos_t, sin_t = _rope_tables(positions)
    swap = _swap_matrix()

    w_in_p = _pack_w_in(w_in)
    w_uq_p = _pack_w_uq(c_w_uq)
    w_ukv_p = _pack_w_ukv(c_w_ukv)
    w_out_p = _pack_w_out(w_out)
    w_up_b = w_up.astype(BF16)
    w_down_b = w_down.astype(BF16)

    mod = _modulation(c, w_mod, b_mod)

    for l in range(depth):
        sh1, sc1, g1, sh2, sc2, g2 = (mod[l, k].reshape(b, 1, d) for k in range(6))
        gains = norm_g[l]

        qkv, qkv_d4, qkv_d16 = _inproj(x, gains[0:1], sc1, sh1, w_in_p, l)

        sink = jnp.broadcast_to(jnp.repeat(a_sink[l][a_order], A_SUB_Q)[:, None], (A_Q_HEADS * A_SUB_Q, LANES))
        oa = _band_attention(qkv, bias_a, sink, dil=1, radius=A_RADIUS, tq=512, q_off=OFF_AQ, k_off=OFF_AK,
                             v_off=OFF_AV, kv_w=LANES, has_lse=False)[0].reshape(b, s, 2 * LANES)

        qc, kc, vc = _mla_prep(qkv, cos_t, sin_t, c_norm_q[l][None], c_norm_kv[l][None], w_uq_p, w_ukv_p,
                               swap, l)
        oc = _mla_attn(qc, kc, vc)

        od, lse = [], []
        for i, ((w, dil), src) in enumerate(zip(D_PATTERNS, (qkv, qkv_d4, qkv_d16))):
            off = OFF_D if dil == 1 else 0
            o_i, l_i = _band_attention(src, bias_d[i], None, dil=dil, radius=(w // 2) // dil, tq=512, q_off=off,
                                       k_off=off + 256, v_off=off + 512, kv_w=2 * LANES, has_lse=True,
                                       group=2 if dil == 16 else 1)
            od.append(o_i)
            lse.append(l_i)

        x = _outproj(x, oa, qkv, oc, od, lse, w_out_p, b_conv, gains[1:2], g1, l)
        x = _ffn(x, gains[2:3], sc2, sh2, g2, gains[3:4], w_up_b, ffn_conv, w_down_b, l)
    return x
```

```python
import functools
import math

import jax
import jax.numpy as jnp
from jax import lax
from jax.experimental import pallas as pl
from jax.experimental.pallas import tpu as pltpu

D_MODEL = 1024
HEAD_DIM = 64
A_Q_HEADS = 4
A_KV_HEADS = 2
A_RADIUS = 128
B_WIDTH = 256
C_HEADS = 4
C_Q_RANK = 256
C_KV_RANK = 128
C_NOPE_DIM = 64
C_ROPE_DIM = 32
C_V_DIM = 64
ROPE_THETA = 10000.0
D_HEADS = 4
D_PATTERNS = ((128, 1), (512, 4), (2048, 16))
REL_BUCKETS = 32
REL_MAX_DISTANCE = 1024
D_FF = 2816
EPS = 1e-6
NEG = -1e30

LANES = 128
BF16_ROWS = 16
A_SUB_Q = 128
D_SUB_Q = 128
VMEM_LIMIT = 48 * 1024 * 1024

OFF_AQ, OFF_AK, OFF_AV = 0, 256, 384
OFF_BB, OFF_BC, OFF_BH = 512, 768, 1024
OFF_D = 1280
A_HEAD_ORDER = (0, 2, 1, 3)
PATTERN_COLS = 3 * D_HEADS * HEAD_DIM
MAIN_COLS = OFF_D + PATTERN_COLS
OFF_C = MAIN_COLS
C_COLS = C_Q_RANK + C_KV_RANK + LANES
OFF_D4 = OFF_C + C_COLS
COL_CHUNK = 512

F32 = jnp.float32
BF16 = jnp.bfloat16


def _layer_spec(w, layer, **kw):
    return pl.BlockSpec((None,) + w.shape[1:], lambda *_: (layer,) + (0,) * (w.ndim - 1), **kw)


def _cparams(sem):
    return pltpu.CompilerParams(dimension_semantics=sem, vmem_limit_bytes=VMEM_LIMIT)


def _rms(v, g):
    return v * lax.rsqrt(jnp.mean(v * v, axis=-1, keepdims=True) + EPS) * g


def _mod_body(c_ref, w_ref, b_ref, o_ref):
    c = c_ref[...]
    act = c / (1.0 + jnp.exp(-c))
    o_ref[...] = jnp.dot(act, w_ref[...], preferred_element_type=F32,
                         precision=lax.Precision.HIGHEST) + b_ref[...]


def _modulation(c, w_mod, b_mod):
    nl, d, _ = w_mod.shape
    b = c.shape[0]
    return pl.pallas_call(
        _mod_body,
        grid=(nl, 6),
        in_specs=[pl.BlockSpec((b, d), lambda l, k: (0, 0)),
                  pl.BlockSpec((None, d, d), lambda l, k: (l, 0, k)),
                  pl.BlockSpec((None, 1, d), lambda l, k: (l, 0, k))],
        out_specs=pl.BlockSpec((None, None, b, d), lambda l, k: (l, k, 0, 0)),
        out_shape=jax.ShapeDtypeStruct((nl, 6, b, d), F32),
        compiler_params=_cparams(("parallel", "parallel")),
        name="modulation",
    )(c, w_mod, b_mod.reshape(nl, 1, 6 * d))


def _latent_qkv(cc, cs, sn, gq_ref, gkv_ref, wq_ref, wkv_ref, swap_ref, q_ref, k_ref, v_ref):
    scale = (C_NOPE_DIM + C_ROPE_DIM) ** -0.5
    nq = _rms(cc[:, :C_Q_RANK], gq_ref[...]).astype(BF16)
    qq = jnp.dot(nq, wq_ref[...], preferred_element_type=F32)
    nkv = _rms(cc[:, C_Q_RANK:C_Q_RANK + C_KV_RANK], gkv_ref[...]).astype(BF16)
    kvv = jnp.dot(nkv, wkv_ref[...], preferred_element_type=F32)
    kr = cc[:, C_Q_RANK + C_KV_RANK:].astype(BF16)
    kr_sw = jnp.dot(kr, swap_ref[...], preferred_element_type=F32)
    k_rope = kr.astype(F32) * cs + kr_sw * sn
    hw = C_HEADS * LANES
    for h in range(C_HEADS):
        a = qq[:, h * LANES:(h + 1) * LANES]
        a_sw = qq[:, hw + h * LANES:hw + (h + 1) * LANES]
        q_ref[:, h * LANES:(h + 1) * LANES] = ((a * cs + a_sw * sn) * scale).astype(BF16)
        k_ref[:, h * LANES:(h + 1) * LANES] = (kvv[:, h * LANES:(h + 1) * LANES] + k_rope).astype(BF16)
    v_ref[...] = kvv[:, hw:].astype(BF16)


def _inproj_body(x_ref, g_ref, sc_ref, sh_ref, w_ref, cos_ref, sin_ref, gq_ref, gkv_ref, wq_ref, wkv_ref, swap_ref,
                 om_ref, o4_ref, o16_ref, q_ref, k_ref, v_ref, hf_ref, *, tm):
    h = _rms(x_ref[...], g_ref[...]) * (1.0 + sc_ref[...]) + sh_ref[...]
    nc = h.shape[-1] // LANES
    for c in range(nc):
        hf_ref[c] = h[:, c * LANES:(c + 1) * LANES]
    hb = h.astype(BF16)
    for c0 in range(0, MAIN_COLS, COL_CHUNK):
        om_ref[:, c0:c0 + COL_CHUNK] = jnp.dot(hb, w_ref[:, c0:c0 + COL_CHUNK],
                                               preferred_element_type=F32).astype(BF16)
    cc = jnp.dot(hb, w_ref[:, OFF_C:OFF_C + C_COLS], preferred_element_type=F32)
    _latent_qkv(cc, cos_ref[...], sin_ref[...], gq_ref, gkv_ref, wq_ref, wkv_ref, swap_ref, q_ref, k_ref, v_ref)
    for o_ref, dil, c0 in ((o4_ref, 4, OFF_D4), (o16_ref, 16, OFF_D4 + PATTERN_COLS)):
        n = tm // dil
        hp = jnp.concatenate(
            [jnp.concatenate([hf_ref[c, pl.ds(r, n, stride=dil), :] for r in range(dil)], axis=0)
             for c in range(nc)], axis=1).astype(BF16)
        res = jnp.dot(hp, w_ref[:, c0:c0 + PATTERN_COLS], preferred_element_type=F32).astype(BF16)
        for r in range(dil):
            o_ref[r] = res[r * n:(r + 1) * n]


def _inproj(x, g, sc, sh, w, cos_t, sin_t, gq, gkv, wq, wkv, swap, layer, tm=1024):
    b, s, d = x.shape
    d4, d16 = D_PATTERNS[1][1], D_PATTERNS[2][1]
    hw = C_HEADS * LANES
    vw = C_HEADS * C_V_DIM

    def const(shape):
        return pl.BlockSpec(shape, lambda bi, i: (0,) * len(shape))

    def rows(width):
        return pl.BlockSpec((None, tm, width), lambda bi, i: (bi, i, 0))

    def per_batch():
        return pl.BlockSpec((None, 1, d), lambda bi, i: (bi, 0, 0))

    return pl.pallas_call(
        functools.partial(_inproj_body, tm=tm),
        grid=(b, s // tm),
        in_specs=[rows(d), const(g.shape), per_batch(), per_batch(),
                  _layer_spec(w, layer, pipeline_mode=pl.Buffered(1)),
                  rows(LANES), rows(LANES), const(gq.shape), const(gkv.shape),
                  _layer_spec(wq, layer), _layer_spec(wkv, layer), const(swap.shape)],
        out_specs=[rows(MAIN_COLS),
                   pl.BlockSpec((None, d4, tm // d4, PATTERN_COLS), lambda bi, i: (bi, 0, i, 0)),
                   pl.BlockSpec((None, d16, tm // d16, PATTERN_COLS), lambda bi, i: (bi, 0, i, 0)),
                   rows(hw), rows(hw), rows(vw)],
        out_shape=[jax.ShapeDtypeStruct((b, s, MAIN_COLS), BF16),
                   jax.ShapeDtypeStruct((b, d4, s // d4, PATTERN_COLS), BF16),
                   jax.ShapeDtypeStruct((b, d16, s // d16, PATTERN_COLS), BF16),
                   jax.ShapeDtypeStruct((b, s, hw), BF16), jax.ShapeDtypeStruct((b, s, hw), BF16),
                   jax.ShapeDtypeStruct((b, s, vw), BF16)],
        scratch_shapes=[pltpu.VMEM((d // LANES, tm, LANES), F32)],
        compiler_params=_cparams(("parallel", "parallel")),
        name="inproj",
    )(x, g, sc, sh, w, cos_t, sin_t, gq, gkv, wq, wkv, swap)


def _t5_bucket(rel):
    half = REL_BUCKETS // 2
    max_exact = half // 2
    n = jnp.abs(rel)
    n_f = jnp.maximum(n, 1).astype(jnp.float32)
    large = max_exact + (jnp.log(n_f / max_exact) / math.log(REL_MAX_DISTANCE / max_exact)
                         * (half - max_exact)).astype(jnp.int32)
    large = jnp.minimum(large, half - 1)
    return jnp.where(rel > 0, half, 0) + jnp.where(n < max_exact, n, large)


def _band_bias(table, radius, stride, sub_q):
    qi = jnp.arange(sub_q)[:, None]
    kj = jnp.arange(sub_q + 2 * radius)[None, :]
    off = kj - radius - qi
    onehot = (_t5_bucket(off * stride)[..., None] == jnp.arange(REL_BUCKETS)).astype(F32)
    bias = jnp.einsum('qkb,bh->hqk', onehot, table.astype(F32), precision=lax.Precision.HIGHEST)
    bias = jnp.where((jnp.abs(off) <= radius)[None], bias, NEG)
    return bias.reshape(-1, bias.shape[-1])


def _band_body(*refs, radius, tq, seq, kv_w, has_sink, has_lse):
    q_ref, kp_ref, kc_ref, kn_ref, vp_ref, vc_ref, vn_ref, bias_ref = refs[:8]
    rest = refs[8:]
    if has_sink:
        sink_ref, rest = rest[0], rest[1:]
    o_ref = rest[0]
    lse_ref = rest[1] if has_lse else None

    i = pl.program_id(2)
    nh = 4
    sub_q = bias_ref.shape[0] // nh
    nk = sub_q + 2 * radius
    per = kv_w // HEAD_DIM
    lane = lax.broadcasted_iota(jnp.int32, (1, kv_w), 1)
    hmask = [(lane >= HEAD_DIM * (h % per)) & (lane < HEAD_DIM * (h % per + 1)) for h in range(nh)]
    scale = HEAD_DIM ** -0.5

    def pick(rows):
        out = rows[(per - 1) * sub_q:per * sub_q]
        for h in range(per - 2, -1, -1):
            out = jnp.where(hmask[h], rows[h * sub_q:(h + 1) * sub_q], out)
        return out

    for rr in range(q_ref.shape[0]):
        kcat = jnp.concatenate([kp_ref[rr], kc_ref[rr], kn_ref[rr]], axis=0)
        vcat = jnp.concatenate([vp_ref[rr], vc_ref[rr], vn_ref[rr]], axis=0)
        for sb in range(tq // sub_q):
            r0 = sb * sub_q
            qs = q_ref[rr, r0:r0 + sub_q, :] * scale
            ks = kcat[r0:r0 + nk]
            vs = vcat[r0:r0 + nk]
            at_edge = r0 - radius < 0 or r0 + sub_q + radius > tq
            kpos = i * tq + (r0 - radius) + lax.broadcasted_iota(jnp.int32, (1, nk), 1)
            inb = (kpos >= 0) & (kpos < seq)
            parts = []
            for h in range(nh):
                src = qs[:, (h // per) * kv_w:(h // per + 1) * kv_w]
                parts.append(jnp.where(hmask[h], src, jnp.zeros_like(src)))
            qst = jnp.concatenate(parts, axis=0)
            sc = lax.dot_general(qst, ks, (((1,), (1,)), ((), ())), preferred_element_type=F32)
            sc = sc + bias_ref[...]
            if at_edge:
                sc = jnp.where(inb, sc, NEG)
            m = jnp.max(sc, axis=-1, keepdims=True)
            if has_sink:
                m = jnp.maximum(m, jnp.max(sink_ref[...], axis=-1, keepdims=True))
            p = jnp.exp(sc - m)
            den = jnp.sum(p, axis=-1, keepdims=True)
            if has_sink:
                den = den + jnp.max(jnp.exp(sink_ref[...] - m), axis=-1, keepdims=True)
            o = jnp.dot(p.astype(BF16), vs, preferred_element_type=F32) / den
            lse = jnp.broadcast_to(m + jnp.log(den), o.shape) if has_lse else None
            for g in range(nh // per):
                rows = slice(g * per * sub_q, (g + 1) * per * sub_q)
                cols = slice(g * kv_w, (g + 1) * kv_w)
                o_ref[rr, r0:r0 + sub_q, cols] = pick(o[rows]).astype(BF16)
                if has_lse:
                    lse_ref[rr, r0:r0 + sub_q, cols] = pick(lse[rows])


def _band_attention(qkv, bias, sink, *, dil, radius, tq, q_off, k_off, v_off, kv_w, has_lse, group=1):
    b = qkv.shape[0]
    seq = qkv.shape[-2]
    s = seq * dil
    tq = min(tq, seq)
    qw = 2 * LANES
    steps = tq // radius
    last = seq // radius - 1
    if dil == 1:
        qkv = qkv.reshape(b, 1, seq, qkv.shape[-1])

    def spec(rows, width, off, row_index):
        return pl.BlockSpec((None, group, rows, width), lambda bi, r, i: (bi, r, row_index(i), off // width))

    def cur(i):
        return i

    def prev(i):
        return jnp.maximum(i * steps - 1, 0)

    def nxt(i):
        return jnp.minimum((i + 1) * steps, last)

    in_specs = [spec(tq, qw, q_off, cur)]
    for off in (k_off, v_off):
        in_specs += [spec(radius, kv_w, off, prev), spec(tq, kv_w, off, cur), spec(radius, kv_w, off, nxt)]
    in_specs.append(pl.BlockSpec(bias.shape, lambda bi, r, i: (0, 0)))
    args = [qkv] * 7 + [bias]
    if sink is not None:
        in_specs.append(pl.BlockSpec(sink.shape, lambda bi, r, i: (0, 0)))
        args.append(sink)

    o_spec = pl.BlockSpec((None, group, tq, qw), lambda bi, r, i: (bi, r, i, 0))
    out_shape = [jax.ShapeDtypeStruct((b, dil, seq, qw), BF16)]
    out_specs = [o_spec]
    if has_lse:
        out_shape.append(jax.ShapeDtypeStruct((b, dil, seq, qw), F32))
        out_specs.append(o_spec)

    outs = pl.pallas_call(
        functools.partial(_band_body, radius=radius, tq=tq, seq=seq, kv_w=kv_w,
                          has_sink=sink is not None, has_lse=has_lse),
        grid=(b, dil // group, seq // tq),
        in_specs=in_specs,
        out_specs=out_specs,
        out_shape=out_shape,
        compiler_params=_cparams(("parallel", "parallel", "parallel")),
        name=f"band_attention_d{dil}_r{radius}",
    )(*args)
    return outs


def _mla_attn_body(q_ref, k_ref, v_ref, o_ref, *, rows):
    lo = lax.broadcasted_iota(jnp.int32, (1, LANES), 1) < C_V_DIM
    for r0 in range(0, q_ref.shape[0], rows):
        for g in range(C_HEADS // 2):
            acc = None
            for half in range(2):
                h = 2 * g + half
                q = q_ref[r0:r0 + rows, h * LANES:(h + 1) * LANES]
                k = k_ref[:, h * LANES:(h + 1) * LANES]
                sc = lax.dot_general(q, k, (((1,), (1,)), ((), ())), preferred_element_type=F32)
                m = jnp.max(sc, axis=-1, keepdims=True)
                p = jnp.exp(sc - m)
                den = jnp.sum(p, axis=-1, keepdims=True)
                o = jnp.dot(p.astype(BF16), v_ref[:, g * LANES:(g + 1) * LANES],
                            preferred_element_type=F32) / den
                acc = o if acc is None else jnp.where(lo, acc, o)
            o_ref[r0:r0 + rows, g * LANES:(g + 1) * LANES] = acc.astype(BF16)


def _mla_attn(q, k, v, tq=1024, rows=256):
    b, s, hw = q.shape
    vw = v.shape[-1]
    return pl.pallas_call(
        functools.partial(_mla_attn_body, rows=rows),
        grid=(b, s // tq),
        in_specs=[pl.BlockSpec((None, tq, hw), lambda bi, i: (bi, i, 0)),
                  pl.BlockSpec((None, s, hw), lambda bi, i: (bi, 0, 0)),
                  pl.BlockSpec((None, s, vw), lambda bi, i: (bi, 0, 0))],
        out_specs=pl.BlockSpec((None, tq, vw), lambda bi, i: (bi, i, 0)),
        out_shape=jax.ShapeDtypeStruct((b, s, vw), BF16),
        compiler_params=_cparams(("parallel", "arbitrary")),
        name="mla_attention",
    )(q, k, v)


def _outproj_body(x_ref, oa_ref, bb_ref, bc_ref, bh_ref, bcp_ref, bhp_ref, bcn_ref, bhn_ref, oc_ref,
                  o0_ref, o1_ref, o2_ref, l0_ref, l1_ref, l2_ref, w_ref, cw_ref, gn_ref, gate_ref,
                  out_ref, conv_ref, *perm_refs, tm):
    i = pl.program_id(1)

    def natural(ref, s_ref):
        dil, n, width = ref.shape
        for r in range(dil):
            blk = ref[r].astype(F32)
            for c in range(width // LANES):
                s_ref[c, pl.ds(r, n, stride=dil), :] = blk[:, c * LANES:(c + 1) * LANES]
        return jnp.concatenate([s_ref[c] for c in range(width // LANES)], axis=1)

    pad = 8
    first = (i > 0).astype(F32)
    final = (i < pl.num_programs(1) - 1).astype(F32)
    conv_ref[pad:pad + tm, :] = bc_ref[...].astype(F32) * bh_ref[...].astype(F32)
    conv_ref[pad - 1:pad, :] = (bcp_ref[BF16_ROWS - 1:, :].astype(F32)
                                * bhp_ref[BF16_ROWS - 1:, :].astype(F32)) * first
    conv_ref[pad + tm:pad + tm + 1, :] = (bcn_ref[:1, :].astype(F32) * bhn_ref[:1, :].astype(F32)) * final
    cw = cw_ref[...]
    conv = (conv_ref[pad - 1:pad - 1 + tm, :] * cw[0:1] + conv_ref[pad:pad + tm, :] * cw[1:2]
            + conv_ref[pad + 1:pad + 1 + tm, :] * cw[2:3])
    ob = (bb_ref[...].astype(F32) * conv).astype(BF16)
    l0 = l0_ref[0]
    l1, l2 = natural(l1_ref, perm_refs[0]), natural(l2_ref, perm_refs[1])
    o1, o2 = natural(o1_ref, perm_refs[2]), natural(o2_ref, perm_refs[3])
    mx = jnp.maximum(jnp.maximum(l0, l1), l2)
    e0, e1, e2 = jnp.exp(l0 - mx), jnp.exp(l1 - mx), jnp.exp(l2 - mx)
    od = ((o0_ref[0].astype(F32) * e0 + o1 * e1 + o2 * e2) / (e0 + e1 + e2)).astype(BF16)
    wd = 2 * LANES
    y = jnp.dot(oa_ref[...], w_ref[0:wd, :], preferred_element_type=F32)
    y = y + jnp.dot(ob, w_ref[wd:2 * wd, :], preferred_element_type=F32)
    y = y + jnp.dot(oc_ref[...], w_ref[2 * wd:3 * wd, :], preferred_element_type=F32)
    y = y + jnp.dot(od, w_ref[3 * wd:4 * wd, :], preferred_element_type=F32)
    out_ref[...] = x_ref[...] + gate_ref[...] * _rms(y, gn_ref[...])


def _outproj(x, oa, qkv, oc, od, lse, w, cw, gn, gate, layer, tm=512):
    b, s, d = x.shape
    wd = 2 * LANES
    hb = tm // BF16_ROWS
    last = s // BF16_ROWS - 1

    def rows(width, off=0):
        return pl.BlockSpec((None, tm, width), lambda bi, i: (bi, i, off // width))

    def halo_prev(off):
        return pl.BlockSpec((None, BF16_ROWS, wd), lambda bi, i: (bi, jnp.maximum(i * hb - 1, 0), off // wd))

    def halo_next(off):
        return pl.BlockSpec((None, BF16_ROWS, wd), lambda bi, i: (bi, jnp.minimum((i + 1) * hb, last), off // wd))

    def const(shape):
        return pl.BlockSpec(shape, lambda bi, i: (0,) * len(shape))

    def grouped(arr):
        dil = arr.shape[1]
        return pl.BlockSpec((None, dil, tm // dil, wd), lambda bi, i: (bi, 0, i, 0))

    return pl.pallas_call(
        functools.partial(_outproj_body, tm=tm),
        grid=(b, s // tm),
        in_specs=[rows(d), rows(wd), rows(wd, OFF_BB), rows(wd, OFF_BC), rows(wd, OFF_BH),
                  halo_prev(OFF_BC), halo_prev(OFF_BH), halo_next(OFF_BC), halo_next(OFF_BH),
                  rows(wd)] + [grouped(a) for a in od] + [grouped(a) for a in lse] + [
                  _layer_spec(w, layer), _layer_spec(cw, layer), const(gn.shape),
                  pl.BlockSpec((None, 1, d), lambda bi, i: (bi, 0, 0))],
        out_specs=rows(d),
        out_shape=jax.ShapeDtypeStruct((b, s, d), F32),
        scratch_shapes=[pltpu.VMEM((tm + 16, wd), F32)] + [pltpu.VMEM((wd // LANES, tm, LANES), F32)] * 4,
        compiler_params=_cparams(("parallel", "arbitrary")),
        name="outproj",
    )(x, oa, qkv, qkv, qkv, qkv, qkv, qkv, qkv, oc, od[0], od[1], od[2], lse[0], lse[1], lse[2],
      w, cw, gn, gate)


def _ffn_body(x_ref, xp_ref, xn_ref, g_ref, sc_ref, sh_ref, gate_ref, gy_ref, wup_ref, cw_ref, wd_ref, out_ref,
              hf_ref, u_ref, a_ref, *, tm, tn):
    i = pl.program_id(1)
    ff = wd_ref.shape[0]
    d = x_ref.shape[-1]
    nc = d // LANES
    sub = 8
    seg = tm // sub

    def mod(v):
        return _rms(v, g_ref[...]) * (1.0 + sc_ref[...]) + sh_ref[...]

    def to_tile_order(ref):
        return jnp.concatenate(
            [jnp.concatenate([ref[c, pl.ds(a, sub, stride=seg), :] for a in range(seg)], axis=0)
             for c in range(nc)], axis=1)

    def to_seq_order(ref):
        return jnp.concatenate(
            [jnp.concatenate([ref[c, pl.ds(b, seg, stride=sub), :] for b in range(sub)], axis=0)
             for c in range(nc)], axis=1)

    hm = mod(x_ref[...])
    for c in range(nc):
        hf_ref[c] = hm[:, c * LANES:(c + 1) * LANES]
    first = (i > 0).astype(F32)
    final = (i < pl.num_programs(1) - 1).astype(F32)
    edge = jnp.concatenate([mod(xp_ref[sub:, :]) * first, mod(xn_ref[:sub, :]) * final], axis=0)
    h = jnp.concatenate([to_tile_order(hf_ref), edge], axis=0).astype(BF16)

    row = lax.broadcasted_iota(jnp.int32, (sub, 1), 0)

    def conv(slot, half, c0):
        lanes = slice(half * tn, (half + 1) * tn)
        cw = cw_ref[:, c0:c0 + tn]
        mid = u_ref[slot, 0:tm, lanes]
        top = jnp.where(row == 0, pltpu.roll(u_ref[slot, tm:tm + sub, lanes], 1, axis=0),
                        pltpu.roll(u_ref[slot, tm - sub:tm, lanes], 1, axis=0))
        prev = jnp.concatenate([top, u_ref[slot, 0:tm - sub, lanes]], axis=0)
        bot = jnp.where(row == sub - 1, pltpu.roll(u_ref[slot, tm + sub:tm + 2 * sub, lanes], sub - 1, axis=0),
                        pltpu.roll(u_ref[slot, 0:sub, lanes], sub - 1, axis=0))
        nxt = jnp.concatenate([u_ref[slot, sub:tm, lanes], bot], axis=0)
        return prev * cw[0:1] + mid * cw[1:2] + nxt * cw[2:3]

    for j in range(ff // tn):
        slot = j % 2
        c0 = j * tn
        u_ref[slot, :, 0:tn] = jnp.dot(h, wup_ref[:, c0:c0 + tn], preferred_element_type=F32)
        u_ref[slot, :, tn:2 * tn] = jnp.dot(h, wup_ref[:, ff + c0:ff + c0 + tn], preferred_element_type=F32)
        a = jax.nn.gelu(conv(slot, 0, c0), approximate=True) * conv(slot, 1, ff + c0)
        a_ref[:, c0:c0 + tn] = a.astype(BF16)

    y = jnp.dot(a_ref[...], wd_ref[...], preferred_element_type=F32)
    for c in range(nc):
        hf_ref[c] = y[:, c * LANES:(c + 1) * LANES]
    out_ref[...] = x_ref[...] + gate_ref[...] * _rms(to_seq_order(hf_ref), gy_ref[...])


def _ffn(x, g, sc, sh, gate, gy, w_up, cw, w_down, layer, tm=512, tn=256):
    b, s, d = x.shape
    ff = w_down.shape[1]
    hb = tm // BF16_ROWS
    last = s // BF16_ROWS - 1

    def const(shape, **kw):
        return pl.BlockSpec(shape, lambda bi, i: (0,) * len(shape), **kw)

    def per_batch():
        return pl.BlockSpec((None, 1, d), lambda bi, i: (bi, 0, 0))

    resident = dict(pipeline_mode=pl.Buffered(1))
    return pl.pallas_call(
        functools.partial(_ffn_body, tm=tm, tn=tn),
        grid=(b, s // tm),
        in_specs=[pl.BlockSpec((None, tm, d), lambda bi, i: (bi, i, 0)),
                  pl.BlockSpec((None, BF16_ROWS, d), lambda bi, i: (bi, jnp.maximum(i * hb - 1, 0), 0)),
                  pl.BlockSpec((None, BF16_ROWS, d), lambda bi, i: (bi, jnp.minimum((i + 1) * hb, last), 0)),
                  const(g.shape), per_batch(), per_batch(), per_batch(), const(gy.shape),
                  _layer_spec(w_up, layer, **resident), _layer_spec(cw, layer),
                  _layer_spec(w_down, layer, **resident)],
        out_specs=pl.BlockSpec((None, tm, d), lambda bi, i: (bi, i, 0)),
        out_shape=jax.ShapeDtypeStruct((b, s, d), F32),
        scratch_shapes=[pltpu.VMEM((d // LANES, tm, LANES), F32),
                        pltpu.VMEM((2, tm + BF16_ROWS, 2 * tn), F32),
                        pltpu.VMEM((tm, ff), BF16)],
        compiler_params=_cparams(("parallel", "arbitrary")),
        name="ffn",
    )(x, x, x, g, sc, sh, gate, gy, w_up, cw, w_down)


def _pack_w_in(w_in):
    nl, d, _ = w_in.shape
    hd = HEAD_DIM
    aq = w_in[..., 0:256]
    aq = jnp.concatenate([aq[..., 0:hd], aq[..., 2 * hd:3 * hd], aq[..., hd:2 * hd], aq[..., 3 * hd:]], axis=-1)
    ckr = w_in[..., 1664:1696]
    ckr = jnp.concatenate([jnp.zeros((nl, d, C_NOPE_DIM), w_in.dtype), ckr,
                           jnp.zeros((nl, d, LANES - C_NOPE_DIM - C_ROPE_DIM), w_in.dtype)], axis=-1)
    d0 = 1696
    return jnp.concatenate([aq, w_in[..., 256:1280], w_in[..., d0:d0 + PATTERN_COLS], w_in[..., 1280:1664], ckr,
                            w_in[..., d0 + PATTERN_COLS:]], axis=-1).astype(BF16)


def _pack_w_uq(w):
    nl, r, _ = w.shape
    half = C_ROPE_DIM // 2
    w = w.reshape(nl, r, C_HEADS, C_NOPE_DIM + C_ROPE_DIM)
    nope, r1, r2 = w[..., :C_NOPE_DIM], w[..., C_NOPE_DIM:C_NOPE_DIM + half], w[..., C_NOPE_DIM + half:]
    z_tail = jnp.zeros((nl, r, C_HEADS, LANES - C_NOPE_DIM - C_ROPE_DIM), w.dtype)
    plain = jnp.concatenate([nope, r1, r2, z_tail], axis=-1).reshape(nl, r, C_HEADS * LANES)
    swapped = jnp.concatenate([jnp.zeros_like(nope), r2, r1, z_tail], axis=-1).reshape(nl, r, C_HEADS * LANES)
    return jnp.concatenate([plain, swapped], axis=-1).astype(BF16)


def _pack_w_ukv(w):
    nl, r, _ = w.shape
    w = w.reshape(nl, r, C_HEADS, C_NOPE_DIM + C_V_DIM)
    kn = jnp.concatenate([w[..., :C_NOPE_DIM], jnp.zeros((nl, r, C_HEADS, LANES - C_NOPE_DIM), w.dtype)], axis=-1)
    return jnp.concatenate([kn.reshape(nl, r, C_HEADS * LANES),
                            w[..., C_NOPE_DIM:].reshape(nl, r, C_HEADS * C_V_DIM)], axis=-1).astype(BF16)


def _pack_w_out(w_out):
    hd = HEAD_DIM
    return jnp.concatenate([w_out[:, 0:hd], w_out[:, 2 * hd:3 * hd], w_out[:, hd:2 * hd], w_out[:, 3 * hd:]],
                           axis=1).astype(BF16)


def _rope_tables(positions):
    half = C_ROPE_DIM // 2
    inv_freq = ROPE_THETA ** (-jnp.arange(half, dtype=jnp.float32) / half)
    ang = positions.astype(jnp.float32)[..., None] * inv_freq
    cos, sin = jnp.cos(ang), jnp.sin(ang)
    lead = positions.shape + (C_NOPE_DIM,)
    tail = positions.shape + (LANES - C_NOPE_DIM - C_ROPE_DIM,)
    cos_t = jnp.concatenate([jnp.ones(lead, F32), cos, cos, jnp.zeros(tail, F32)], axis=-1)
    sin_t = jnp.concatenate([jnp.zeros(lead, F32), -sin, sin, jnp.zeros(tail, F32)], axis=-1)
    return cos_t, sin_t


def _swap_matrix():
    half = C_ROPE_DIM // 2
    idx = jnp.arange(LANES)
    src = jnp.where((idx >= C_NOPE_DIM) & (idx < C_NOPE_DIM + half), idx + half,
                    jnp.where((idx >= C_NOPE_DIM + half) & (idx < C_NOPE_DIM + C_ROPE_DIM), idx - half, idx))
    return (idx[:, None] == src[None, :]).astype(BF16)


def kernel(x, c, positions, rel_bias, w_mod, b_mod, norm_g, w_in, a_sink, b_conv, c_norm_q, c_norm_kv,
           c_w_uq, c_w_ukv, w_out, w_up, ffn_conv, w_down):
    b, s, d = x.shape
    depth = w_in.shape[0]

    a_order = jnp.array(A_HEAD_ORDER)
    bias_a = _band_bias(rel_bias[:, :A_Q_HEADS][:, a_order], A_RADIUS, 1, A_SUB_Q)
    bias_d = [_band_bias(rel_bias[:, A_Q_HEADS + i * D_HEADS:A_Q_HEADS + (i + 1) * D_HEADS], (w // 2) // dil, dil,
                         D_SUB_Q)
              for i, (w, dil) in enumerate(D_PATTERNS)]
    cos_t, sin_t = _rope_tables(positions)
    swap = _swap_matrix()

    w_in_p = _pack_w_in(w_in)
    w_uq_p = _pack_w_uq(c_w_uq)
    w_ukv_p = _pack_w_ukv(c_w_ukv)
    w_out_p = _pack_w_out(w_out)
    w_up_b = w_up.astype(BF16)
    w_down_b = w_down.astype(BF16)

    mod = _modulation(c, w_mod, b_mod)

    for l in range(depth):
        sh1, sc1, g1, sh2, sc2, g2 = (mod[l, k].reshape(b, 1, d) for k in range(6))
        gains = norm_g[l]

        qkv, qkv_d4, qkv_d16, qc, kc, vc = _inproj(x, gains[0:1], sc1, sh1, w_in_p, cos_t, sin_t, c_norm_q[l][None],
                                                   c_norm_kv[l][None], w_uq_p, w_ukv_p, swap, l)

        sink = jnp.broadcast_to(jnp.repeat(a_sink[l][a_order], A_SUB_Q)[:, None], (A_Q_HEADS * A_SUB_Q, LANES))
        oa = _band_attention(qkv, bias_a, sink, dil=1, radius=A_RADIUS, tq=512, q_off=OFF_AQ, k_off=OFF_AK,
                             v_off=OFF_AV, kv_w=LANES, has_lse=False)[0].reshape(b, s, 2 * LANES)

        oc = _mla_attn(qc, kc, vc)

        od, lse = [], []
        for i, ((w, dil), src) in enumerate(zip(D_PATTERNS, (qkv, qkv_d4, qkv_d16))):
            off = OFF_D if dil == 1 else 0
            o_i, l_i = _band_attention(src, bias_d[i], None, dil=dil, radius=(w // 2) // dil, tq=512, q_off=off,
                                       k_off=off + 256, v_off=off + 512, kv_w=2 * LANES, has_lse=True,
                                       group=2 if dil == 16 else 1)
            od.append(o_i)
            lse.append(l_i)

        x = _outproj(x, oa, qkv, oc, od, lse, w_out_p, b_conv, gains[1:2], g1, l)
        x = _ffn(x, gains[2:3], sc2, sh2, g2, gains[3:4], w_up_b, ffn_conv, w_down_b, l)
    return x
```

```python
import functools
import math

import jax
import jax.numpy as jnp
from jax import lax
from jax.experimental import pallas as pl
from jax.experimental.pallas import tpu as pltpu

D_MODEL = 1024
HEAD_DIM = 64
A_Q_HEADS = 4
A_KV_HEADS = 2
A_RADIUS = 128
B_WIDTH = 256
C_HEADS = 4
C_Q_RANK = 256
C_KV_RANK = 128
C_NOPE_DIM = 64
C_ROPE_DIM = 32
C_V_DIM = 64
ROPE_THETA = 10000.0
D_HEADS = 4
D_PATTERNS = ((128, 1), (512, 4), (2048, 16))
REL_BUCKETS = 32
REL_MAX_DISTANCE = 1024
D_FF = 2816
EPS = 1e-6
NEG = -1e30

LANES = 128
BF16_ROWS = 16
A_SUB_Q = 128
D_SUB_Q = 128
VMEM_LIMIT = 48 * 1024 * 1024

OFF_AQ, OFF_AK, OFF_AV = 0, 256, 384
OFF_BB, OFF_BC, OFF_BH = 512, 768, 1024
OFF_D = 1280
A_HEAD_ORDER = (0, 2, 1, 3)
PATTERN_COLS = 3 * D_HEADS * HEAD_DIM
MAIN_COLS = OFF_D + PATTERN_COLS
OFF_C = MAIN_COLS
C_COLS = C_Q_RANK + C_KV_RANK + LANES
OFF_D4 = OFF_C + C_COLS
COL_CHUNK = 512

F32 = jnp.float32
BF16 = jnp.bfloat16


def _layer_spec(w, layer, **kw):
    return pl.BlockSpec((None,) + w.shape[1:], lambda *_: (layer,) + (0,) * (w.ndim - 1), **kw)


def _cparams(sem):
    return pltpu.CompilerParams(dimension_semantics=sem, vmem_limit_bytes=VMEM_LIMIT)


def _rms(v, g):
    return v * lax.rsqrt(jnp.mean(v * v, axis=-1, keepdims=True) + EPS) * g


def _mod_body(c_ref, w_ref, b_ref, o_ref):
    c = c_ref[...]
    act = c / (1.0 + jnp.exp(-c))
    o_ref[...] = jnp.dot(act, w_ref[...], preferred_element_type=F32,
                         precision=lax.Precision.HIGHEST) + b_ref[...]


def _modulation(c, w_mod, b_mod):
    nl, d, _ = w_mod.shape
    b = c.shape[0]
    return pl.pallas_call(
        _mod_body,
        grid=(nl, 6),
        in_specs=[pl.BlockSpec((b, d), lambda l, k: (0, 0)),
                  pl.BlockSpec((None, d, d), lambda l, k: (l, 0, k)),
                  pl.BlockSpec((None, 1, d), lambda l, k: (l, 0, k))],
        out_specs=pl.BlockSpec((None, None, b, d), lambda l, k: (l, k, 0, 0)),
        out_shape=jax.ShapeDtypeStruct((nl, 6, b, d), F32),
        compiler_params=_cparams(("parallel", "parallel")),
        name="modulation",
    )(c, w_mod, b_mod.reshape(nl, 1, 6 * d))


def _latent_qkv(cc, cs, sn, gq_ref, gkv_ref, wq_ref, wkv_ref, swap_ref, q_ref, k_ref, v_ref):
    scale = (C_NOPE_DIM + C_ROPE_DIM) ** -0.5
    nq = _rms(cc[:, :C_Q_RANK], gq_ref[...]).astype(BF16)
    qq = jnp.dot(nq, wq_ref[...], preferred_element_type=F32)
    nkv = _rms(cc[:, C_Q_RANK:C_Q_RANK + C_KV_RANK], gkv_ref[...]).astype(BF16)
    kvv = jnp.dot(nkv, wkv_ref[...], preferred_element_type=F32)
    kr = cc[:, C_Q_RANK + C_KV_RANK:].astype(BF16)
    kr_sw = jnp.dot(kr, swap_ref[...], preferred_element_type=F32)
    k_rope = kr.astype(F32) * cs + kr_sw * sn
    hw = C_HEADS * LANES
    for h in range(C_HEADS):
        a = qq[:, h * LANES:(h + 1) * LANES]
        a_sw = qq[:, hw + h * LANES:hw + (h + 1) * LANES]
        q_ref[:, h * LANES:(h + 1) * LANES] = ((a * cs + a_sw * sn) * scale).astype(BF16)
        k_ref[:, h * LANES:(h + 1) * LANES] = (kvv[:, h * LANES:(h + 1) * LANES] + k_rope).astype(BF16)
    v_ref[...] = kvv[:, hw:].astype(BF16)


def _inproj_body(x_ref, g_ref, sc_ref, sh_ref, w_ref, cos_ref, sin_ref, gq_ref, gkv_ref, wq_ref, wkv_ref, swap_ref,
                 om_ref, o4_ref, o16_ref, q_ref, k_ref, v_ref, hf_ref, *, tm):
    h = _rms(x_ref[...], g_ref[...]) * (1.0 + sc_ref[...]) + sh_ref[...]
    nc = h.shape[-1] // LANES
    for c in range(nc):
        hf_ref[c] = h[:, c * LANES:(c + 1) * LANES]
    hb = h.astype(BF16)
    for c0 in range(0, MAIN_COLS, COL_CHUNK):
        om_ref[:, c0:c0 + COL_CHUNK] = jnp.dot(hb, w_ref[:, c0:c0 + COL_CHUNK],
                                               preferred_element_type=F32).astype(BF16)
    cc = jnp.dot(hb, w_ref[:, OFF_C:OFF_C + C_COLS], preferred_element_type=F32)
    _latent_qkv(cc, cos_ref[...], sin_ref[...], gq_ref, gkv_ref, wq_ref, wkv_ref, swap_ref, q_ref, k_ref, v_ref)
    for o_ref, dil, c0 in ((o4_ref, 4, OFF_D4), (o16_ref, 16, OFF_D4 + PATTERN_COLS)):
        n = tm // dil
        hp = jnp.concatenate(
            [jnp.concatenate([hf_ref[c, pl.ds(r, n, stride=dil), :] for r in range(dil)], axis=0)
             for c in range(nc)], axis=1).astype(BF16)
        res = jnp.dot(hp, w_ref[:, c0:c0 + PATTERN_COLS], preferred_element_type=F32).astype(BF16)
        for r in range(dil):
            o_ref[r] = res[r * n:(r + 1) * n]


def _inproj(x, g, sc, sh, w, cos_t, sin_t, gq, gkv, wq, wkv, swap, layer, tm=1024):
    b, s, d = x.shape
    d4, d16 = D_PATTERNS[1][1], D_PATTERNS[2][1]
    hw = C_HEADS * LANES
    vw = C_HEADS * C_V_DIM

    def const(shape):
        return pl.BlockSpec(shape, lambda bi, i: (0,) * len(shape))

    def rows(width):
        return pl.BlockSpec((None, tm, width), lambda bi, i: (bi, i, 0))

    def per_batch():
        return pl.BlockSpec((None, 1, d), lambda bi, i: (bi, 0, 0))

    return pl.pallas_call(
        functools.partial(_inproj_body, tm=tm),
        grid=(b, s // tm),
        in_specs=[rows(d), const(g.shape), per_batch(), per_batch(),
                  _layer_spec(w, layer, pipeline_mode=pl.Buffered(1)),
                  rows(LANES), rows(LANES), const(gq.shape), const(gkv.shape),
                  _layer_spec(wq, layer), _layer_spec(wkv, layer), const(swap.shape)],
        out_specs=[rows(MAIN_COLS),
                   pl.BlockSpec((None, d4, tm // d4, PATTERN_COLS), lambda bi, i: (bi, 0, i, 0)),
                   pl.BlockSpec((None, d16, tm // d16, PATTERN_COLS), lambda bi, i: (bi, 0, i, 0)),
                   rows(hw), rows(hw), rows(vw)],
        out_shape=[jax.ShapeDtypeStruct((b, s, MAIN_COLS), BF16),
                   jax.ShapeDtypeStruct((b, d4, s // d4, PATTERN_COLS), BF16),
                   jax.ShapeDtypeStruct((b, d16, s // d16, PATTERN_COLS), BF16),
                   jax.ShapeDtypeStruct((b, s, hw), BF16), jax.ShapeDtypeStruct((b, s, hw), BF16),
                   jax.ShapeDtypeStruct((b, s, vw), BF16)],
        scratch_shapes=[pltpu.VMEM((d // LANES, tm, LANES), F32)],
        compiler_params=_cparams(("parallel", "parallel")),
        name="inproj",
    )(x, g, sc, sh, w, cos_t, sin_t, gq, gkv, wq, wkv, swap)


def _t5_bucket(rel):
    half = REL_BUCKETS // 2
    max_exact = half // 2
    n = jnp.abs(rel)
    n_f = jnp.maximum(n, 1).astype(jnp.float32)
    large = max_exact + (jnp.log(n_f / max_exact) / math.log(REL_MAX_DISTANCE / max_exact)
                         * (half - max_exact)).astype(jnp.int32)
    large = jnp.minimum(large, half - 1)
    return jnp.where(rel > 0, half, 0) + jnp.where(n < max_exact, n, large)


def _band_bias(table, radius, stride, sub_q):
    qi = jnp.arange(sub_q)[:, None]
    kj = jnp.arange(sub_q + 2 * radius)[None, :]
    off = kj - radius - qi
    onehot = (_t5_bucket(off * stride)[..., None] == jnp.arange(REL_BUCKETS)).astype(F32)
    bias = jnp.einsum('qkb,bh->hqk', onehot, table.astype(F32), precision=lax.Precision.HIGHEST)
    bias = jnp.where((jnp.abs(off) <= radius)[None], bias, NEG)
    return bias.reshape(-1, bias.shape[-1])


def _band_body(*refs, radius, tq, seq, kv_w, has_sink, has_lse):
    q_ref, kp_ref, kc_ref, kn_ref, vp_ref, vc_ref, vn_ref, bias_ref = refs[:8]
    rest = refs[8:]
    if has_sink:
        sink_ref, rest = rest[0], rest[1:]
    o_ref = rest[0]
    lse_ref = rest[1] if has_lse else None

    i = pl.program_id(2)
    nh = 4
    sub_q = bias_ref.shape[0] // nh
    nk = sub_q + 2 * radius
    per = kv_w // HEAD_DIM
    lane = lax.broadcasted_iota(jnp.int32, (1, kv_w), 1)
    hmask = [(lane >= HEAD_DIM * (h % per)) & (lane < HEAD_DIM * (h % per + 1)) for h in range(nh)]
    scale = HEAD_DIM ** -0.5

    def pick(rows):
        out = rows[(per - 1) * sub_q:per * sub_q]
        for h in range(per - 2, -1, -1):
            out = jnp.where(hmask[h], rows[h * sub_q:(h + 1) * sub_q], out)
        return out

    for rr in range(q_ref.shape[0]):
        kcat = jnp.concatenate([kp_ref[rr], kc_ref[rr], kn_ref[rr]], axis=0)
        vcat = jnp.concatenate([vp_ref[rr], vc_ref[rr], vn_ref[rr]], axis=0)
        for sb in range(tq // sub_q):
            r0 = sb * sub_q
            qs = q_ref[rr, r0:r0 + sub_q, :] * scale
            ks = kcat[r0:r0 + nk]
            vs = vcat[r0:r0 + nk]
            at_edge = r0 - radius < 0 or r0 + sub_q + radius > tq
            kpos = i * tq + (r0 - radius) + lax.broadcasted_iota(jnp.int32, (1, nk), 1)
            inb = (kpos >= 0) & (kpos < seq)
            parts = []
            for h in range(nh):
                src = qs[:, (h // per) * kv_w:(h // per + 1) * kv_w]
                parts.append(jnp.where(hmask[h], src, jnp.zeros_like(src)))
            qst = jnp.concatenate(parts, axis=0)
            sc = lax.dot_general(qst, ks, (((1,), (1,)), ((), ())), preferred_element_type=F32)
            sc = sc + bias_ref[...]
            if at_edge:
                sc = jnp.where(inb, sc, NEG)
            m = jnp.max(sc, axis=-1, keepdims=True)
            if has_sink:
                m = jnp.maximum(m, jnp.max(sink_ref[...], axis=-1, keepdims=True))
            p = jnp.exp(sc - m)
            den = jnp.sum(p, axis=-1, keepdims=True)
            if has_sink:
                den = den + jnp.max(jnp.exp(sink_ref[...] - m), axis=-1, keepdims=True)
            o = jnp.dot(p.astype(BF16), vs, preferred_element_type=F32) / den
            lse = jnp.broadcast_to(m + jnp.log(den), o.shape) if has_lse else None
            for g in range(nh // per):
                rows = slice(g * per * sub_q, (g + 1) * per * sub_q)
                cols = slice(g * kv_w, (g + 1) * kv_w)
                o_ref[rr, r0:r0 + sub_q, cols] = pick(o[rows]).astype(BF16)
                if has_lse:
                    lse_ref[rr, r0:r0 + sub_q, cols] = pick(lse[rows])


def _band_attention(qkv, bias, sink, *, dil, radius, tq, q_off, k_off, v_off, kv_w, has_lse, group=1):
    b = qkv.shape[0]
    seq = qkv.shape[-2]
    s = seq * dil
    tq = min(tq, seq)
    qw = 2 * LANES
    steps = tq // radius
    last = seq // radius - 1
    if dil == 1:
        qkv = qkv.reshape(b, 1, seq, qkv.shape[-1])

    def spec(rows, width, off, row_index):
        return pl.BlockSpec((None, group, rows, width), lambda bi, r, i: (bi, r, row_index(i), off // width))

    def cur(i):
        return i

    def prev(i):
        return jnp.maximum(i * steps - 1, 0)

    def nxt(i):
        return jnp.minimum((i + 1) * steps, last)

    in_specs = [spec(tq, qw, q_off, cur)]
    for off in (k_off, v_off):
        in_specs += [spec(radius, kv_w, off, prev), spec(tq, kv_w, off, cur), spec(radius, kv_w, off, nxt)]
    in_specs.append(pl.BlockSpec(bias.shape, lambda bi, r, i: (0, 0)))
    args = [qkv] * 7 + [bias]
    if sink is not None:
        in_specs.append(pl.BlockSpec(sink.shape, lambda bi, r, i: (0, 0)))
        args.append(sink)

    o_spec = pl.BlockSpec((None, group, tq, qw), lambda bi, r, i: (bi, r, i, 0))
    out_shape = [jax.ShapeDtypeStruct((b, dil, seq, qw), BF16)]
    out_specs = [o_spec]
    if has_lse:
        out_shape.append(jax.ShapeDtypeStruct((b, dil, seq, qw), F32))
        out_specs.append(o_spec)

    outs = pl.pallas_call(
        functools.partial(_band_body, radius=radius, tq=tq, seq=seq, kv_w=kv_w,
                          has_sink=sink is not None, has_lse=has_lse),
        grid=(b, dil // group, seq // tq),
        in_specs=in_specs,
        out_specs=out_specs,
        out_shape=out_shape,
        compiler_params=_cparams(("parallel", "parallel", "parallel")),
        name=f"band_attention_d{dil}_r{radius}",
    )(*args)
    return outs


def _mla_attn_body(q_ref, k_ref, v_ref, o_ref, *, rows):
    lo = lax.broadcasted_iota(jnp.int32, (1, LANES), 1) < C_V_DIM
    for r0 in range(0, q_ref.shape[0], rows):
        for g in range(C_HEADS // 2):
            acc = None
            for half in range(2):
                h = 2 * g + half
                q = q_ref[r0:r0 + rows, h * LANES:(h + 1) * LANES]
                k = k_ref[:, h * LANES:(h + 1) * LANES]
                sc = lax.dot_general(q, k, (((1,), (1,)), ((), ())), preferred_element_type=F32)
                m = jnp.max(sc, axis=-1, keepdims=True)
                p = jnp.exp(sc - m)
                den = jnp.sum(p, axis=-1, keepdims=True)
                o = jnp.dot(p.astype(BF16), v_ref[:, g * LANES:(g + 1) * LANES],
                            preferred_element_type=F32) / den
                acc = o if acc is None else jnp.where(lo, acc, o)
            o_ref[r0:r0 + rows, g * LANES:(g + 1) * LANES] = acc.astype(BF16)


def _mla_attn(q, k, v, tq=1024, rows=256):
    b, s, hw = q.shape
    vw = v.shape[-1]
    return pl.pallas_call(
        functools.partial(_mla_attn_body, rows=rows),
        grid=(b, s // tq),
        in_specs=[pl.BlockSpec((None, tq, hw), lambda bi, i: (bi, i, 0)),
                  pl.BlockSpec((None, s, hw), lambda bi, i: (bi, 0, 0)),
                  pl.BlockSpec((None, s, vw), lambda bi, i: (bi, 0, 0))],
        out_specs=pl.BlockSpec((None, tq, vw), lambda bi, i: (bi, i, 0)),
        out_shape=jax.ShapeDtypeStruct((b, s, vw), BF16),
        compiler_params=_cparams(("parallel", "arbitrary")),
        name="mla_attention",
    )(q, k, v)


def _outproj_body(x_ref, oa_ref, bb_ref, bc_ref, bh_ref, bcp_ref, bhp_ref, bcn_ref, bhn_ref, oc_ref,
                  o0_ref, o1_ref, o2_ref, l0_ref, l1_ref, l2_ref, w_ref, cw_ref, gn_ref, gate_ref,
                  out_ref, conv_ref, *perm_refs, tm):
    i = pl.program_id(1)

    def natural(ref, s_ref):
        dil, n, width = ref.shape
        for r in range(dil):
            blk = ref[r].astype(F32)
            for c in range(width // LANES):
                s_ref[c, pl.ds(r, n, stride=dil), :] = blk[:, c * LANES:(c + 1) * LANES]
        return jnp.concatenate([s_ref[c] for c in range(width // LANES)], axis=1)

    pad = 8
    first = (i > 0).astype(F32)
    final = (i < pl.num_programs(1) - 1).astype(F32)
    conv_ref[pad:pad + tm, :] = bc_ref[...].astype(F32) * bh_ref[...].astype(F32)
    conv_ref[pad - 1:pad, :] = (bcp_ref[BF16_ROWS - 1:, :].astype(F32)
                                * bhp_ref[BF16_ROWS - 1:, :].astype(F32)) * first
    conv_ref[pad + tm:pad + tm + 1, :] = (bcn_ref[:1, :].astype(F32) * bhn_ref[:1, :].astype(F32)) * final
    cw = cw_ref[...]
    conv = (conv_ref[pad - 1:pad - 1 + tm, :] * cw[0:1] + conv_ref[pad:pad + tm, :] * cw[1:2]
            + conv_ref[pad + 1:pad + 1 + tm, :] * cw[2:3])
    ob = (bb_ref[...].astype(F32) * conv).astype(BF16)
    l0 = l0_ref[0]
    l1, l2 = natural(l1_ref, perm_refs[0]), natural(l2_ref, perm_refs[1])
    o1, o2 = natural(o1_ref, perm_refs[2]), natural(o2_ref, perm_refs[3])
    mx = jnp.maximum(jnp.maximum(l0, l1), l2)
    e0, e1, e2 = jnp.exp(l0 - mx), jnp.exp(l1 - mx), jnp.exp(l2 - mx)
    od = ((o0_ref[0].astype(F32) * e0 + o1 * e1 + o2 * e2) / (e0 + e1 + e2)).astype(BF16)
    wd = 2 * LANES
    y = jnp.dot(oa_ref[...], w_ref[0:wd, :], preferred_element_type=F32)
    y = y + jnp.dot(ob, w_ref[wd:2 * wd, :], preferred_element_type=F32)
    y = y + jnp.dot(oc_ref[...], w_ref[2 * wd:3 * wd, :], preferred_element_type=F32)
    y = y + jnp.dot(od, w_ref[3 * wd:4 * wd, :], preferred_element_type=F32)
    out_ref[...] = x_ref[...] + gate_ref[...] * _rms(y, gn_ref[...])


def _outproj(x, oa, qkv, oc, od, lse, w, cw, gn, gate, layer, tm=1024):
    b, s, d = x.shape
    wd = 2 * LANES
    hb = tm // BF16_ROWS
    last = s // BF16_ROWS - 1

    def rows(width, off=0):
        return pl.BlockSpec((None, tm, width), lambda bi, i: (bi, i, off // width))

    def halo_prev(off):
        return pl.BlockSpec((None, BF16_ROWS, wd), lambda bi, i: (bi, jnp.maximum(i * hb - 1, 0), off // wd))

    def halo_next(off):
        return pl.BlockSpec((None, BF16_ROWS, wd), lambda bi, i: (bi, jnp.minimum((i + 1) * hb, last), off // wd))

    def const(shape):
        return pl.BlockSpec(shape, lambda bi, i: (0,) * len(shape))

    def grouped(arr):
        dil = arr.shape[1]
        return pl.BlockSpec((None, dil, tm // dil, wd), lambda bi, i: (bi, 0, i, 0))

    return pl.pallas_call(
        functools.partial(_outproj_body, tm=tm),
        grid=(b, s // tm),
        in_specs=[rows(d), rows(wd), rows(wd, OFF_BB), rows(wd, OFF_BC), rows(wd, OFF_BH),
                  halo_prev(OFF_BC), halo_prev(OFF_BH), halo_next(OFF_BC), halo_next(OFF_BH),
                  rows(wd)] + [grouped(a) for a in od] + [grouped(a) for a in lse] + [
                  _layer_spec(w, layer), _layer_spec(cw, layer), const(gn.shape),
                  pl.BlockSpec((None, 1, d), lambda bi, i: (bi, 0, 0))],
        out_specs=rows(d),
        out_shape=jax.ShapeDtypeStruct((b, s, d), F32),
        scratch_shapes=[pltpu.VMEM((tm + 16, wd), F32)] + [pltpu.VMEM((wd // LANES, tm, LANES), F32)] * 4,
        compiler_params=_cparams(("parallel", "arbitrary")),
        name="outproj",
    )(x, oa, qkv, qkv, qkv, qkv, qkv, qkv, qkv, oc, od[0], od[1], od[2], lse[0], lse[1], lse[2],
      w, cw, gn, gate)


def _ffn_body(x_ref, xp_ref, xn_ref, g_ref, sc_ref, sh_ref, gate_ref, gy_ref, wup_ref, cw_ref, wd_ref, out_ref,
              hf_ref, u_ref, a_ref, *, tm, tn):
    i = pl.program_id(1)
    ff = wd_ref.shape[0]
    d = x_ref.shape[-1]
    nc = d // LANES
    sub = 8
    seg = tm // sub

    def mod(v):
        return _rms(v, g_ref[...]) * (1.0 + sc_ref[...]) + sh_ref[...]

    def to_tile_order(ref):
        return jnp.concatenate(
            [jnp.concatenate([ref[c, pl.ds(a, sub, stride=seg), :] for a in range(seg)], axis=0)
             for c in range(nc)], axis=1)

    def to_seq_order(ref):
        return jnp.concatenate(
            [jnp.concatenate([ref[c, pl.ds(b, seg, stride=sub), :] for b in range(sub)], axis=0)
             for c in range(nc)], axis=1)

    hm = mod(x_ref[...])
    for c in range(nc):
        hf_ref[c] = hm[:, c * LANES:(c + 1) * LANES]
    first = (i > 0).astype(F32)
    final = (i < pl.num_programs(1) - 1).astype(F32)
    edge = jnp.concatenate([mod(xp_ref[sub:, :]) * first, mod(xn_ref[:sub, :]) * final], axis=0)
    h = jnp.concatenate([to_tile_order(hf_ref), edge], axis=0).astype(BF16)

    row = lax.broadcasted_iota(jnp.int32, (sub, 1), 0)

    def conv(slot, half, c0):
        lanes = slice(half * tn, (half + 1) * tn)
        cw = cw_ref[:, c0:c0 + tn]
        mid = u_ref[slot, 0:tm, lanes]
        top = jnp.where(row == 0, pltpu.roll(u_ref[slot, tm:tm + sub, lanes], 1, axis=0),
                        pltpu.roll(u_ref[slot, tm - sub:tm, lanes], 1, axis=0))
        prev = jnp.concatenate([top, u_ref[slot, 0:tm - sub, lanes]], axis=0)
        bot = jnp.where(row == sub - 1, pltpu.roll(u_ref[slot, tm + sub:tm + 2 * sub, lanes], sub - 1, axis=0),
                        pltpu.roll(u_ref[slot, 0:sub, lanes], sub - 1, axis=0))
        nxt = jnp.concatenate([u_ref[slot, sub:tm, lanes], bot], axis=0)
        return prev * cw[0:1] + mid * cw[1:2] + nxt * cw[2:3]

    for j in range(ff // tn):
        slot = j % 2
        c0 = j * tn
        u_ref[slot, :, 0:tn] = jnp.dot(h, wup_ref[:, c0:c0 + tn], preferred_element_type=F32)
        u_ref[slot, :, tn:2 * tn] = jnp.dot(h, wup_ref[:, ff + c0:ff + c0 + tn], preferred_element_type=F32)
        a = jax.nn.gelu(conv(slot, 0, c0), approximate=True) * conv(slot, 1, ff + c0)
        a_ref[:, c0:c0 + tn] = a.astype(BF16)

    y = jnp.dot(a_ref[...], wd_ref[...], preferred_element_type=F32)
    for c in range(nc):
        hf_ref[c] = y[:, c * LANES:(c + 1) * LANES]
    out_ref[...] = x_ref[...] + gate_ref[...] * _rms(to_seq_order(hf_ref), gy_ref[...])


def _ffn(x, g, sc, sh, gate, gy, w_up, cw, w_down, layer, tm=512, tn=256):
    b, s, d = x.shape
    ff = w_down.shape[1]
    hb = tm // BF16_ROWS
    last = s // BF16_ROWS - 1

    def const(shape, **kw):
        return pl.BlockSpec(shape, lambda bi, i: (0,) * len(shape), **kw)

    def per_batch():
        return pl.BlockSpec((None, 1, d), lambda bi, i: (bi, 0, 0))

    resident = dict(pipeline_mode=pl.Buffered(1))
    return pl.pallas_call(
        functools.partial(_ffn_body, tm=tm, tn=tn),
        grid=(b, s // tm),
        in_specs=[pl.BlockSpec((None, tm, d), lambda bi, i: (bi, i, 0)),
                  pl.BlockSpec((None, BF16_ROWS, d), lambda bi, i: (bi, jnp.maximum(i * hb - 1, 0), 0)),
                  pl.BlockSpec((None, BF16_ROWS, d), lambda bi, i: (bi, jnp.minimum((i + 1) * hb, last), 0)),
                  const(g.shape), per_batch(), per_batch(), per_batch(), const(gy.shape),
                  _layer_spec(w_up, layer, **resident), _layer_spec(cw, layer),
                  _layer_spec(w_down, layer, **resident)],
        out_specs=pl.BlockSpec((None, tm, d), lambda bi, i: (bi, i, 0)),
        out_shape=jax.ShapeDtypeStruct((b, s, d), F32),
        scratch_shapes=[pltpu.VMEM((d // LANES, tm, LANES), F32),
                        pltpu.VMEM((2, tm + BF16_ROWS, 2 * tn), F32),
                        pltpu.VMEM((tm, ff), BF16)],
        compiler_params=_cparams(("parallel", "arbitrary")),
        name="ffn",
    )(x, x, x, g, sc, sh, gate, gy, w_up, cw, w_down)


def _pack_w_in(w_in):
    nl, d, _ = w_in.shape
    hd = HEAD_DIM
    aq = w_in[..., 0:256]
    aq = jnp.concatenate([aq[..., 0:hd], aq[..., 2 * hd:3 * hd], aq[..., hd:2 * hd], aq[..., 3 * hd:]], axis=-1)
    ckr = w_in[..., 1664:1696]
    ckr = jnp.concatenate([jnp.zeros((nl, d, C_NOPE_DIM), w_in.dtype), ckr,
                           jnp.zeros((nl, d, LANES - C_NOPE_DIM - C_ROPE_DIM), w_in.dtype)], axis=-1)
    d0 = 1696
    return jnp.concatenate([aq, w_in[..., 256:1280], w_in[..., d0:d0 + PATTERN_COLS], w_in[..., 1280:1664], ckr,
                            w_in[..., d0 + PATTERN_COLS:]], axis=-1).astype(BF16)


def _pack_w_uq(w):
    nl, r, _ = w.shape
    half = C_ROPE_DIM // 2
    w = w.reshape(nl, r, C_HEADS, C_NOPE_DIM + C_ROPE_DIM)
    nope, r1, r2 = w[..., :C_NOPE_DIM], w[..., C_NOPE_DIM:C_NOPE_DIM + half], w[..., C_NOPE_DIM + half:]
    z_tail = jnp.zeros((nl, r, C_HEADS, LANES - C_NOPE_DIM - C_ROPE_DIM), w.dtype)
    plain = jnp.concatenate([nope, r1, r2, z_tail], axis=-1).reshape(nl, r, C_HEADS * LANES)
    swapped = jnp.concatenate([jnp.zeros_like(nope), r2, r1, z_tail], axis=-1).reshape(nl, r, C_HEADS * LANES)
    return jnp.concatenate([plain, swapped], axis=-1).astype(BF16)


def _pack_w_ukv(w):
    nl, r, _ = w.shape
    w = w.reshape(nl, r, C_HEADS, C_NOPE_DIM + C_V_DIM)
    kn = jnp.concatenate([w[..., :C_NOPE_DIM], jnp.zeros((nl, r, C_HEADS, LANES - C_NOPE_DIM), w.dtype)], axis=-1)
    return jnp.concatenate([kn.reshape(nl, r, C_HEADS * LANES),
                            w[..., C_NOPE_DIM:].reshape(nl, r, C_HEADS * C_V_DIM)], axis=-1).astype(BF16)


def _pack_w_out(w_out):
    hd = HEAD_DIM
    return jnp.concatenate([w_out[:, 0:hd], w_out[:, 2 * hd:3 * hd], w_out[:, hd:2 * hd], w_out[:, 3 * hd:]],
                           axis=1).astype(BF16)


def _rope_tables(positions):
    half = C_ROPE_DIM // 2
    inv_freq = ROPE_THETA ** (-jnp.arange(half, dtype=jnp.float32) / half)
    ang = positions.astype(jnp.float32)[..., None] * inv_freq
    cos, sin = jnp.cos(ang), jnp.sin(ang)
    lead = positions.shape + (C_NOPE_DIM,)
    tail = positions.shape + (LANES - C_NOPE_DIM - C_ROPE_DIM,)
    cos_t = jnp.concatenate([jnp.ones(lead, F32), cos, cos, jnp.zeros(tail, F32)], axis=-1)
    sin_t = jnp.concatenate([jnp.zeros(lead, F32), -sin, sin, jnp.zeros(tail, F32)], axis=-1)
    return cos_t, sin_t


def _swap_matrix():
    half = C_ROPE_DIM // 2
    idx = jnp.arange(LANES)
    src = jnp.where((idx >= C_NOPE_DIM) & (idx < C_NOPE_DIM + half), idx + half,
                    jnp.where((idx >= C_NOPE_DIM + half) & (idx < C_NOPE_DIM + C_ROPE_DIM), idx - half, idx))
    return (idx[:, None] == src[None, :]).astype(BF16)


def kernel(x, c, positions, rel_bias, w_mod, b_mod, norm_g, w_in, a_sink, b_conv, c_norm_q, c_norm_kv,
           c_w_uq, c_w_ukv, w_out, w_up, ffn_conv, w_down):
    b, s, d = x.shape
    depth = w_in.shape[0]

    a_order = jnp.array(A_HEAD_ORDER)
    bias_a = _band_bias(rel_bias[:, :A_Q_HEADS][:, a_order], A_RADIUS, 1, A_SUB_Q)
    bias_d = [_band_bias(rel_bias[:, A_Q_HEADS + i * D_HEADS:A_Q_HEADS + (i + 1) * D_HEADS], (w // 2) // dil, dil,
                         D_SUB_Q)
              for i, (w, dil) in enumerate(D_PATTERNS)]
    cos_t, sin_t = _rope_tables(positions)
    swap = _swap_matrix()

    w_in_p = _pack_w_in(w_in)
    w_uq_p = _pack_w_uq(c_w_uq)
    w_ukv_p = _pack_w_ukv(c_w_ukv)
    w_out_p = _pack_w_out(w_out)
    w_up_b = w_up.astype(BF16)
    w_down_b = w_down.astype(BF16)

    mod = _modulation(c, w_mod, b_mod)

    for l in range(depth):
        sh1, sc1, g1, sh2, sc2, g2 = (mod[l, k].reshape(b, 1, d) for k in range(6))
        gains = norm_g[l]

        qkv, qkv_d4, qkv_d16, qc, kc, vc = _inproj(x, gains[0:1], sc1, sh1, w_in_p, cos_t, sin_t, c_norm_q[l][None],
                                                   c_norm_kv[l][None], w_uq_p, w_ukv_p, swap, l)

        sink = jnp.broadcast_to(jnp.repeat(a_sink[l][a_order], A_SUB_Q)[:, None], (A_Q_HEADS * A_SUB_Q, LANES))
        oa = _band_attention(qkv, bias_a, sink, dil=1, radius=A_RADIUS, tq=1024, q_off=OFF_AQ, k_off=OFF_AK,
                             v_off=OFF_AV, kv_w=LANES, has_lse=False)[0].reshape(b, s, 2 * LANES)

        oc = _mla_attn(qc, kc, vc)

        od, lse = [], []
        for i, ((w, dil), src) in enumerate(zip(D_PATTERNS, (qkv, qkv_d4, qkv_d16))):
            off = OFF_D if dil == 1 else 0
            o_i, l_i = _band_attention(src, bias_d[i], None, dil=dil, radius=(w // 2) // dil, tq=1024, q_off=off,
                                       k_off=off + 256, v_off=off + 512, kv_w=2 * LANES, has_lse=True,
                                       group=4 if dil == 16 else 1)
            od.append(o_i)
            lse.append(l_i)

        x = _outproj(x, oa, qkv, oc, od, lse, w_out_p, b_conv, gains[1:2], g1, l)
        x = _ffn(x, gains[2:3], sc2, sh2, g2, gains[3:4], w_up_b, ffn_conv, w_down_b, l)
    return x
```

```python
import functools
import math

import jax
import jax.numpy as jnp
from jax import lax
from jax.experimental import pallas as pl
from jax.experimental.pallas import tpu as pltpu

D_MODEL = 1024
HEAD_DIM = 64
A_Q_HEADS = 4
A_KV_HEADS = 2
A_RADIUS = 128
B_WIDTH = 256
C_HEADS = 4
C_Q_RANK = 256
C_KV_RANK = 128
C_NOPE_DIM = 64
C_ROPE_DIM = 32
C_V_DIM = 64
ROPE_THETA = 10000.0
D_HEADS = 4
D_PATTERNS = ((128, 1), (512, 4), (2048, 16))
REL_BUCKETS = 32
REL_MAX_DISTANCE = 1024
D_FF = 2816
EPS = 1e-6
NEG = -1e30

LANES = 128
BF16_ROWS = 16
A_SUB_Q = 128
D_SUB_Q = 128
BAND_ROWS = 2048
VMEM_LIMIT = 48 * 1024 * 1024

OFF_AQ, OFF_AK, OFF_AV = 0, 256, 384
OFF_BB, OFF_BC, OFF_BH = 512, 768, 1024
OFF_D = 1280
A_HEAD_ORDER = (0, 2, 1, 3)
PATTERN_COLS = 3 * D_HEADS * HEAD_DIM
MAIN_COLS = OFF_D + PATTERN_COLS
OFF_C = MAIN_COLS
C_COLS = C_Q_RANK + C_KV_RANK + LANES
OFF_D4 = OFF_C + C_COLS
COL_CHUNK = 512

F32 = jnp.float32
BF16 = jnp.bfloat16


def _layer_spec(w, layer, **kw):
    return pl.BlockSpec((None,) + w.shape[1:], lambda *_: (layer,) + (0,) * (w.ndim - 1), **kw)


def _cparams(sem):
    return pltpu.CompilerParams(dimension_semantics=sem, vmem_limit_bytes=VMEM_LIMIT)


def _rms(v, g):
    return v * lax.rsqrt(jnp.mean(v * v, axis=-1, keepdims=True) + EPS) * g


def _mod_body(c_ref, w_ref, b_ref, o_ref):
    c = c_ref[...]
    act = c / (1.0 + jnp.exp(-c))
    o_ref[...] = jnp.dot(act, w_ref[...], preferred_element_type=F32,
                         precision=lax.Precision.HIGHEST) + b_ref[...]


def _modulation(c, w_mod, b_mod):
    nl, d, _ = w_mod.shape
    b = c.shape[0]
    return pl.pallas_call(
        _mod_body,
        grid=(nl, 6),
        in_specs=[pl.BlockSpec((b, d), lambda l, k: (0, 0)),
                  pl.BlockSpec((None, d, d), lambda l, k: (l, 0, k)),
                  pl.BlockSpec((None, 1, d), lambda l, k: (l, 0, k))],
        out_specs=pl.BlockSpec((None, None, b, d), lambda l, k: (l, k, 0, 0)),
        out_shape=jax.ShapeDtypeStruct((nl, 6, b, d), F32),
        compiler_params=_cparams(("parallel", "parallel")),
        name="modulation",
    )(c, w_mod, b_mod.reshape(nl, 1, 6 * d))


def _latent_qkv(cc, cs, sn, gq_ref, gkv_ref, wq_ref, wkv_ref, swap_ref, q_ref, k_ref, v_ref):
    scale = (C_NOPE_DIM + C_ROPE_DIM) ** -0.5
    nq = _rms(cc[:, :C_Q_RANK], gq_ref[...]).astype(BF16)
    qq = jnp.dot(nq, wq_ref[...], preferred_element_type=F32)
    nkv = _rms(cc[:, C_Q_RANK:C_Q_RANK + C_KV_RANK], gkv_ref[...]).astype(BF16)
    kvv = jnp.dot(nkv, wkv_ref[...], preferred_element_type=F32)
    kr = cc[:, C_Q_RANK + C_KV_RANK:].astype(BF16)
    kr_sw = jnp.dot(kr, swap_ref[...], preferred_element_type=F32)
    k_rope = kr.astype(F32) * cs + kr_sw * sn
    hw = C_HEADS * LANES
    for h in range(C_HEADS):
        a = qq[:, h * LANES:(h + 1) * LANES]
        a_sw = qq[:, hw + h * LANES:hw + (h + 1) * LANES]
        q_ref[:, h * LANES:(h + 1) * LANES] = ((a * cs + a_sw * sn) * scale).astype(BF16)
        k_ref[:, h * LANES:(h + 1) * LANES] = (kvv[:, h * LANES:(h + 1) * LANES] + k_rope).astype(BF16)
    v_ref[...] = kvv[:, hw:].astype(BF16)


def _inproj_body(x_ref, g_ref, sc_ref, sh_ref, w_ref, cos_ref, sin_ref, gq_ref, gkv_ref, wq_ref, wkv_ref, swap_ref,
                 om_ref, o4_ref, o16_ref, q_ref, k_ref, v_ref, hf_ref, *, tm):
    h = _rms(x_ref[...], g_ref[...]) * (1.0 + sc_ref[...]) + sh_ref[...]
    nc = h.shape[-1] // LANES
    for c in range(nc):
        hf_ref[c] = h[:, c * LANES:(c + 1) * LANES]
    hb = h.astype(BF16)
    for c0 in range(0, MAIN_COLS, COL_CHUNK):
        om_ref[:, c0:c0 + COL_CHUNK] = jnp.dot(hb, w_ref[:, c0:c0 + COL_CHUNK],
                                               preferred_element_type=F32).astype(BF16)
    cc = jnp.dot(hb, w_ref[:, OFF_C:OFF_C + C_COLS], preferred_element_type=F32)
    _latent_qkv(cc, cos_ref[...], sin_ref[...], gq_ref, gkv_ref, wq_ref, wkv_ref, swap_ref, q_ref, k_ref, v_ref)
    for o_ref, dil, c0 in ((o4_ref, 4, OFF_D4), (o16_ref, 16, OFF_D4 + PATTERN_COLS)):
        n = tm // dil
        hp = jnp.concatenate(
            [jnp.concatenate([hf_ref[c, pl.ds(r, n, stride=dil), :] for r in range(dil)], axis=0)
             for c in range(nc)], axis=1).astype(BF16)
        res = jnp.dot(hp, w_ref[:, c0:c0 + PATTERN_COLS], preferred_element_type=F32).astype(BF16)
        for r in range(dil):
            o_ref[r] = res[r * n:(r + 1) * n]


def _inproj(x, g, sc, sh, w, cos_t, sin_t, gq, gkv, wq, wkv, swap, layer, tm=1024):
    b, s, d = x.shape
    d4, d16 = D_PATTERNS[1][1], D_PATTERNS[2][1]
    hw = C_HEADS * LANES
    vw = C_HEADS * C_V_DIM

    def const(shape):
        return pl.BlockSpec(shape, lambda bi, i: (0,) * len(shape))

    def rows(width):
        return pl.BlockSpec((None, tm, width), lambda bi, i: (bi, i, 0))

    def per_batch():
        return pl.BlockSpec((None, 1, d), lambda bi, i: (bi, 0, 0))

    return pl.pallas_call(
        functools.partial(_inproj_body, tm=tm),
        grid=(b, s // tm),
        in_specs=[rows(d), const(g.shape), per_batch(), per_batch(),
                  _layer_spec(w, layer, pipeline_mode=pl.Buffered(1)),
                  rows(LANES), rows(LANES), const(gq.shape), const(gkv.shape),
                  _layer_spec(wq, layer), _layer_spec(wkv, layer), const(swap.shape)],
        out_specs=[rows(MAIN_COLS),
                   pl.BlockSpec((None, d4, tm // d4, PATTERN_COLS), lambda bi, i: (bi, 0, i, 0)),
                   pl.BlockSpec((None, d16, tm // d16, PATTERN_COLS), lambda bi, i: (bi, 0, i, 0)),
                   rows(hw), rows(hw), rows(vw)],
        out_shape=[jax.ShapeDtypeStruct((b, s, MAIN_COLS), BF16),
                   jax.ShapeDtypeStruct((b, d4, s // d4, PATTERN_COLS), BF16),
                   jax.ShapeDtypeStruct((b, d16, s // d16, PATTERN_COLS), BF16),
                   jax.ShapeDtypeStruct((b, s, hw), BF16), jax.ShapeDtypeStruct((b, s, hw), BF16),
                   jax.ShapeDtypeStruct((b, s, vw), BF16)],
        scratch_shapes=[pltpu.VMEM((d // LANES, tm, LANES), F32)],
        compiler_params=_cparams(("parallel", "parallel")),
        name="inproj",
    )(x, g, sc, sh, w, cos_t, sin_t, gq, gkv, wq, wkv, swap)


def _t5_bucket(rel):
    half = REL_BUCKETS // 2
    max_exact = half // 2
    n = jnp.abs(rel)
    n_f = jnp.maximum(n, 1).astype(jnp.float32)
    large = max_exact + (jnp.log(n_f / max_exact) / math.log(REL_MAX_DISTANCE / max_exact)
                         * (half - max_exact)).astype(jnp.int32)
    large = jnp.minimum(large, half - 1)
    return jnp.where(rel > 0, half, 0) + jnp.where(n < max_exact, n, large)


def _band_bias(table, radius, stride, sub_q):
    qi = jnp.arange(sub_q)[:, None]
    kj = jnp.arange(sub_q + 2 * radius)[None, :]
    off = kj - radius - qi
    onehot = (_t5_bucket(off * stride)[..., None] == jnp.arange(REL_BUCKETS)).astype(F32)
    bias = jnp.einsum('qkb,bh->hqk', onehot, table.astype(F32), precision=lax.Precision.HIGHEST)
    bias = jnp.where((jnp.abs(off) <= radius)[None], bias, NEG)
    return bias.reshape(-1, bias.shape[-1])


def _band_body(*refs, radius, tq, seq, kv_w, has_sink, has_lse):
    q_ref, kp_ref, kc_ref, kn_ref, vp_ref, vc_ref, vn_ref, bias_ref = refs[:8]
    rest = refs[8:]
    if has_sink:
        sink_ref, rest = rest[0], rest[1:]
    o_ref = rest[0]
    lse_ref = rest[1] if has_lse else None

    i = pl.program_id(2)
    nh = 4
    sub_q = bias_ref.shape[0] // nh
    nk = sub_q + 2 * radius
    per = kv_w // HEAD_DIM
    lane = lax.broadcasted_iota(jnp.int32, (1, kv_w), 1)
    hmask = [(lane >= HEAD_DIM * (h % per)) & (lane < HEAD_DIM * (h % per + 1)) for h in range(nh)]
    scale = HEAD_DIM ** -0.5

    def pick(rows):
        out = rows[(per - 1) * sub_q:per * sub_q]
        for h in range(per - 2, -1, -1):
            out = jnp.where(hmask[h], rows[h * sub_q:(h + 1) * sub_q], out)
        return out

    for rr in range(q_ref.shape[0]):
        kcat = jnp.concatenate([kp_ref[rr], kc_ref[rr], kn_ref[rr]], axis=0)
        vcat = jnp.concatenate([vp_ref[rr], vc_ref[rr], vn_ref[rr]], axis=0)
        for sb in range(tq // sub_q):
            r0 = sb * sub_q
            qs = q_ref[rr, r0:r0 + sub_q, :] * scale
            ks = kcat[r0:r0 + nk]
            vs = vcat[r0:r0 + nk]
            at_edge = r0 - radius < 0 or r0 + sub_q + radius > tq
            kpos = i * tq + (r0 - radius) + lax.broadcasted_iota(jnp.int32, (1, nk), 1)
            inb = (kpos >= 0) & (kpos < seq)
            parts = []
            for h in range(nh):
                src = qs[:, (h // per) * kv_w:(h // per + 1) * kv_w]
                parts.append(jnp.where(hmask[h], src, jnp.zeros_like(src)))
            qst = jnp.concatenate(parts, axis=0)
            sc = lax.dot_general(qst, ks, (((1,), (1,)), ((), ())), preferred_element_type=F32)
            sc = sc + bias_ref[...]
            if at_edge:
                sc = jnp.where(inb, sc, NEG)
            m = jnp.max(sc, axis=-1, keepdims=True)
            if has_sink:
                m = jnp.maximum(m, jnp.max(sink_ref[...], axis=-1, keepdims=True))
            p = jnp.exp(sc - m)
            den = jnp.sum(p, axis=-1, keepdims=True)
            if has_sink:
                den = den + jnp.max(jnp.exp(sink_ref[...] - m), axis=-1, keepdims=True)
            o = jnp.dot(p.astype(BF16), vs, preferred_element_type=F32) / den
            lse = jnp.broadcast_to(m + jnp.log(den), o.shape) if has_lse else None
            for g in range(nh // per):
                rows = slice(g * per * sub_q, (g + 1) * per * sub_q)
                cols = slice(g * kv_w, (g + 1) * kv_w)
                o_ref[rr, r0:r0 + sub_q, cols] = pick(o[rows]).astype(BF16)
                if has_lse:
                    lse_ref[rr, r0:r0 + sub_q, cols] = pick(lse[rows])


def _band_attention(qkv, bias, sink, *, dil, radius, tq, q_off, k_off, v_off, kv_w, has_lse, group=1):
    b = qkv.shape[0]
    seq = qkv.shape[-2]
    s = seq * dil
    tq = min(tq, seq)
    qw = 2 * LANES
    steps = tq // radius
    last = seq // radius - 1
    if dil == 1:
        qkv = qkv.reshape(b, 1, seq, qkv.shape[-1])

    def spec(rows, width, off, row_index):
        return pl.BlockSpec((None, group, rows, width), lambda bi, r, i: (bi, r, row_index(i), off // width))

    def cur(i):
        return i

    def prev(i):
        return jnp.maximum(i * steps - 1, 0)

    def nxt(i):
        return jnp.minimum((i + 1) * steps, last)

    in_specs = [spec(tq, qw, q_off, cur)]
    for off in (k_off, v_off):
        in_specs += [spec(radius, kv_w, off, prev), spec(tq, kv_w, off, cur), spec(radius, kv_w, off, nxt)]
    in_specs.append(pl.BlockSpec(bias.shape, lambda bi, r, i: (0, 0)))
    args = [qkv] * 7 + [bias]
    if sink is not None:
        in_specs.append(pl.BlockSpec(sink.shape, lambda bi, r, i: (0, 0)))
        args.append(sink)

    o_spec = pl.BlockSpec((None, group, tq, qw), lambda bi, r, i: (bi, r, i, 0))
    out_shape = [jax.ShapeDtypeStruct((b, dil, seq, qw), BF16)]
    out_specs = [o_spec]
    if has_lse:
        out_shape.append(jax.ShapeDtypeStruct((b, dil, seq, qw), F32))
        out_specs.append(o_spec)

    outs = pl.pallas_call(
        functools.partial(_band_body, radius=radius, tq=tq, seq=seq, kv_w=kv_w,
                          has_sink=sink is not None, has_lse=has_lse),
        grid=(b, dil // group, seq // tq),
        in_specs=in_specs,
        out_specs=out_specs,
        out_shape=out_shape,
        compiler_params=_cparams(("parallel", "parallel", "parallel")),
        name=f"band_attention_d{dil}_r{radius}",
    )(*args)
    return outs


def _mla_attn_body(q_ref, k_ref, v_ref, o_ref, *, rows):
    lo = lax.broadcasted_iota(jnp.int32, (1, LANES), 1) < C_V_DIM
    for r0 in range(0, q_ref.shape[0], rows):
        for g in range(C_HEADS // 2):
            acc = None
            for half in range(2):
                h = 2 * g + half
                q = q_ref[r0:r0 + rows, h * LANES:(h + 1) * LANES]
                k = k_ref[:, h * LANES:(h + 1) * LANES]
                sc = lax.dot_general(q, k, (((1,), (1,)), ((), ())), preferred_element_type=F32)
                m = jnp.max(sc, axis=-1, keepdims=True)
                p = jnp.exp(sc - m)
                den = jnp.sum(p, axis=-1, keepdims=True)
                o = jnp.dot(p.astype(BF16), v_ref[:, g * LANES:(g + 1) * LANES],
                            preferred_element_type=F32) / den
                acc = o if acc is None else jnp.where(lo, acc, o)
            o_ref[r0:r0 + rows, g * LANES:(g + 1) * LANES] = acc.astype(BF16)


def _mla_attn(q, k, v, tq=1024, rows=256):
    b, s, hw = q.shape
    vw = v.shape[-1]
    return pl.pallas_call(
        functools.partial(_mla_attn_body, rows=rows),
        grid=(b, s // tq),
        in_specs=[pl.BlockSpec((None, tq, hw), lambda bi, i: (bi, i, 0)),
                  pl.BlockSpec((None, s, hw), lambda bi, i: (bi, 0, 0)),
                  pl.BlockSpec((None, s, vw), lambda bi, i: (bi, 0, 0))],
        out_specs=pl.BlockSpec((None, tq, vw), lambda bi, i: (bi, i, 0)),
        out_shape=jax.ShapeDtypeStruct((b, s, vw), BF16),
        compiler_params=_cparams(("parallel", "arbitrary")),
        name="mla_attention",
    )(q, k, v)


def _outproj_body(x_ref, oa_ref, bb_ref, bc_ref, bh_ref, bcp_ref, bhp_ref, bcn_ref, bhn_ref, oc_ref,
                  o0_ref, o1_ref, o2_ref, l0_ref, l1_ref, l2_ref, w_ref, cw_ref, gn_ref, gate_ref,
                  out_ref, conv_ref, *perm_refs, tm):
    i = pl.program_id(1)

    def natural(ref, s_ref):
        dil, n, width = ref.shape
        for r in range(dil):
            blk = ref[r].astype(F32)
            for c in range(width // LANES):
                s_ref[c, pl.ds(r, n, stride=dil), :] = blk[:, c * LANES:(c + 1) * LANES]
        return jnp.concatenate([s_ref[c] for c in range(width // LANES)], axis=1)

    pad = 8
    first = (i > 0).astype(F32)
    final = (i < pl.num_programs(1) - 1).astype(F32)
    conv_ref[pad:pad + tm, :] = bc_ref[...].astype(F32) * bh_ref[...].astype(F32)
    conv_ref[pad - 1:pad, :] = (bcp_ref[BF16_ROWS - 1:, :].astype(F32)
                                * bhp_ref[BF16_ROWS - 1:, :].astype(F32)) * first
    conv_ref[pad + tm:pad + tm + 1, :] = (bcn_ref[:1, :].astype(F32) * bhn_ref[:1, :].astype(F32)) * final
    cw = cw_ref[...]
    conv = (conv_ref[pad - 1:pad - 1 + tm, :] * cw[0:1] + conv_ref[pad:pad + tm, :] * cw[1:2]
            + conv_ref[pad + 1:pad + 1 + tm, :] * cw[2:3])
    ob = (bb_ref[...].astype(F32) * conv).astype(BF16)
    l0 = l0_ref[0]
    l1, l2 = natural(l1_ref, perm_refs[0]), natural(l2_ref, perm_refs[1])
    o1, o2 = natural(o1_ref, perm_refs[2]), natural(o2_ref, perm_refs[3])
    mx = jnp.maximum(jnp.maximum(l0, l1), l2)
    e0, e1, e2 = jnp.exp(l0 - mx), jnp.exp(l1 - mx), jnp.exp(l2 - mx)
    od = ((o0_ref[0].astype(F32) * e0 + o1 * e1 + o2 * e2) / (e0 + e1 + e2)).astype(BF16)
    wd = 2 * LANES
    y = jnp.dot(oa_ref[...], w_ref[0:wd, :], preferred_element_type=F32)
    y = y + jnp.dot(ob, w_ref[wd:2 * wd, :], preferred_element_type=F32)
    y = y + jnp.dot(oc_ref[...], w_ref[2 * wd:3 * wd, :], preferred_element_type=F32)
    y = y + jnp.dot(od, w_ref[3 * wd:4 * wd, :], preferred_element_type=F32)
    out_ref[...] = x_ref[...] + gate_ref[...] * _rms(y, gn_ref[...])


def _outproj(x, oa, qkv, oc, od, lse, w, cw, gn, gate, layer, tm=1024):
    b, s, d = x.shape
    wd = 2 * LANES
    hb = tm // BF16_ROWS
    last = s // BF16_ROWS - 1

    def rows(width, off=0):
        return pl.BlockSpec((None, tm, width), lambda bi, i: (bi, i, off // width))

    def halo_prev(off):
        return pl.BlockSpec((None, BF16_ROWS, wd), lambda bi, i: (bi, jnp.maximum(i * hb - 1, 0), off // wd))

    def halo_next(off):
        return pl.BlockSpec((None, BF16_ROWS, wd), lambda bi, i: (bi, jnp.minimum((i + 1) * hb, last), off // wd))

    def const(shape):
        return pl.BlockSpec(shape, lambda bi, i: (0,) * len(shape))

    def grouped(arr):
        dil = arr.shape[1]
        return pl.BlockSpec((None, dil, tm // dil, wd), lambda bi, i: (bi, 0, i, 0))

    return pl.pallas_call(
        functools.partial(_outproj_body, tm=tm),
        grid=(b, s // tm),
        in_specs=[rows(d), rows(wd), rows(wd, OFF_BB), rows(wd, OFF_BC), rows(wd, OFF_BH),
                  halo_prev(OFF_BC), halo_prev(OFF_BH), halo_next(OFF_BC), halo_next(OFF_BH),
                  rows(wd)] + [grouped(a) for a in od] + [grouped(a) for a in lse] + [
                  _layer_spec(w, layer), _layer_spec(cw, layer), const(gn.shape),
                  pl.BlockSpec((None, 1, d), lambda bi, i: (bi, 0, 0))],
        out_specs=rows(d),
        out_shape=jax.ShapeDtypeStruct((b, s, d), F32),
        scratch_shapes=[pltpu.VMEM((tm + 16, wd), F32)] + [pltpu.VMEM((wd // LANES, tm, LANES), F32)] * 4,
        compiler_params=_cparams(("parallel", "arbitrary")),
        name="outproj",
    )(x, oa, qkv, qkv, qkv, qkv, qkv, qkv, qkv, oc, od[0], od[1], od[2], lse[0], lse[1], lse[2],
      w, cw, gn, gate)


def _ffn_body(x_ref, xp_ref, xn_ref, g_ref, sc_ref, sh_ref, gate_ref, gy_ref, wup_ref, cw_ref, wd_ref, out_ref,
              hf_ref, u_ref, a_ref, *, tm, tn):
    i = pl.program_id(1)
    ff = wd_ref.shape[0]
    d = x_ref.shape[-1]
    nc = d // LANES
    sub = 8
    seg = tm // sub

    def mod(v):
        return _rms(v, g_ref[...]) * (1.0 + sc_ref[...]) + sh_ref[...]

    def to_tile_order(ref):
        return jnp.concatenate(
            [jnp.concatenate([ref[c, pl.ds(a, sub, stride=seg), :] for a in range(seg)], axis=0)
             for c in range(nc)], axis=1)

    def to_seq_order(ref):
        return jnp.concatenate(
            [jnp.concatenate([ref[c, pl.ds(b, seg, stride=sub), :] for b in range(sub)], axis=0)
             for c in range(nc)], axis=1)

    hm = mod(x_ref[...])
    for c in range(nc):
        hf_ref[c] = hm[:, c * LANES:(c + 1) * LANES]
    first = (i > 0).astype(F32)
    final = (i < pl.num_programs(1) - 1).astype(F32)
    edge = jnp.concatenate([mod(xp_ref[sub:, :]) * first, mod(xn_ref[:sub, :]) * final], axis=0)
    h = jnp.concatenate([to_tile_order(hf_ref), edge], axis=0).astype(BF16)

    row = lax.broadcasted_iota(jnp.int32, (sub, 1), 0)

    def conv(slot, half, c0):
        lanes = slice(half * tn, (half + 1) * tn)
        cw = cw_ref[:, c0:c0 + tn]
        mid = u_ref[slot, 0:tm, lanes]
        top = jnp.where(row == 0, pltpu.roll(u_ref[slot, tm:tm + sub, lanes], 1, axis=0),
                        pltpu.roll(u_ref[slot, tm - sub:tm, lanes], 1, axis=0))
        prev = jnp.concatenate([top, u_ref[slot, 0:tm - sub, lanes]], axis=0)
        bot = jnp.where(row == sub - 1, pltpu.roll(u_ref[slot, tm + sub:tm + 2 * sub, lanes], sub - 1, axis=0),
                        pltpu.roll(u_ref[slot, 0:sub, lanes], sub - 1, axis=0))
        nxt = jnp.concatenate([u_ref[slot, sub:tm, lanes], bot], axis=0)
        return prev * cw[0:1] + mid * cw[1:2] + nxt * cw[2:3]

    for j in range(ff // tn):
        slot = j % 2
        c0 = j * tn
        u_ref[slot, :, 0:tn] = jnp.dot(h, wup_ref[:, c0:c0 + tn], preferred_element_type=F32)
        u_ref[slot, :, tn:2 * tn] = jnp.dot(h, wup_ref[:, ff + c0:ff + c0 + tn], preferred_element_type=F32)
        a = jax.nn.gelu(conv(slot, 0, c0), approximate=True) * conv(slot, 1, ff + c0)
        a_ref[:, c0:c0 + tn] = a.astype(BF16)

    y = jnp.dot(a_ref[...], wd_ref[...], preferred_element_type=F32)
    for c in range(nc):
        hf_ref[c] = y[:, c * LANES:(c + 1) * LANES]
    out_ref[...] = x_ref[...] + gate_ref[...] * _rms(to_seq_order(hf_ref), gy_ref[...])


def _ffn(x, g, sc, sh, gate, gy, w_up, cw, w_down, layer, tm=512, tn=256):
    b, s, d = x.shape
    ff = w_down.shape[1]
    hb = tm // BF16_ROWS
    last = s // BF16_ROWS - 1

    def const(shape, **kw):
        return pl.BlockSpec(shape, lambda bi, i: (0,) * len(shape), **kw)

    def per_batch():
        return pl.BlockSpec((None, 1, d), lambda bi, i: (bi, 0, 0))

    resident = dict(pipeline_mode=pl.Buffered(1))
    return pl.pallas_call(
        functools.partial(_ffn_body, tm=tm, tn=tn),
        grid=(b, s // tm),
        in_specs=[pl.BlockSpec((None, tm, d), lambda bi, i: (bi, i, 0)),
                  pl.BlockSpec((None, BF16_ROWS, d), lambda bi, i: (bi, jnp.maximum(i * hb - 1, 0), 0)),
                  pl.BlockSpec((None, BF16_ROWS, d), lambda bi, i: (bi, jnp.minimum((i + 1) * hb, last), 0)),
                  const(g.shape), per_batch(), per_batch(), per_batch(), const(gy.shape),
                  _layer_spec(w_up, layer, **resident), _layer_spec(cw, layer),
                  _layer_spec(w_down, layer, **resident)],
        out_specs=pl.BlockSpec((None, tm, d), lambda bi, i: (bi, i, 0)),
        out_shape=jax.ShapeDtypeStruct((b, s, d), F32),
        scratch_shapes=[pltpu.VMEM((d // LANES, tm, LANES), F32),
                        pltpu.VMEM((2, tm + BF16_ROWS, 2 * tn), F32),
                        pltpu.VMEM((tm, ff), BF16)],
        compiler_params=_cparams(("parallel", "arbitrary")),
        name="ffn",
    )(x, x, x, g, sc, sh, gate, gy, w_up, cw, w_down)


def _pack_w_in(w_in):
    nl, d, _ = w_in.shape
    hd = HEAD_DIM
    aq = w_in[..., 0:256]
    aq = jnp.concatenate([aq[..., 0:hd], aq[..., 2 * hd:3 * hd], aq[..., hd:2 * hd], aq[..., 3 * hd:]], axis=-1)
    ckr = w_in[..., 1664:1696]
    ckr = jnp.concatenate([jnp.zeros((nl, d, C_NOPE_DIM), w_in.dtype), ckr,
                           jnp.zeros((nl, d, LANES - C_NOPE_DIM - C_ROPE_DIM), w_in.dtype)], axis=-1)
    d0 = 1696
    return jnp.concatenate([aq, w_in[..., 256:1280], w_in[..., d0:d0 + PATTERN_COLS], w_in[..., 1280:1664], ckr,
                            w_in[..., d0 + PATTERN_COLS:]], axis=-1).astype(BF16)


def _pack_w_uq(w):
    nl, r, _ = w.shape
    half = C_ROPE_DIM // 2
    w = w.reshape(nl, r, C_HEADS, C_NOPE_DIM + C_ROPE_DIM)
    nope, r1, r2 = w[..., :C_NOPE_DIM], w[..., C_NOPE_DIM:C_NOPE_DIM + half], w[..., C_NOPE_DIM + half:]
    z_tail = jnp.zeros((nl, r, C_HEADS, LANES - C_NOPE_DIM - C_ROPE_DIM), w.dtype)
    plain = jnp.concatenate([nope, r1, r2, z_tail], axis=-1).reshape(nl, r, C_HEADS * LANES)
    swapped = jnp.concatenate([jnp.zeros_like(nope), r2, r1, z_tail], axis=-1).reshape(nl, r, C_HEADS * LANES)
    return jnp.concatenate([plain, swapped], axis=-1).astype(BF16)


def _pack_w_ukv(w):
    nl, r, _ = w.shape
    w = w.reshape(nl, r, C_HEADS, C_NOPE_DIM + C_V_DIM)
    kn = jnp.concatenate([w[..., :C_NOPE_DIM], jnp.zeros((nl, r, C_HEADS, LANES - C_NOPE_DIM), w.dtype)], axis=-1)
    return jnp.concatenate([kn.reshape(nl, r, C_HEADS * LANES),
                            w[..., C_NOPE_DIM:].reshape(nl, r, C_HEADS * C_V_DIM)], axis=-1).astype(BF16)


def _pack_w_out(w_out):
    hd = HEAD_DIM
    return jnp.concatenate([w_out[:, 0:hd], w_out[:, 2 * hd:3 * hd], w_out[:, hd:2 * hd], w_out[:, 3 * hd:]],
                           axis=1).astype(BF16)


def _rope_tables(positions):
    half = C_ROPE_DIM // 2
    inv_freq = ROPE_THETA ** (-jnp.arange(half, dtype=jnp.float32) / half)
    ang = positions.astype(jnp.float32)[..., None] * inv_freq
    cos, sin = jnp.cos(ang), jnp.sin(ang)
    lead = positions.shape + (C_NOPE_DIM,)
    tail = positions.shape + (LANES - C_NOPE_DIM - C_ROPE_DIM,)
    cos_t = jnp.concatenate([jnp.ones(lead, F32), cos, cos, jnp.zeros(tail, F32)], axis=-1)
    sin_t = jnp.concatenate([jnp.zeros(lead, F32), -sin, sin, jnp.zeros(tail, F32)], axis=-1)
    return cos_t, sin_t


def _swap_matrix():
    half = C_ROPE_DIM // 2
    idx = jnp.arange(LANES)
    src = jnp.where((idx >= C_NOPE_DIM) & (idx < C_NOPE_DIM + half), idx + half,
                    jnp.where((idx >= C_NOPE_DIM + half) & (idx < C_NOPE_DIM + C_ROPE_DIM), idx - half, idx))
    return (idx[:, None] == src[None, :]).astype(BF16)


def kernel(x, c, positions, rel_bias, w_mod, b_mod, norm_g, w_in, a_sink, b_conv, c_norm_q, c_norm_kv,
           c_w_uq, c_w_ukv, w_out, w_up, ffn_conv, w_down):
    b, s, d = x.shape
    depth = w_in.shape[0]

    a_order = jnp.array(A_HEAD_ORDER)
    bias_a = _band_bias(rel_bias[:, :A_Q_HEADS][:, a_order], A_RADIUS, 1, A_SUB_Q)
    bias_d = [_band_bias(rel_bias[:, A_Q_HEADS + i * D_HEADS:A_Q_HEADS + (i + 1) * D_HEADS], (w // 2) // dil, dil,
                         D_SUB_Q)
              for i, (w, dil) in enumerate(D_PATTERNS)]
    cos_t, sin_t = _rope_tables(positions)
    swap = _swap_matrix()

    w_in_p = _pack_w_in(w_in)
    w_uq_p = _pack_w_uq(c_w_uq)
    w_ukv_p = _pack_w_ukv(c_w_ukv)
    w_out_p = _pack_w_out(w_out)
    w_up_b = w_up.astype(BF16)
    w_down_b = w_down.astype(BF16)

    mod = _modulation(c, w_mod, b_mod)

    for l in range(depth):
        sh1, sc1, g1, sh2, sc2, g2 = (mod[l, k].reshape(b, 1, d) for k in range(6))
        gains = norm_g[l]

        qkv, qkv_d4, qkv_d16, qc, kc, vc = _inproj(x, gains[0:1], sc1, sh1, w_in_p, cos_t, sin_t, c_norm_q[l][None],
                                                   c_norm_kv[l][None], w_uq_p, w_ukv_p, swap, l)

        sink = jnp.broadcast_to(jnp.repeat(a_sink[l][a_order], A_SUB_Q)[:, None], (A_Q_HEADS * A_SUB_Q, LANES))
        oa = _band_attention(qkv, bias_a, sink, dil=1, radius=A_RADIUS, tq=BAND_ROWS, q_off=OFF_AQ, k_off=OFF_AK,
                             v_off=OFF_AV, kv_w=LANES, has_lse=False)[0].reshape(b, s, 2 * LANES)

        oc = _mla_attn(qc, kc, vc)

        od, lse = [], []
        for i, ((w, dil), src) in enumerate(zip(D_PATTERNS, (qkv, qkv_d4, qkv_d16))):
            off = OFF_D if dil == 1 else 0
            seq = s // dil
            o_i, l_i = _band_attention(src, bias_d[i], None, dil=dil, radius=(w // 2) // dil, tq=BAND_ROWS, q_off=off,
                                       k_off=off + 256, v_off=off + 512, kv_w=2 * LANES, has_lse=True,
                                       group=max(1, BAND_ROWS // seq))
            od.append(o_i)
            lse.append(l_i)

        x = _outproj(x, oa, qkv, oc, od, lse, w_out_p, b_conv, gains[1:2], g1, l)
        x = _ffn(x, gains[2:3], sc2, sh2, g2, gains[3:4], w_up_b, ffn_conv, w_down_b, l)
    return x
```

```python
import functools
import math

import jax
import jax.numpy as jnp
from jax import lax
from jax.experimental import pallas as pl
from jax.experimental.pallas import tpu as pltpu

D_MODEL = 1024
HEAD_DIM = 64
A_Q_HEADS = 4
A_KV_HEADS = 2
A_RADIUS = 128
B_WIDTH = 256
C_HEADS = 4
C_Q_RANK = 256
C_KV_RANK = 128
C_NOPE_DIM = 64
C_ROPE_DIM = 32
C_V_DIM = 64
ROPE_THETA = 10000.0
D_HEADS = 4
D_PATTERNS = ((128, 1), (512, 4), (2048, 16))
REL_BUCKETS = 32
REL_MAX_DISTANCE = 1024
D_FF = 2816
EPS = 1e-6
NEG = -1e30

LANES = 128
BF16_ROWS = 16
A_SUB_Q = 128
D_SUB_Q = 128
BAND_ROWS = 4096
VMEM_LIMIT = 48 * 1024 * 1024

OFF_AQ, OFF_AK, OFF_AV = 0, 256, 384
OFF_BB, OFF_BC, OFF_BH = 512, 768, 1024
OFF_D = 1280
A_HEAD_ORDER = (0, 2, 1, 3)
PATTERN_COLS = 3 * D_HEADS * HEAD_DIM
MAIN_COLS = OFF_D + PATTERN_COLS
OFF_C = MAIN_COLS
C_COLS = C_Q_RANK + C_KV_RANK + LANES
OFF_D4 = OFF_C + C_COLS
COL_CHUNK = 512

F32 = jnp.float32
BF16 = jnp.bfloat16


def _layer_spec(w, layer, **kw):
    return pl.BlockSpec((None,) + w.shape[1:], lambda *_: (layer,) + (0,) * (w.ndim - 1), **kw)


def _cparams(sem):
    return pltpu.CompilerParams(dimension_semantics=sem, vmem_limit_bytes=VMEM_LIMIT)


def _rms(v, g):
    return v * lax.rsqrt(jnp.mean(v * v, axis=-1, keepdims=True) + EPS) * g


def _mod_body(c_ref, w_ref, b_ref, o_ref):
    c = c_ref[...]
    act = c / (1.0 + jnp.exp(-c))
    o_ref[...] = jnp.dot(act, w_ref[...], preferred_element_type=F32,
                         precision=lax.Precision.HIGHEST) + b_ref[...]


def _modulation(c, w_mod, b_mod):
    nl, d, _ = w_mod.shape
    b = c.shape[0]
    return pl.pallas_call(
        _mod_body,
        grid=(nl, 6),
        in_specs=[pl.BlockSpec((b, d), lambda l, k: (0, 0)),
                  pl.BlockSpec((None, d, d), lambda l, k: (l, 0, k)),
                  pl.BlockSpec((None, 1, d), lambda l, k: (l, 0, k))],
        out_specs=pl.BlockSpec((None, None, b, d), lambda l, k: (l, k, 0, 0)),
        out_shape=jax.ShapeDtypeStruct((nl, 6, b, d), F32),
        compiler_params=_cparams(("parallel", "parallel")),
        name="modulation",
    )(c, w_mod, b_mod.reshape(nl, 1, 6 * d))


def _latent_qkv(cc, cs, sn, gq_ref, gkv_ref, wq_ref, wkv_ref, swap_ref, q_ref, k_ref, v_ref):
    scale = (C_NOPE_DIM + C_ROPE_DIM) ** -0.5
    nq = _rms(cc[:, :C_Q_RANK], gq_ref[...]).astype(BF16)
    qq = jnp.dot(nq, wq_ref[...], preferred_element_type=F32)
    nkv = _rms(cc[:, C_Q_RANK:C_Q_RANK + C_KV_RANK], gkv_ref[...]).astype(BF16)
    kvv = jnp.dot(nkv, wkv_ref[...], preferred_element_type=F32)
    kr = cc[:, C_Q_RANK + C_KV_RANK:].astype(BF16)
    kr_sw = jnp.dot(kr, swap_ref[...], preferred_element_type=F32)
    k_rope = kr.astype(F32) * cs + kr_sw * sn
    hw = C_HEADS * LANES
    for h in range(C_HEADS):
        a = qq[:, h * LANES:(h + 1) * LANES]
        a_sw = qq[:, hw + h * LANES:hw + (h + 1) * LANES]
        q_ref[:, h * LANES:(h + 1) * LANES] = ((a * cs + a_sw * sn) * scale).astype(BF16)
        k_ref[:, h * LANES:(h + 1) * LANES] = (kvv[:, h * LANES:(h + 1) * LANES] + k_rope).astype(BF16)
    v_ref[...] = kvv[:, hw:].astype(BF16)


def _inproj_body(x_ref, g_ref, sc_ref, sh_ref, w_ref, cos_ref, sin_ref, gq_ref, gkv_ref, wq_ref, wkv_ref, swap_ref,
                 om_ref, o4_ref, o16_ref, q_ref, k_ref, v_ref, hf_ref, *, tm):
    h = _rms(x_ref[...], g_ref[...]) * (1.0 + sc_ref[...]) + sh_ref[...]
    nc = h.shape[-1] // LANES
    for c in range(nc):
        hf_ref[c] = h[:, c * LANES:(c + 1) * LANES]
    hb = h.astype(BF16)
    for c0 in range(0, MAIN_COLS, COL_CHUNK):
        om_ref[:, c0:c0 + COL_CHUNK] = jnp.dot(hb, w_ref[:, c0:c0 + COL_CHUNK],
                                               preferred_element_type=F32).astype(BF16)
    cc = jnp.dot(hb, w_ref[:, OFF_C:OFF_C + C_COLS], preferred_element_type=F32)
    _latent_qkv(cc, cos_ref[...], sin_ref[...], gq_ref, gkv_ref, wq_ref, wkv_ref, swap_ref, q_ref, k_ref, v_ref)
    for o_ref, dil, c0 in ((o4_ref, 4, OFF_D4), (o16_ref, 16, OFF_D4 + PATTERN_COLS)):
        n = tm // dil
        hp = jnp.concatenate(
            [jnp.concatenate([hf_ref[c, pl.ds(r, n, stride=dil), :] for r in range(dil)], axis=0)
             for c in range(nc)], axis=1).astype(BF16)
        res = jnp.dot(hp, w_ref[:, c0:c0 + PATTERN_COLS], preferred_element_type=F32).astype(BF16)
        for r in range(dil):
            o_ref[r] = res[r * n:(r + 1) * n]


def _inproj(x, g, sc, sh, w, cos_t, sin_t, gq, gkv, wq, wkv, swap, layer, tm=1024):
    b, s, d = x.shape
    d4, d16 = D_PATTERNS[1][1], D_PATTERNS[2][1]
    hw = C_HEADS * LANES
    vw = C_HEADS * C_V_DIM

    def const(shape):
        return pl.BlockSpec(shape, lambda bi, i: (0,) * len(shape))

    def rows(width):
        return pl.BlockSpec((None, tm, width), lambda bi, i: (bi, i, 0))

    def per_batch():
        return pl.BlockSpec((None, 1, d), lambda bi, i: (bi, 0, 0))

    return pl.pallas_call(
        functools.partial(_inproj_body, tm=tm),
        grid=(b, s // tm),
        in_specs=[rows(d), const(g.shape), per_batch(), per_batch(),
                  _layer_spec(w, layer, pipeline_mode=pl.Buffered(1)),
                  rows(LANES), rows(LANES), const(gq.shape), const(gkv.shape),
                  _layer_spec(wq, layer), _layer_spec(wkv, layer), const(swap.shape)],
        out_specs=[rows(MAIN_COLS),
                   pl.BlockSpec((None, d4, tm // d4, PATTERN_COLS), lambda bi, i: (bi, 0, i, 0)),
                   pl.BlockSpec((None, d16, tm // d16, PATTERN_COLS), lambda bi, i: (bi, 0, i, 0)),
                   rows(hw), rows(hw), rows(vw)],
        out_shape=[jax.ShapeDtypeStruct((b, s, MAIN_COLS), BF16),
                   jax.ShapeDtypeStruct((b, d4, s // d4, PATTERN_COLS), BF16),
                   jax.ShapeDtypeStruct((b, d16, s // d16, PATTERN_COLS), BF16),
                   jax.ShapeDtypeStruct((b, s, hw), BF16), jax.ShapeDtypeStruct((b, s, hw), BF16),
                   jax.ShapeDtypeStruct((b, s, vw), BF16)],
        scratch_shapes=[pltpu.VMEM((d // LANES, tm, LANES), F32)],
        compiler_params=_cparams(("parallel", "parallel")),
        name="inproj",
    )(x, g, sc, sh, w, cos_t, sin_t, gq, gkv, wq, wkv, swap)


def _t5_bucket(rel):
    half = REL_BUCKETS // 2
    max_exact = half // 2
    n = jnp.abs(rel)
    n_f = jnp.maximum(n, 1).astype(jnp.float32)
    large = max_exact + (jnp.log(n_f / max_exact) / math.log(REL_MAX_DISTANCE / max_exact)
                         * (half - max_exact)).astype(jnp.int32)
    large = jnp.minimum(large, half - 1)
    return jnp.where(rel > 0, half, 0) + jnp.where(n < max_exact, n, large)


def _band_bias(table, radius, stride, sub_q):
    qi = jnp.arange(sub_q)[:, None]
    kj = jnp.arange(sub_q + 2 * radius)[None, :]
    off = kj - radius - qi
    onehot = (_t5_bucket(off * stride)[..., None] == jnp.arange(REL_BUCKETS)).astype(F32)
    bias = jnp.einsum('qkb,bh->hqk', onehot, table.astype(F32), precision=lax.Precision.HIGHEST)
    bias = jnp.where((jnp.abs(off) <= radius)[None], bias, NEG)
    return bias.reshape(-1, bias.shape[-1])


def _band_body(*refs, radius, tq, seq, kv_w, has_sink, has_lse):
    q_ref, kp_ref, kc_ref, kn_ref, vp_ref, vc_ref, vn_ref, bias_ref = refs[:8]
    rest = refs[8:]
    if has_sink:
        sink_ref, rest = rest[0], rest[1:]
    o_ref = rest[0]
    lse_ref = rest[1] if has_lse else None

    i = pl.program_id(2)
    nh = 4
    sub_q = bias_ref.shape[0] // nh
    nk = sub_q + 2 * radius
    per = kv_w // HEAD_DIM
    lane = lax.broadcasted_iota(jnp.int32, (1, kv_w), 1)
    hmask = [(lane >= HEAD_DIM * (h % per)) & (lane < HEAD_DIM * (h % per + 1)) for h in range(nh)]
    scale = HEAD_DIM ** -0.5

    def pick(rows):
        out = rows[(per - 1) * sub_q:per * sub_q]
        for h in range(per - 2, -1, -1):
            out = jnp.where(hmask[h], rows[h * sub_q:(h + 1) * sub_q], out)
        return out

    for rr in range(q_ref.shape[0]):
        kcat = jnp.concatenate([kp_ref[rr], kc_ref[rr], kn_ref[rr]], axis=0)
        vcat = jnp.concatenate([vp_ref[rr], vc_ref[rr], vn_ref[rr]], axis=0)
        for sb in range(tq // sub_q):
            r0 = sb * sub_q
            qs = q_ref[rr, r0:r0 + sub_q, :] * scale
            ks = kcat[r0:r0 + nk]
            vs = vcat[r0:r0 + nk]
            at_edge = r0 - radius < 0 or r0 + sub_q + radius > tq
            kpos = i * tq + (r0 - radius) + lax.broadcasted_iota(jnp.int32, (1, nk), 1)
            inb = (kpos >= 0) & (kpos < seq)
            parts = []
            for h in range(nh):
                src = qs[:, (h // per) * kv_w:(h // per + 1) * kv_w]
                parts.append(jnp.where(hmask[h], src, jnp.zeros_like(src)))
            qst = jnp.concatenate(parts, axis=0)
            sc = lax.dot_general(qst, ks, (((1,), (1,)), ((), ())), preferred_element_type=F32)
            sc = sc + bias_ref[...]
            if at_edge:
                sc = jnp.where(inb, sc, NEG)
            m = jnp.max(sc, axis=-1, keepdims=True)
            if has_sink:
                m = jnp.maximum(m, jnp.max(sink_ref[...], axis=-1, keepdims=True))
            p = jnp.exp(sc - m)
            den = jnp.sum(p, axis=-1, keepdims=True)
            if has_sink:
                den = den + jnp.max(jnp.exp(sink_ref[...] - m), axis=-1, keepdims=True)
            o = jnp.dot(p.astype(BF16), vs, preferred_element_type=F32) / den
            lse = jnp.broadcast_to(m + jnp.log(den), o.shape) if has_lse else None
            for g in range(nh // per):
                rows = slice(g * per * sub_q, (g + 1) * per * sub_q)
                cols = slice(g * kv_w, (g + 1) * kv_w)
                o_ref[rr, r0:r0 + sub_q, cols] = pick(o[rows]).astype(BF16)
                if has_lse:
                    lse_ref[rr, r0:r0 + sub_q, cols] = pick(lse[rows])


def _band_attention(qkv, bias, sink, *, dil, radius, tq, q_off, k_off, v_off, kv_w, has_lse, group=1):
    b = qkv.shape[0]
    seq = qkv.shape[-2]
    s = seq * dil
    tq = min(tq, seq)
    qw = 2 * LANES
    steps = tq // radius
    last = seq // radius - 1
    if dil == 1:
        qkv = qkv.reshape(b, 1, seq, qkv.shape[-1])

    def spec(rows, width, off, row_index):
        return pl.BlockSpec((None, group, rows, width), lambda bi, r, i: (bi, r, row_index(i), off // width))

    def cur(i):
        return i

    def prev(i):
        return jnp.maximum(i * steps - 1, 0)

    def nxt(i):
        return jnp.minimum((i + 1) * steps, last)

    in_specs = [spec(tq, qw, q_off, cur)]
    for off in (k_off, v_off):
        in_specs += [spec(radius, kv_w, off, prev), spec(tq, kv_w, off, cur), spec(radius, kv_w, off, nxt)]
    in_specs.append(pl.BlockSpec(bias.shape, lambda bi, r, i: (0, 0)))
    args = [qkv] * 7 + [bias]
    if sink is not None:
        in_specs.append(pl.BlockSpec(sink.shape, lambda bi, r, i: (0, 0)))
        args.append(sink)

    o_spec = pl.BlockSpec((None, group, tq, qw), lambda bi, r, i: (bi, r, i, 0))
    out_shape = [jax.ShapeDtypeStruct((b, dil, seq, qw), BF16)]
    out_specs = [o_spec]
    if has_lse:
        out_shape.append(jax.ShapeDtypeStruct((b, dil, seq, qw), F32))
        out_specs.append(o_spec)

    outs = pl.pallas_call(
        functools.partial(_band_body, radius=radius, tq=tq, seq=seq, kv_w=kv_w,
                          has_sink=sink is not None, has_lse=has_lse),
        grid=(b, dil // group, seq // tq),
        in_specs=in_specs,
        out_specs=out_specs,
        out_shape=out_shape,
        compiler_params=_cparams(("parallel", "parallel", "parallel")),
        name=f"band_attention_d{dil}_r{radius}",
    )(*args)
    return outs


def _mla_attn_body(q_ref, k_ref, v_ref, o_ref, *, rows):
    lo = lax.broadcasted_iota(jnp.int32, (1, LANES), 1) < C_V_DIM
    for r0 in range(0, q_ref.shape[0], rows):
        for g in range(C_HEADS // 2):
            acc = None
            for half in range(2):
                h = 2 * g + half
                q = q_ref[r0:r0 + rows, h * LANES:(h + 1) * LANES]
                k = k_ref[:, h * LANES:(h + 1) * LANES]
                sc = lax.dot_general(q, k, (((1,), (1,)), ((), ())), preferred_element_type=F32)
                m = jnp.max(sc, axis=-1, keepdims=True)
                p = jnp.exp(sc - m)
                den = jnp.sum(p, axis=-1, keepdims=True)
                o = jnp.dot(p.astype(BF16), v_ref[:, g * LANES:(g + 1) * LANES],
                            preferred_element_type=F32) / den
                acc = o if acc is None else jnp.where(lo, acc, o)
            o_ref[r0:r0 + rows, g * LANES:(g + 1) * LANES] = acc.astype(BF16)


def _mla_attn(q, k, v, tq=1024, rows=256):
    b, s, hw = q.shape
    vw = v.shape[-1]
    return pl.pallas_call(
        functools.partial(_mla_attn_body, rows=rows),
        grid=(b, s // tq),
        in_specs=[pl.BlockSpec((None, tq, hw), lambda bi, i: (bi, i, 0)),
                  pl.BlockSpec((None, s, hw), lambda bi, i: (bi, 0, 0)),
                  pl.BlockSpec((None, s, vw), lambda bi, i: (bi, 0, 0))],
        out_specs=pl.BlockSpec((None, tq, vw), lambda bi, i: (bi, i, 0)),
        out_shape=jax.ShapeDtypeStruct((b, s, vw), BF16),
        compiler_params=_cparams(("parallel", "arbitrary")),
        name="mla_attention",
    )(q, k, v)


def _outproj_body(x_ref, oa_ref, bb_ref, bc_ref, bh_ref, bcp_ref, bhp_ref, bcn_ref, bhn_ref, oc_ref,
                  o0_ref, o1_ref, o2_ref, l0_ref, l1_ref, l2_ref, w_ref, cw_ref, gn_ref, gate_ref,
                  out_ref, conv_ref, *perm_refs, tm):
    i = pl.program_id(1)

    def natural(ref, s_ref):
        dil, n, width = ref.shape
        for r in range(dil):
            blk = ref[r].astype(F32)
            for c in range(width // LANES):
                s_ref[c, pl.ds(r, n, stride=dil), :] = blk[:, c * LANES:(c + 1) * LANES]
        return jnp.concatenate([s_ref[c] for c in range(width // LANES)], axis=1)

    pad = 8
    first = (i > 0).astype(F32)
    final = (i < pl.num_programs(1) - 1).astype(F32)
    conv_ref[pad:pad + tm, :] = bc_ref[...].astype(F32) * bh_ref[...].astype(F32)
    conv_ref[pad - 1:pad, :] = (bcp_ref[BF16_ROWS - 1:, :].astype(F32)
                                * bhp_ref[BF16_ROWS - 1:, :].astype(F32)) * first
    conv_ref[pad + tm:pad + tm + 1, :] = (bcn_ref[:1, :].astype(F32) * bhn_ref[:1, :].astype(F32)) * final
    cw = cw_ref[...]
    conv = (conv_ref[pad - 1:pad - 1 + tm, :] * cw[0:1] + conv_ref[pad:pad + tm, :] * cw[1:2]
            + conv_ref[pad + 1:pad + 1 + tm, :] * cw[2:3])
    ob = (bb_ref[...].astype(F32) * conv).astype(BF16)
    l0 = l0_ref[0]
    l1, l2 = natural(l1_ref, perm_refs[0]), natural(l2_ref, perm_refs[1])
    o1, o2 = natural(o1_ref, perm_refs[2]), natural(o2_ref, perm_refs[3])
    mx = jnp.maximum(jnp.maximum(l0, l1), l2)
    e0, e1, e2 = jnp.exp(l0 - mx), jnp.exp(l1 - mx), jnp.exp(l2 - mx)
    od = ((o0_ref[0].astype(F32) * e0 + o1 * e1 + o2 * e2) / (e0 + e1 + e2)).astype(BF16)
    wd = 2 * LANES
    y = jnp.dot(oa_ref[...], w_ref[0:wd, :], preferred_element_type=F32)
    y = y + jnp.dot(ob, w_ref[wd:2 * wd, :], preferred_element_type=F32)
    y = y + jnp.dot(oc_ref[...], w_ref[2 * wd:3 * wd, :], preferred_element_type=F32)
    y = y + jnp.dot(od, w_ref[3 * wd:4 * wd, :], preferred_element_type=F32)
    out_ref[...] = x_ref[...] + gate_ref[...] * _rms(y, gn_ref[...])


def _outproj(x, oa, qkv, oc, od, lse, w, cw, gn, gate, layer, tm=1024):
    b, s, d = x.shape
    wd = 2 * LANES
    hb = tm // BF16_ROWS
    last = s // BF16_ROWS - 1

    def rows(width, off=0):
        return pl.BlockSpec((None, tm, width), lambda bi, i: (bi, i, off // width))

    def halo_prev(off):
        return pl.BlockSpec((None, BF16_ROWS, wd), lambda bi, i: (bi, jnp.maximum(i * hb - 1, 0), off // wd))

    def halo_next(off):
        return pl.BlockSpec((None, BF16_ROWS, wd), lambda bi, i: (bi, jnp.minimum((i + 1) * hb, last), off // wd))

    def const(shape):
        return pl.BlockSpec(shape, lambda bi, i: (0,) * len(shape))

    def grouped(arr):
        dil = arr.shape[1]
        return pl.BlockSpec((None, dil, tm // dil, wd), lambda bi, i: (bi, 0, i, 0))

    return pl.pallas_call(
        functools.partial(_outproj_body, tm=tm),
        grid=(b, s // tm),
        in_specs=[rows(d), rows(wd), rows(wd, OFF_BB), rows(wd, OFF_BC), rows(wd, OFF_BH),
                  halo_prev(OFF_BC), halo_prev(OFF_BH), halo_next(OFF_BC), halo_next(OFF_BH),
                  rows(wd)] + [grouped(a) for a in od] + [grouped(a) for a in lse] + [
                  _layer_spec(w, layer), _layer_spec(cw, layer), const(gn.shape),
                  pl.BlockSpec((None, 1, d), lambda bi, i: (bi, 0, 0))],
        out_specs=rows(d),
        out_shape=jax.ShapeDtypeStruct((b, s, d), F32),
        scratch_shapes=[pltpu.VMEM((tm + 16, wd), F32)] + [pltpu.VMEM((wd // LANES, tm, LANES), F32)] * 4,
        compiler_params=_cparams(("parallel", "arbitrary")),
        name="outproj",
    )(x, oa, qkv, qkv, qkv, qkv, qkv, qkv, qkv, oc, od[0], od[1], od[2], lse[0], lse[1], lse[2],
      w, cw, gn, gate)


def _ffn_body(x_ref, xp_ref, xn_ref, g_ref, sc_ref, sh_ref, gate_ref, gy_ref, wup_ref, cw_ref, wd_ref, out_ref,
              hf_ref, u_ref, a_ref, *, tm, tn):
    i = pl.program_id(1)
    ff = wd_ref.shape[0]
    d = x_ref.shape[-1]
    nc = d // LANES
    sub = 8
    seg = tm // sub

    def mod(v):
        return _rms(v, g_ref[...]) * (1.0 + sc_ref[...]) + sh_ref[...]

    def to_tile_order(ref):
        return jnp.concatenate(
            [jnp.concatenate([ref[c, pl.ds(a, sub, stride=seg), :] for a in range(seg)], axis=0)
             for c in range(nc)], axis=1)

    def to_seq_order(ref):
        return jnp.concatenate(
            [jnp.concatenate([ref[c, pl.ds(b, seg, stride=sub), :] for b in range(sub)], axis=0)
             for c in range(nc)], axis=1)

    hm = mod(x_ref[...])
    for c in range(nc):
        hf_ref[c] = hm[:, c * LANES:(c + 1) * LANES]
    first = (i > 0).astype(F32)
    final = (i < pl.num_programs(1) - 1).astype(F32)
    edge = jnp.concatenate([mod(xp_ref[sub:, :]) * first, mod(xn_ref[:sub, :]) * final], axis=0)
    h = jnp.concatenate([to_tile_order(hf_ref), edge], axis=0).astype(BF16)

    row = lax.broadcasted_iota(jnp.int32, (sub, 1), 0)

    def conv(slot, half, c0):
        lanes = slice(half * tn, (half + 1) * tn)
        cw = cw_ref[:, c0:c0 + tn]
        mid = u_ref[slot, 0:tm, lanes]
        top = jnp.where(row == 0, pltpu.roll(u_ref[slot, tm:tm + sub, lanes], 1, axis=0),
                        pltpu.roll(u_ref[slot, tm - sub:tm, lanes], 1, axis=0))
        prev = jnp.concatenate([top, u_ref[slot, 0:tm - sub, lanes]], axis=0)
        bot = jnp.where(row == sub - 1, pltpu.roll(u_ref[slot, tm + sub:tm + 2 * sub, lanes], sub - 1, axis=0),
                        pltpu.roll(u_ref[slot, 0:sub, lanes], sub - 1, axis=0))
        nxt = jnp.concatenate([u_ref[slot, sub:tm, lanes], bot], axis=0)
        return prev * cw[0:1] + mid * cw[1:2] + nxt * cw[2:3]

    for j in range(ff // tn):
        slot = j % 2
        c0 = j * tn
        u_ref[slot, :, 0:tn] = jnp.dot(h, wup_ref[:, c0:c0 + tn], preferred_element_type=F32)
        u_ref[slot, :, tn:2 * tn] = jnp.dot(h, wup_ref[:, ff + c0:ff + c0 + tn], preferred_element_type=F32)
        a = jax.nn.gelu(conv(slot, 0, c0), approximate=True) * conv(slot, 1, ff + c0)
        a_ref[:, c0:c0 + tn] = a.astype(BF16)

    y = jnp.dot(a_ref[...], wd_ref[...], preferred_element_type=F32)
    for c in range(nc):
        hf_ref[c] = y[:, c * LANES:(c + 1) * LANES]
    out_ref[...] = x_ref[...] + gate_ref[...] * _rms(to_seq_order(hf_ref), gy_ref[...])


def _ffn(x, g, sc, sh, gate, gy, w_up, cw, w_down, layer, tm=512, tn=256):
    b, s, d = x.shape
    ff = w_down.shape[1]
    hb = tm // BF16_ROWS
    last = s // BF16_ROWS - 1

    def const(shape, **kw):
        return pl.BlockSpec(shape, lambda bi, i: (0,) * len(shape), **kw)

    def per_batch():
        return pl.BlockSpec((None, 1, d), lambda bi, i: (bi, 0, 0))

    resident = dict(pipeline_mode=pl.Buffered(1))
    return pl.pallas_call(
        functools.partial(_ffn_body, tm=tm, tn=tn),
        grid=(b, s // tm),
        in_specs=[pl.BlockSpec((None, tm, d), lambda bi, i: (bi, i, 0)),
                  pl.BlockSpec((None, BF16_ROWS, d), lambda bi, i: (bi, jnp.maximum(i * hb - 1, 0), 0)),
                  pl.BlockSpec((None, BF16_ROWS, d), lambda bi, i: (bi, jnp.minimum((i + 1) * hb, last), 0)),
                  const(g.shape), per_batch(), per_batch(), per_batch(), const(gy.shape),
                  _layer_spec(w_up, layer, **resident), _layer_spec(cw, layer),
                  _layer_spec(w_down, layer, **resident)],
        out_specs=pl.BlockSpec((None, tm, d), lambda bi, i: (bi, i, 0)),
        out_shape=jax.ShapeDtypeStruct((b, s, d), F32),
        scratch_shapes=[pltpu.VMEM((d // LANES, tm, LANES), F32),
                        pltpu.VMEM((2, tm + BF16_ROWS, 2 * tn), F32),
                        pltpu.VMEM((tm, ff), BF16)],
        compiler_params=_cparams(("parallel", "arbitrary")),
        name="ffn",
    )(x, x, x, g, sc, sh, gate, gy, w_up, cw, w_down)


def _pack_w_in(w_in):
    nl, d, _ = w_in.shape
    hd = HEAD_DIM
    aq = w_in[..., 0:256]
    aq = jnp.concatenate([aq[..., 0:hd], aq[..., 2 * hd:3 * hd], aq[..., hd:2 * hd], aq[..., 3 * hd:]], axis=-1)
    ckr = w_in[..., 1664:1696]
    ckr = jnp.concatenate([jnp.zeros((nl, d, C_NOPE_DIM), w_in.dtype), ckr,
                           jnp.zeros((nl, d, LANES - C_NOPE_DIM - C_ROPE_DIM), w_in.dtype)], axis=-1)
    d0 = 1696
    return jnp.concatenate([aq, w_in[..., 256:1280], w_in[..., d0:d0 + PATTERN_COLS], w_in[..., 1280:1664], ckr,
                            w_in[..., d0 + PATTERN_COLS:]], axis=-1).astype(BF16)


def _pack_w_uq(w):
    nl, r, _ = w.shape
    half = C_ROPE_DIM // 2
    w = w.reshape(nl, r, C_HEADS, C_NOPE_DIM + C_ROPE_DIM)
    nope, r1, r2 = w[..., :C_NOPE_DIM], w[..., C_NOPE_DIM:C_NOPE_DIM + half], w[..., C_NOPE_DIM + half:]
    z_tail = jnp.zeros((nl, r, C_HEADS, LANES - C_NOPE_DIM - C_ROPE_DIM), w.dtype)
    plain = jnp.concatenate([nope, r1, r2, z_tail], axis=-1).reshape(nl, r, C_HEADS * LANES)
    swapped = jnp.concatenate([jnp.zeros_like(nope), r2, r1, z_tail], axis=-1).reshape(nl, r, C_HEADS * LANES)
    return jnp.concatenate([plain, swapped], axis=-1).astype(BF16)


def _pack_w_ukv(w):
    nl, r, _ = w.shape
    w = w.reshape(nl, r, C_HEADS, C_NOPE_DIM + C_V_DIM)
    kn = jnp.concatenate([w[..., :C_NOPE_DIM], jnp.zeros((nl, r, C_HEADS, LANES - C_NOPE_DIM), w.dtype)], axis=-1)
    return jnp.concatenate([kn.reshape(nl, r, C_HEADS * LANES),
                            w[..., C_NOPE_DIM:].reshape(nl, r, C_HEADS * C_V_DIM)], axis=-1).astype(BF16)


def _pack_w_out(w_out):
    hd = HEAD_DIM
    return jnp.concatenate([w_out[:, 0:hd], w_out[:, 2 * hd:3 * hd], w_out[:, hd:2 * hd], w_out[:, 3 * hd:]],
                           axis=1).astype(BF16)


def _rope_tables(positions):
    half = C_ROPE_DIM // 2
    inv_freq = ROPE_THETA ** (-jnp.arange(half, dtype=jnp.float32) / half)
    ang = positions.astype(jnp.float32)[..., None] * inv_freq
    cos, sin = jnp.cos(ang), jnp.sin(ang)
    lead = positions.shape + (C_NOPE_DIM,)
    tail = positions.shape + (LANES - C_NOPE_DIM - C_ROPE_DIM,)
    cos_t = jnp.concatenate([jnp.ones(lead, F32), cos, cos, jnp.zeros(tail, F32)], axis=-1)
    sin_t = jnp.concatenate([jnp.zeros(lead, F32), -sin, sin, jnp.zeros(tail, F32)], axis=-1)
    return cos_t, sin_t


def _swap_matrix():
    half = C_ROPE_DIM // 2
    idx = jnp.arange(LANES)
    src = jnp.where((idx >= C_NOPE_DIM) & (idx < C_NOPE_DIM + half), idx + half,
                    jnp.where((idx >= C_NOPE_DIM + half) & (idx < C_NOPE_DIM + C_ROPE_DIM), idx - half, idx))
    return (idx[:, None] == src[None, :]).astype(BF16)


def kernel(x, c, positions, rel_bias, w_mod, b_mod, norm_g, w_in, a_sink, b_conv, c_norm_q, c_norm_kv,
           c_w_uq, c_w_ukv, w_out, w_up, ffn_conv, w_down):
    b, s, d = x.shape
    depth = w_in.shape[0]

    a_order = jnp.array(A_HEAD_ORDER)
    bias_a = _band_bias(rel_bias[:, :A_Q_HEADS][:, a_order], A_RADIUS, 1, A_SUB_Q)
    bias_d = [_band_bias(rel_bias[:, A_Q_HEADS + i * D_HEADS:A_Q_HEADS + (i + 1) * D_HEADS], (w // 2) // dil, dil,
                         D_SUB_Q)
              for i, (w, dil) in enumerate(D_PATTERNS)]
    cos_t, sin_t = _rope_tables(positions)
    swap = _swap_matrix()

    w_in_p = _pack_w_in(w_in)
    w_uq_p = _pack_w_uq(c_w_uq)
    w_ukv_p = _pack_w_ukv(c_w_ukv)
    w_out_p = _pack_w_out(w_out)
    w_up_b = w_up.astype(BF16)
    w_down_b = w_down.astype(BF16)

    mod = _modulation(c, w_mod, b_mod)

    for l in range(depth):
        sh1, sc1, g1, sh2, sc2, g2 = (mod[l, k].reshape(b, 1, d) for k in range(6))
        gains = norm_g[l]

        qkv, qkv_d4, qkv_d16, qc, kc, vc = _inproj(x, gains[0:1], sc1, sh1, w_in_p, cos_t, sin_t, c_norm_q[l][None],
                                                   c_norm_kv[l][None], w_uq_p, w_ukv_p, swap, l)

        sink = jnp.broadcast_to(jnp.repeat(a_sink[l][a_order], A_SUB_Q)[:, None], (A_Q_HEADS * A_SUB_Q, LANES))
        oa = _band_attention(qkv, bias_a, sink, dil=1, radius=A_RADIUS, tq=BAND_ROWS, q_off=OFF_AQ, k_off=OFF_AK,
                             v_off=OFF_AV, kv_w=LANES, has_lse=False)[0].reshape(b, s, 2 * LANES)

        oc = _mla_attn(qc, kc, vc)

        od, lse = [], []
        for i, ((w, dil), src) in enumerate(zip(D_PATTERNS, (qkv, qkv_d4, qkv_d16))):
            off = OFF_D if dil == 1 else 0
            seq = s // dil
            o_i, l_i = _band_attention(src, bias_d[i], None, dil=dil, radius=(w // 2) // dil, tq=BAND_ROWS, q_off=off,
                                       k_off=off + 256, v_off=off + 512, kv_w=2 * LANES, has_lse=True,
                                       group=max(1, BAND_ROWS // seq))
            od.append(o_i)
            lse.append(l_i)

        x = _outproj(x, oa, qkv, oc, od, lse, w_out_p, b_conv, gains[1:2], g1, l)
        x = _ffn(x, gains[2:3], sc2, sh2, g2, gains[3:4], w_up_b, ffn_conv, w_down_b, l)
    return x
```

```python
import functools
import math

import jax
import jax.numpy as jnp
from jax import lax
from jax.experimental import pallas as pl
from jax.experimental.pallas import tpu as pltpu

D_MODEL = 1024
HEAD_DIM = 64
A_Q_HEADS = 4
A_KV_HEADS = 2
A_RADIUS = 128
B_WIDTH = 256
C_HEADS = 4
C_Q_RANK = 256
C_KV_RANK = 128
C_NOPE_DIM = 64
C_ROPE_DIM = 32
C_V_DIM = 64
ROPE_THETA = 10000.0
D_HEADS = 4
D_PATTERNS = ((128, 1), (512, 4), (2048, 16))
REL_BUCKETS = 32
REL_MAX_DISTANCE = 1024
D_FF = 2816
EPS = 1e-6
NEG = -1e30

LANES = 128
BF16_ROWS = 16
A_SUB_Q = 128
D_SUB_Q = 128
BAND_ROWS = 4096
VMEM_LIMIT = 48 * 1024 * 1024

OFF_AQ, OFF_AK, OFF_AV = 0, 256, 384
OFF_BB, OFF_BC, OFF_BH = 512, 768, 1024
OFF_D = 1280
A_HEAD_ORDER = (0, 2, 1, 3)
PATTERN_COLS = 3 * D_HEADS * HEAD_DIM
MAIN_COLS = OFF_D + PATTERN_COLS
OFF_C = MAIN_COLS
C_COLS = C_Q_RANK + C_KV_RANK + LANES
OFF_D4 = OFF_C + C_COLS
COL_CHUNK = 512

F32 = jnp.float32
BF16 = jnp.bfloat16


def _layer_spec(w, layer, **kw):
    return pl.BlockSpec((None,) + w.shape[1:], lambda *_: (layer,) + (0,) * (w.ndim - 1), **kw)


def _cparams(sem):
    return pltpu.CompilerParams(dimension_semantics=sem, vmem_limit_bytes=VMEM_LIMIT)


def _rms(v, g):
    return v * lax.rsqrt(jnp.mean(v * v, axis=-1, keepdims=True) + EPS) * g


def _mod_body(c_ref, w_ref, b_ref, o_ref):
    c = c_ref[...]
    act = c / (1.0 + jnp.exp(-c))
    o_ref[...] = jnp.dot(act, w_ref[...], preferred_element_type=F32,
                         precision=lax.Precision.HIGHEST) + b_ref[...]


def _modulation(c, w_mod, b_mod):
    nl, d, _ = w_mod.shape
    b = c.shape[0]
    return pl.pallas_call(
        _mod_body,
        grid=(nl, 6),
        in_specs=[pl.BlockSpec((b, d), lambda l, k: (0, 0)),
                  pl.BlockSpec((None, d, d), lambda l, k: (l, 0, k)),
                  pl.BlockSpec((None, 1, d), lambda l, k: (l, 0, k))],
        out_specs=pl.BlockSpec((None, None, b, d), lambda l, k: (l, k, 0, 0)),
        out_shape=jax.ShapeDtypeStruct((nl, 6, b, d), F32),
        compiler_params=_cparams(("parallel", "parallel")),
        name="modulation",
    )(c, w_mod, b_mod.reshape(nl, 1, 6 * d))


def _latent_qkv(cc, cs, sn, gq_ref, gkv_ref, wq_ref, wkv_ref, swap_ref, q_ref, k_ref, v_ref):
    scale = (C_NOPE_DIM + C_ROPE_DIM) ** -0.5
    nq = _rms(cc[:, :C_Q_RANK], gq_ref[...]).astype(BF16)
    qq = jnp.dot(nq, wq_ref[...], preferred_element_type=F32)
    nkv = _rms(cc[:, C_Q_RANK:C_Q_RANK + C_KV_RANK], gkv_ref[...]).astype(BF16)
    kvv = jnp.dot(nkv, wkv_ref[...], preferred_element_type=F32)
    kr = cc[:, C_Q_RANK + C_KV_RANK:].astype(BF16)
    kr_sw = jnp.dot(kr, swap_ref[...], preferred_element_type=F32)
    k_rope = kr.astype(F32) * cs + kr_sw * sn
    hw = C_HEADS * LANES
    for h in range(C_HEADS):
        a = qq[:, h * LANES:(h + 1) * LANES]
        a_sw = qq[:, hw + h * LANES:hw + (h + 1) * LANES]
        q_ref[:, h * LANES:(h + 1) * LANES] = ((a * cs + a_sw * sn) * scale).astype(BF16)
        k_ref[:, h * LANES:(h + 1) * LANES] = (kvv[:, h * LANES:(h + 1) * LANES] + k_rope).astype(BF16)
    v_ref[...] = kvv[:, hw:].astype(BF16)


def _inproj_body(x_ref, g_ref, sc_ref, sh_ref, w_ref, cos_ref, sin_ref, gq_ref, gkv_ref, wq_ref, wkv_ref, swap_ref,
                 om_ref, o4_ref, o16_ref, q_ref, k_ref, v_ref, hf_ref, *, tm):
    h = _rms(x_ref[...], g_ref[...]) * (1.0 + sc_ref[...]) + sh_ref[...]
    nc = h.shape[-1] // LANES
    for c in range(nc):
        hf_ref[c] = h[:, c * LANES:(c + 1) * LANES]
    hb = h.astype(BF16)
    for c0 in range(0, MAIN_COLS, COL_CHUNK):
        om_ref[:, c0:c0 + COL_CHUNK] = jnp.dot(hb, w_ref[:, c0:c0 + COL_CHUNK],
                                               preferred_element_type=F32).astype(BF16)
    cc = jnp.dot(hb, w_ref[:, OFF_C:OFF_C + C_COLS], preferred_element_type=F32)
    _latent_qkv(cc, cos_ref[...], sin_ref[...], gq_ref, gkv_ref, wq_ref, wkv_ref, swap_ref, q_ref, k_ref, v_ref)
    for o_ref, dil, c0 in ((o4_ref, 4, OFF_D4), (o16_ref, 16, OFF_D4 + PATTERN_COLS)):
        n = tm // dil
        hp = jnp.concatenate(
            [jnp.concatenate([hf_ref[c, pl.ds(r, n, stride=dil), :] for r in range(dil)], axis=0)
             for c in range(nc)], axis=1).astype(BF16)
        res = jnp.dot(hp, w_ref[:, c0:c0 + PATTERN_COLS], preferred_element_type=F32).astype(BF16)
        for r in range(dil):
            o_ref[r] = res[r * n:(r + 1) * n]


def _inproj(x, g, sc, sh, w, cos_t, sin_t, gq, gkv, wq, wkv, swap, layer, tm=1024):
    b, s, d = x.shape
    d4, d16 = D_PATTERNS[1][1], D_PATTERNS[2][1]
    hw = C_HEADS * LANES
    vw = C_HEADS * C_V_DIM

    def const(shape):
        return pl.BlockSpec(shape, lambda bi, i: (0,) * len(shape))

    def rows(width):
        return pl.BlockSpec((None, tm, width), lambda bi, i: (bi, i, 0))

    def per_batch():
        return pl.BlockSpec((None, 1, d), lambda bi, i: (bi, 0, 0))

    return pl.pallas_call(
        functools.partial(_inproj_body, tm=tm),
        grid=(b, s // tm),
        in_specs=[rows(d), const(g.shape), per_batch(), per_batch(),
                  _layer_spec(w, layer, pipeline_mode=pl.Buffered(1)),
                  rows(LANES), rows(LANES), const(gq.shape), const(gkv.shape),
                  _layer_spec(wq, layer), _layer_spec(wkv, layer), const(swap.shape)],
        out_specs=[rows(MAIN_COLS),
                   pl.BlockSpec((None, d4, tm // d4, PATTERN_COLS), lambda bi, i: (bi, 0, i, 0)),
                   pl.BlockSpec((None, d16, tm // d16, PATTERN_COLS), lambda bi, i: (bi, 0, i, 0)),
                   rows(hw), rows(hw), rows(vw)],
        out_shape=[jax.ShapeDtypeStruct((b, s, MAIN_COLS), BF16),
                   jax.ShapeDtypeStruct((b, d4, s // d4, PATTERN_COLS), BF16),
                   jax.ShapeDtypeStruct((b, d16, s // d16, PATTERN_COLS), BF16),
                   jax.ShapeDtypeStruct((b, s, hw), BF16), jax.ShapeDtypeStruct((b, s, hw), BF16),
                   jax.ShapeDtypeStruct((b, s, vw), BF16)],
        scratch_shapes=[pltpu.VMEM((d // LANES, tm, LANES), F32)],
        compiler_params=_cparams(("parallel", "parallel")),
        name="inproj",
    )(x, g, sc, sh, w, cos_t, sin_t, gq, gkv, wq, wkv, swap)


def _t5_bucket(rel):
    half = REL_BUCKETS // 2
    max_exact = half // 2
    n = jnp.abs(rel)
    n_f = jnp.maximum(n, 1).astype(jnp.float32)
    large = max_exact + (jnp.log(n_f / max_exact) / math.log(REL_MAX_DISTANCE / max_exact)
                         * (half - max_exact)).astype(jnp.int32)
    large = jnp.minimum(large, half - 1)
    return jnp.where(rel > 0, half, 0) + jnp.where(n < max_exact, n, large)


def _band_bias(table, radius, stride, sub_q):
    qi = jnp.arange(sub_q)[:, None]
    kj = jnp.arange(sub_q + 2 * radius)[None, :]
    off = kj - radius - qi
    onehot = (_t5_bucket(off * stride)[..., None] == jnp.arange(REL_BUCKETS)).astype(F32)
    bias = jnp.einsum('qkb,bh->hqk', onehot, table.astype(F32), precision=lax.Precision.HIGHEST)
    bias = jnp.where((jnp.abs(off) <= radius)[None], bias, NEG)
    return bias.reshape(-1, bias.shape[-1])


def _band_body(*refs, radius, tq, seq, kv_w, has_sink, has_lse):
    q_ref, kp_ref, kc_ref, kn_ref, vp_ref, vc_ref, vn_ref, bias_ref = refs[:8]
    rest = refs[8:]
    if has_sink:
        sink_ref, rest = rest[0], rest[1:]
    o_ref = rest[0]
    lse_ref = rest[1] if has_lse else None

    i = pl.program_id(2)
    nh = 4
    sub_q = bias_ref.shape[0] // nh
    nk = sub_q + 2 * radius
    per = kv_w // HEAD_DIM
    lane = lax.broadcasted_iota(jnp.int32, (1, kv_w), 1)
    hmask = [(lane >= HEAD_DIM * (h % per)) & (lane < HEAD_DIM * (h % per + 1)) for h in range(nh)]
    scale = HEAD_DIM ** -0.5

    def pick(rows):
        out = rows[(per - 1) * sub_q:per * sub_q]
        for h in range(per - 2, -1, -1):
            out = jnp.where(hmask[h], rows[h * sub_q:(h + 1) * sub_q], out)
        return out

    for rr in range(q_ref.shape[0]):
        kcat = jnp.concatenate([kp_ref[rr], kc_ref[rr], kn_ref[rr]], axis=0)
        vcat = jnp.concatenate([vp_ref[rr], vc_ref[rr], vn_ref[rr]], axis=0)
        for sb in range(tq // sub_q):
            r0 = sb * sub_q
            qs = q_ref[rr, r0:r0 + sub_q, :] * scale
            ks = kcat[r0:r0 + nk]
            vs = vcat[r0:r0 + nk]
            at_edge = r0 - radius < 0 or r0 + sub_q + radius > tq
            kpos = i * tq + (r0 - radius) + lax.broadcasted_iota(jnp.int32, (1, nk), 1)
            inb = (kpos >= 0) & (kpos < seq)
            parts = []
            for h in range(nh):
                src = qs[:, (h // per) * kv_w:(h // per + 1) * kv_w]
                parts.append(jnp.where(hmask[h], src, jnp.zeros_like(src)))
            qst = jnp.concatenate(parts, axis=0)
            sc = lax.dot_general(qst, ks, (((1,), (1,)), ((), ())), preferred_element_type=F32)
            sc = sc + bias_ref[...]
            if at_edge:
                sc = jnp.where(inb, sc, NEG)
            m = jnp.max(sc, axis=-1, keepdims=True)
            if has_sink:
                m = jnp.maximum(m, jnp.max(sink_ref[...], axis=-1, keepdims=True))
            p = jnp.exp(sc - m)
            den = jnp.sum(p, axis=-1, keepdims=True)
            if has_sink:
                den = den + jnp.max(jnp.exp(sink_ref[...] - m), axis=-1, keepdims=True)
            o = jnp.dot(p.astype(BF16), vs, preferred_element_type=F32) / den
            lse = jnp.broadcast_to(m + jnp.log(den), o.shape) if has_lse else None
            for g in range(nh // per):
                rows = slice(g * per * sub_q, (g + 1) * per * sub_q)
                cols = slice(g * kv_w, (g + 1) * kv_w)
                o_ref[rr, r0:r0 + sub_q, cols] = pick(o[rows]).astype(BF16)
                if has_lse:
                    lse_ref[rr, r0:r0 + sub_q, cols] = pick(lse[rows])


def _band_attention(qkv, bias, sink, *, dil, radius, tq, q_off, k_off, v_off, kv_w, has_lse, group=1):
    b = qkv.shape[0]
    seq = qkv.shape[-2]
    s = seq * dil
    tq = min(tq, seq)
    qw = 2 * LANES
    steps = tq // radius
    last = seq // radius - 1
    if dil == 1:
        qkv = qkv.reshape(b, 1, seq, qkv.shape[-1])

    def spec(rows, width, off, row_index):
        return pl.BlockSpec((None, group, rows, width), lambda bi, r, i: (bi, r, row_index(i), off // width))

    def cur(i):
        return i

    def prev(i):
        return jnp.maximum(i * steps - 1, 0)

    def nxt(i):
        return jnp.minimum((i + 1) * steps, last)

    in_specs = [spec(tq, qw, q_off, cur)]
    for off in (k_off, v_off):
        in_specs += [spec(radius, kv_w, off, prev), spec(tq, kv_w, off, cur), spec(radius, kv_w, off, nxt)]
    in_specs.append(pl.BlockSpec(bias.shape, lambda bi, r, i: (0, 0)))
    args = [qkv] * 7 + [bias]
    if sink is not None:
        in_specs.append(pl.BlockSpec(sink.shape, lambda bi, r, i: (0, 0)))
        args.append(sink)

    o_spec = pl.BlockSpec((None, group, tq, qw), lambda bi, r, i: (bi, r, i, 0))
    out_shape = [jax.ShapeDtypeStruct((b, dil, seq, qw), BF16)]
    out_specs = [o_spec]
    if has_lse:
        out_shape.append(jax.ShapeDtypeStruct((b, dil, seq, qw), F32))
        out_specs.append(o_spec)

    outs = pl.pallas_call(
        functools.partial(_band_body, radius=radius, tq=tq, seq=seq, kv_w=kv_w,
                          has_sink=sink is not None, has_lse=has_lse),
        grid=(b, dil // group, seq // tq),
        in_specs=in_specs,
        out_specs=out_specs,
        out_shape=out_shape,
        compiler_params=_cparams(("parallel", "parallel", "parallel")),
        name=f"band_attention_d{dil}_r{radius}",
    )(*args)
    return outs


def _mla_attn_body(q_ref, k_ref, v_ref, o_ref, *, rows):
    lo = lax.broadcasted_iota(jnp.int32, (1, LANES), 1) < C_V_DIM
    for r0 in range(0, q_ref.shape[0], rows):
        for g in range(C_HEADS // 2):
            acc = None
            for half in range(2):
                h = 2 * g + half
                q = q_ref[r0:r0 + rows, h * LANES:(h + 1) * LANES]
                k = k_ref[:, h * LANES:(h + 1) * LANES]
                sc = lax.dot_general(q, k, (((1,), (1,)), ((), ())), preferred_element_type=F32)
                m = jnp.max(sc, axis=-1, keepdims=True)
                p = jnp.exp(sc - m)
                den = jnp.sum(p, axis=-1, keepdims=True)
                o = jnp.dot(p.astype(BF16), v_ref[:, g * LANES:(g + 1) * LANES],
                            preferred_element_type=F32) / den
                acc = o if acc is None else jnp.where(lo, acc, o)
            o_ref[r0:r0 + rows, g * LANES:(g + 1) * LANES] = acc.astype(BF16)


def _mla_attn(q, k, v, tq=1024, rows=256):
    b, s, hw = q.shape
    vw = v.shape[-1]
    return pl.pallas_call(
        functools.partial(_mla_attn_body, rows=rows),
        grid=(b, s // tq),
        in_specs=[pl.BlockSpec((None, tq, hw), lambda bi, i: (bi, i, 0)),
                  pl.BlockSpec((None, s, hw), lambda bi, i: (bi, 0, 0)),
                  pl.BlockSpec((None, s, vw), lambda bi, i: (bi, 0, 0))],
        out_specs=pl.BlockSpec((None, tq, vw), lambda bi, i: (bi, i, 0)),
        out_shape=jax.ShapeDtypeStruct((b, s, vw), BF16),
        compiler_params=_cparams(("parallel", "arbitrary")),
        name="mla_attention",
    )(q, k, v)


def _outproj_body(x_ref, oa_ref, bb_ref, bc_ref, bh_ref, bcp_ref, bhp_ref, bcn_ref, bhn_ref, oc_ref,
                  o0_ref, o1_ref, o2_ref, l0_ref, l1_ref, l2_ref, w_ref, cw_ref, gn_ref, gate_ref,
                  out_ref, conv_ref, *perm_refs, tm):
    i = pl.program_id(1)

    def natural(ref, s_ref):
        dil, n, width = ref.shape
        for r in range(dil):
            blk = ref[r].astype(F32)
            for c in range(width // LANES):
                s_ref[c, pl.ds(r, n, stride=dil), :] = blk[:, c * LANES:(c + 1) * LANES]
        return jnp.concatenate([s_ref[c] for c in range(width // LANES)], axis=1)

    pad = 8
    first = (i > 0).astype(F32)
    final = (i < pl.num_programs(1) - 1).astype(F32)
    conv_ref[pad:pad + tm, :] = bc_ref[...].astype(F32) * bh_ref[...].astype(F32)
    conv_ref[pad - 1:pad, :] = (bcp_ref[BF16_ROWS - 1:, :].astype(F32)
                                * bhp_ref[BF16_ROWS - 1:, :].astype(F32)) * first
    conv_ref[pad + tm:pad + tm + 1, :] = (bcn_ref[:1, :].astype(F32) * bhn_ref[:1, :].astype(F32)) * final
    cw = cw_ref[...]
    conv = (conv_ref[pad - 1:pad - 1 + tm, :] * cw[0:1] + conv_ref[pad:pad + tm, :] * cw[1:2]
            + conv_ref[pad + 1:pad + 1 + tm, :] * cw[2:3])
    ob = (bb_ref[...].astype(F32) * conv).astype(BF16)
    l0 = l0_ref[0]
    l1, l2 = natural(l1_ref, perm_refs[0]), natural(l2_ref, perm_refs[1])
    o1, o2 = natural(o1_ref, perm_refs[2]), natural(o2_ref, perm_refs[3])
    mx = jnp.maximum(jnp.maximum(l0, l1), l2)
    e0, e1, e2 = jnp.exp(l0 - mx), jnp.exp(l1 - mx), jnp.exp(l2 - mx)
    od = ((o0_ref[0].astype(F32) * e0 + o1 * e1 + o2 * e2) / (e0 + e1 + e2)).astype(BF16)
    wd = 2 * LANES
    y = jnp.dot(oa_ref[...], w_ref[0:wd, :], preferred_element_type=F32)
    y = y + jnp.dot(ob, w_ref[wd:2 * wd, :], preferred_element_type=F32)
    y = y + jnp.dot(oc_ref[...], w_ref[2 * wd:3 * wd, :], preferred_element_type=F32)
    y = y + jnp.dot(od, w_ref[3 * wd:4 * wd, :], preferred_element_type=F32)
    out_ref[...] = x_ref[...] + gate_ref[...] * _rms(y, gn_ref[...])


def _outproj(x, oa, qkv, oc, od, lse, w, cw, gn, gate, layer, tm=1024):
    b, s, d = x.shape
    wd = 2 * LANES
    hb = tm // BF16_ROWS
    last = s // BF16_ROWS - 1

    def rows(width, off=0):
        return pl.BlockSpec((None, tm, width), lambda bi, i: (bi, i, off // width))

    def halo_prev(off):
        return pl.BlockSpec((None, BF16_ROWS, wd), lambda bi, i: (bi, jnp.maximum(i * hb - 1, 0), off // wd))

    def halo_next(off):
        return pl.BlockSpec((None, BF16_ROWS, wd), lambda bi, i: (bi, jnp.minimum((i + 1) * hb, last), off // wd))

    def const(shape):
        return pl.BlockSpec(shape, lambda bi, i: (0,) * len(shape))

    def grouped(arr):
        dil = arr.shape[1]
        return pl.BlockSpec((None, dil, tm // dil, wd), lambda bi, i: (bi, 0, i, 0))

    return pl.pallas_call(
        functools.partial(_outproj_body, tm=tm),
        grid=(b, s // tm),
        in_specs=[rows(d), rows(wd), rows(wd, OFF_BB), rows(wd, OFF_BC), rows(wd, OFF_BH),
                  halo_prev(OFF_BC), halo_prev(OFF_BH), halo_next(OFF_BC), halo_next(OFF_BH),
                  rows(wd)] + [grouped(a) for a in od] + [grouped(a) for a in lse] + [
                  _layer_spec(w, layer), _layer_spec(cw, layer), const(gn.shape),
                  pl.BlockSpec((None, 1, d), lambda bi, i: (bi, 0, 0))],
        out_specs=rows(d),
        out_shape=jax.ShapeDtypeStruct((b, s, d), F32),
        scratch_shapes=[pltpu.VMEM((tm + 16, wd), F32)] + [pltpu.VMEM((wd // LANES, tm, LANES), F32)] * 4,
        compiler_params=_cparams(("parallel", "arbitrary")),
        name="outproj",
    )(x, oa, qkv, qkv, qkv, qkv, qkv, qkv, qkv, oc, od[0], od[1], od[2], lse[0], lse[1], lse[2],
      w, cw, gn, gate)


def _ffn_body(x_ref, xp_ref, xn_ref, g_ref, sc_ref, sh_ref, gate_ref, gy_ref, wup_ref, cw_ref, wd_ref, out_ref,
              hf_ref, u_ref, a_ref, *, tm, tn):
    i = pl.program_id(1)
    ff = wd_ref.shape[0]
    d = x_ref.shape[-1]
    nc = d // LANES
    sub = 8
    seg = tm // sub

    def mod(v):
        return _rms(v, g_ref[...]) * (1.0 + sc_ref[...]) + sh_ref[...]

    def to_tile_order(ref):
        return jnp.concatenate(
            [jnp.concatenate([ref[c, pl.ds(a, sub, stride=seg), :] for a in range(seg)], axis=0)
             for c in range(nc)], axis=1)

    def to_seq_order(ref):
        return jnp.concatenate(
            [jnp.concatenate([ref[c, pl.ds(b, seg, stride=sub), :] for b in range(sub)], axis=0)
             for c in range(nc)], axis=1)

    hm = mod(x_ref[...])
    for c in range(nc):
        hf_ref[c] = hm[:, c * LANES:(c + 1) * LANES]
    first = (i > 0).astype(F32)
    final = (i < pl.num_programs(1) - 1).astype(F32)
    edge = jnp.concatenate([mod(xp_ref[sub:, :]) * first, mod(xn_ref[:sub, :]) * final], axis=0)
    h = jnp.concatenate([to_tile_order(hf_ref), edge], axis=0).astype(BF16)

    row = lax.broadcasted_iota(jnp.int32, (sub, 1), 0)

    def conv(slot, half, c0):
        lanes = slice(half * tn, (half + 1) * tn)
        cw = cw_ref[:, c0:c0 + tn]
        mid = u_ref[slot, 0:tm, lanes]
        top = jnp.where(row == 0, pltpu.roll(u_ref[slot, tm:tm + sub, lanes], 1, axis=0),
                        pltpu.roll(u_ref[slot, tm - sub:tm, lanes], 1, axis=0))
        prev = jnp.concatenate([top, u_ref[slot, 0:tm - sub, lanes]], axis=0)
        bot = jnp.where(row == sub - 1, pltpu.roll(u_ref[slot, tm + sub:tm + 2 * sub, lanes], sub - 1, axis=0),
                        pltpu.roll(u_ref[slot, 0:sub, lanes], sub - 1, axis=0))
        nxt = jnp.concatenate([u_ref[slot, sub:tm, lanes], bot], axis=0)
        return prev * cw[0:1] + mid * cw[1:2] + nxt * cw[2:3]

    for j in range(ff // tn):
        slot = j % 2
        c0 = j * tn
        u_ref[slot, :, 0:tn] = jnp.dot(h, wup_ref[:, c0:c0 + tn], preferred_element_type=F32)
        u_ref[slot, :, tn:2 * tn] = jnp.dot(h, wup_ref[:, ff + c0:ff + c0 + tn], preferred_element_type=F32)
        a = jax.nn.gelu(conv(slot, 0, c0), approximate=True) * conv(slot, 1, ff + c0)
        a_ref[:, c0:c0 + tn] = a.astype(BF16)

    y = jnp.dot(a_ref[...], wd_ref[...], preferred_element_type=F32)
    for c in range(nc):
        hf_ref[c] = y[:, c * LANES:(c + 1) * LANES]
    out_ref[...] = x_ref[...] + gate_ref[...] * _rms(to_seq_order(hf_ref), gy_ref[...])


def _ffn(x, g, sc, sh, gate, gy, w_up, cw, w_down, layer, tm=512, tn=256):
    b, s, d = x.shape
    ff = w_down.shape[1]
    hb = tm // BF16_ROWS
    last = s // BF16_ROWS - 1

    def const(shape, **kw):
        return pl.BlockSpec(shape, lambda bi, i: (0,) * len(shape), **kw)

    def per_batch():
        return pl.BlockSpec((None, 1, d), lambda bi, i: (bi, 0, 0))

    resident = dict(pipeline_mode=pl.Buffered(1))
    return pl.pallas_call(
        functools.partial(_ffn_body, tm=tm, tn=tn),
        grid=(b, s // tm),
        in_specs=[pl.BlockSpec((None, tm, d), lambda bi, i: (bi, i, 0)),
                  pl.BlockSpec((None, BF16_ROWS, d), lambda bi, i: (bi, jnp.maximum(i * hb - 1, 0), 0)),
                  pl.BlockSpec((None, BF16_ROWS, d), lambda bi, i: (bi, jnp.minimum((i + 1) * hb, last), 0)),
                  const(g.shape), per_batch(), per_batch(), per_batch(), const(gy.shape),
                  _layer_spec(w_up, layer, **resident), _layer_spec(cw, layer),
                  _layer_spec(w_down, layer, **resident)],
        out_specs=pl.BlockSpec((None, tm, d), lambda bi, i: (bi, i, 0)),
        out_shape=jax.ShapeDtypeStruct((b, s, d), F32),
        scratch_shapes=[pltpu.VMEM((d // LANES, tm, LANES), F32),
                        pltpu.VMEM((2, tm + BF16_ROWS, 2 * tn), F32),
                        pltpu.VMEM((tm, ff), BF16)],
        compiler_params=_cparams(("parallel", "arbitrary")),
        name="ffn",
    )(x, x, x, g, sc, sh, gate, gy, w_up, cw, w_down)


def _pack_w_in(w_in):
    nl, d, _ = w_in.shape
    hd = HEAD_DIM
    aq = w_in[..., 0:256]
    aq = jnp.concatenate([aq[..., 0:hd], aq[..., 2 * hd:3 * hd], aq[..., hd:2 * hd], aq[..., 3 * hd:]], axis=-1)
    ckr = w_in[..., 1664:1696]
    ckr = jnp.concatenate([jnp.zeros((nl, d, C_NOPE_DIM), w_in.dtype), ckr,
                           jnp.zeros((nl, d, LANES - C_NOPE_DIM - C_ROPE_DIM), w_in.dtype)], axis=-1)
    d0 = 1696
    return jnp.concatenate([aq, w_in[..., 256:1280], w_in[..., d0:d0 + PATTERN_COLS], w_in[..., 1280:1664], ckr,
                            w_in[..., d0 + PATTERN_COLS:]], axis=-1).astype(BF16)


def _pack_w_uq(w):
    nl, r, _ = w.shape
    half = C_ROPE_DIM // 2
    w = w.reshape(nl, r, C_HEADS, C_NOPE_DIM + C_ROPE_DIM)
    nope, r1, r2 = w[..., :C_NOPE_DIM], w[..., C_NOPE_DIM:C_NOPE_DIM + half], w[..., C_NOPE_DIM + half:]
    z_tail = jnp.zeros((nl, r, C_HEADS, LANES - C_NOPE_DIM - C_ROPE_DIM), w.dtype)
    plain = jnp.concatenate([nope, r1, r2, z_tail], axis=-1).reshape(nl, r, C_HEADS * LANES)
    swapped = jnp.concatenate([jnp.zeros_like(nope), r2, r1, z_tail], axis=-1).reshape(nl, r, C_HEADS * LANES)
    return jnp.concatenate([plain, swapped], axis=-1).astype(BF16)


def _pack_w_ukv(w):
    nl, r, _ = w.shape
    w = w.reshape(nl, r, C_HEADS, C_NOPE_DIM + C_V_DIM)
    kn = jnp.concatenate([w[..., :C_NOPE_DIM], jnp.zeros((nl, r, C_HEADS, LANES - C_NOPE_DIM), w.dtype)], axis=-1)
    return jnp.concatenate([kn.reshape(nl, r, C_HEADS * LANES),
                            w[..., C_NOPE_DIM:].reshape(nl, r, C_HEADS * C_V_DIM)], axis=-1).astype(BF16)


def _pack_w_out(w_out):
    hd = HEAD_DIM
    return jnp.concatenate([w_out[:, 0:hd], w_out[:, 2 * hd:3 * hd], w_out[:, hd:2 * hd], w_out[:, 3 * hd:]],
                           axis=1).astype(BF16)


def _rope_tables(positions):
    half = C_ROPE_DIM // 2
    inv_freq = ROPE_THETA ** (-jnp.arange(half, dtype=jnp.float32) / half)
    ang = positions.astype(jnp.float32)[..., None] * inv_freq
    dense = ang.reshape(-1, LANES)
    cos, sin = jnp.cos(dense).reshape(ang.shape), jnp.sin(dense).reshape(ang.shape)
    lead = positions.shape + (C_NOPE_DIM,)
    tail = positions.shape + (LANES - C_NOPE_DIM - C_ROPE_DIM,)
    cos_t = jnp.concatenate([jnp.ones(lead, F32), cos, cos, jnp.zeros(tail, F32)], axis=-1)
    sin_t = jnp.concatenate([jnp.zeros(lead, F32), -sin, sin, jnp.zeros(tail, F32)], axis=-1)
    return cos_t, sin_t


def _swap_matrix():
    half = C_ROPE_DIM // 2
    idx = jnp.arange(LANES)
    src = jnp.where((idx >= C_NOPE_DIM) & (idx < C_NOPE_DIM + half), idx + half,
                    jnp.where((idx >= C_NOPE_DIM + half) & (idx < C_NOPE_DIM + C_ROPE_DIM), idx - half, idx))
    return (idx[:, None] == src[None, :]).astype(BF16)


def kernel(x, c, positions, rel_bias, w_mod, b_mod, norm_g, w_in, a_sink, b_conv, c_norm_q, c_norm_kv,
           c_w_uq, c_w_ukv, w_out, w_up, ffn_conv, w_down):
    b, s, d = x.shape
    depth = w_in.shape[0]

    a_order = jnp.array(A_HEAD_ORDER)
    bias_a = _band_bias(rel_bias[:, :A_Q_HEADS][:, a_order], A_RADIUS, 1, A_SUB_Q)
    bias_d = [_band_bias(rel_bias[:, A_Q_HEADS + i * D_HEADS:A_Q_HEADS + (i + 1) * D_HEADS], (w // 2) // dil, dil,
                         D_SUB_Q)
              for i, (w, dil) in enumerate(D_PATTERNS)]
    cos_t, sin_t = _rope_tables(positions)
    swap = _swap_matrix()

    w_in_p = _pack_w_in(w_in)
    w_uq_p = _pack_w_uq(c_w_uq)
    w_ukv_p = _pack_w_ukv(c_w_ukv)
    w_out_p = _pack_w_out(w_out)
    w_up_b = w_up.astype(BF16)
    w_down_b = w_down.astype(BF16)

    mod = _modulation(c, w_mod, b_mod)

    for l in range(depth):
        sh1, sc1, g1, sh2, sc2, g2 = (mod[l, k].reshape(b, 1, d) for k in range(6))
        gains = norm_g[l]

        qkv, qkv_d4, qkv_d16, qc, kc, vc = _inproj(x, gains[0:1], sc1, sh1, w_in_p, cos_t, sin_t, c_norm_q[l][None],
                                                   c_norm_kv[l][None], w_uq_p, w_ukv_p, swap, l)

        sink = jnp.broadcast_to(jnp.repeat(a_sink[l][a_order], A_SUB_Q)[:, None], (A_Q_HEADS * A_SUB_Q, LANES))
        oa = _band_attention(qkv, bias_a, sink, dil=1, radius=A_RADIUS, tq=BAND_ROWS, q_off=OFF_AQ, k_off=OFF_AK,
                             v_off=OFF_AV, kv_w=LANES, has_lse=False)[0].reshape(b, s, 2 * LANES)

        oc = _mla_attn(qc, kc, vc)

        od, lse = [], []
        for i, ((w, dil), src) in enumerate(zip(D_PATTERNS, (qkv, qkv_d4, qkv_d16))):
            off = OFF_D if dil == 1 else 0
            seq = s // dil
            o_i, l_i = _band_attention(src, bias_d[i], None, dil=dil, radius=(w // 2) // dil, tq=BAND_ROWS, q_off=off,
                                       k_off=off + 256, v_off=off + 512, kv_w=2 * LANES, has_lse=True,
                                       group=max(1, BAND_ROWS // seq))
            od.append(o_i)
            lse.append(l_i)

        x = _outproj(x, oa, qkv, oc, od, lse, w_out_p, b_conv, gains[1:2], g1, l)
        x = _ffn(x, gains[2:3], sc2, sh2, g2, gains[3:4], w_up_b, ffn_conv, w_down_b, l)
    return x
```

```python
import functools
import math

import jax
import jax.numpy as jnp
from jax import lax
from jax.experimental import pallas as pl
from jax.experimental.pallas import tpu as pltpu

D_MODEL = 1024
HEAD_DIM = 64
A_Q_HEADS = 4
A_KV_HEADS = 2
A_RADIUS = 128
B_WIDTH = 256
C_HEADS = 4
C_Q_RANK = 256
C_KV_RANK = 128
C_NOPE_DIM = 64
C_ROPE_DIM = 32
C_V_DIM = 64
ROPE_THETA = 10000.0
D_HEADS = 4
D_PATTERNS = ((128, 1), (512, 4), (2048, 16))
REL_BUCKETS = 32
REL_MAX_DISTANCE = 1024
D_FF = 2816
EPS = 1e-6
NEG = -1e30

LANES = 128
BF16_ROWS = 16
A_SUB_Q = 128
D_SUB_Q = 128
BAND_ROWS = 4096
VMEM_LIMIT = 48 * 1024 * 1024

OFF_AQ, OFF_AK, OFF_AV = 0, 256, 384
OFF_BB, OFF_BC, OFF_BH = 512, 768, 1024
OFF_D = 1280
A_HEAD_ORDER = (0, 2, 1, 3)
PATTERN_COLS = 3 * D_HEADS * HEAD_DIM
MAIN_COLS = OFF_D + PATTERN_COLS
OFF_C = MAIN_COLS
C_COLS = C_Q_RANK + C_KV_RANK + LANES
OFF_D4 = OFF_C + C_COLS
COL_CHUNK = 512

F32 = jnp.float32
BF16 = jnp.bfloat16


def _layer_spec(w, layer, **kw):
    return pl.BlockSpec((None,) + w.shape[1:], lambda *_: (layer,) + (0,) * (w.ndim - 1), **kw)


def _cparams(sem):
    return pltpu.CompilerParams(dimension_semantics=sem, vmem_limit_bytes=VMEM_LIMIT)


def _rms(v, g):
    return v * lax.rsqrt(jnp.mean(v * v, axis=-1, keepdims=True) + EPS) * g


def _mod_body(c_ref, w_ref, b_ref, o_ref):
    c = c_ref[...]
    act = c / (1.0 + jnp.exp(-c))
    o_ref[...] = jnp.dot(act, w_ref[...], preferred_element_type=F32,
                         precision=lax.Precision.HIGHEST) + b_ref[...]


def _modulation(c, w_mod, b_mod):
    nl, d, _ = w_mod.shape
    b = c.shape[0]
    return pl.pallas_call(
        _mod_body,
        grid=(nl, 6),
        in_specs=[pl.BlockSpec((b, d), lambda l, k: (0, 0)),
                  pl.BlockSpec((None, d, d), lambda l, k: (l, 0, k)),
                  pl.BlockSpec((None, 1, d), lambda l, k: (l, 0, k))],
        out_specs=pl.BlockSpec((None, None, b, d), lambda l, k: (l, k, 0, 0)),
        out_shape=jax.ShapeDtypeStruct((nl, 6, b, d), F32),
        compiler_params=_cparams(("parallel", "parallel")),
        name="modulation",
    )(c, w_mod, b_mod.reshape(nl, 1, 6 * d))


def _latent_qkv(cc, cs, sn, gq_ref, gkv_ref, wq_ref, wkv_ref, swap_ref, q_ref, k_ref, v_ref):
    scale = (C_NOPE_DIM + C_ROPE_DIM) ** -0.5
    nq = _rms(cc[:, :C_Q_RANK], gq_ref[...]).astype(BF16)
    qq = jnp.dot(nq, wq_ref[...], preferred_element_type=F32)
    nkv = _rms(cc[:, C_Q_RANK:C_Q_RANK + C_KV_RANK], gkv_ref[...]).astype(BF16)
    kvv = jnp.dot(nkv, wkv_ref[...], preferred_element_type=F32)
    kr = cc[:, C_Q_RANK + C_KV_RANK:].astype(BF16)
    kr_sw = jnp.dot(kr, swap_ref[...], preferred_element_type=F32)
    k_rope = kr.astype(F32) * cs + kr_sw * sn
    hw = C_HEADS * LANES
    for h in range(C_HEADS):
        a = qq[:, h * LANES:(h + 1) * LANES]
        a_sw = qq[:, hw + h * LANES:hw + (h + 1) * LANES]
        q_ref[:, h * LANES:(h + 1) * LANES] = ((a * cs + a_sw * sn) * scale).astype(BF16)
        k_ref[:, h * LANES:(h + 1) * LANES] = (kvv[:, h * LANES:(h + 1) * LANES] + k_rope).astype(BF16)
    v_ref[...] = kvv[:, hw:].astype(BF16)


def _inproj_body(x_ref, g_ref, sc_ref, sh_ref, w_ref, cos_ref, sin_ref, gq_ref, gkv_ref, wq_ref, wkv_ref, swap_ref,
                 om_ref, o4_ref, o16_ref, q_ref, k_ref, v_ref, hf_ref, *, tm):
    h = _rms(x_ref[...], g_ref[...]) * (1.0 + sc_ref[...]) + sh_ref[...]
    nc = h.shape[-1] // LANES
    for c in range(nc):
        hf_ref[c] = h[:, c * LANES:(c + 1) * LANES]
    hb = h.astype(BF16)
    for c0 in range(0, MAIN_COLS, COL_CHUNK):
        om_ref[:, c0:c0 + COL_CHUNK] = jnp.dot(hb, w_ref[:, c0:c0 + COL_CHUNK],
                                               preferred_element_type=F32).astype(BF16)
    cc = jnp.dot(hb, w_ref[:, OFF_C:OFF_C + C_COLS], preferred_element_type=F32)
    _latent_qkv(cc, cos_ref[...], sin_ref[...], gq_ref, gkv_ref, wq_ref, wkv_ref, swap_ref, q_ref, k_ref, v_ref)
    for o_ref, dil, c0 in ((o4_ref, 4, OFF_D4), (o16_ref, 16, OFF_D4 + PATTERN_COLS)):
        n = tm // dil
        hp = jnp.concatenate(
            [jnp.concatenate([hf_ref[c, pl.ds(r, n, stride=dil), :] for r in range(dil)], axis=0)
             for c in range(nc)], axis=1).astype(BF16)
        res = jnp.dot(hp, w_ref[:, c0:c0 + PATTERN_COLS], preferred_element_type=F32).astype(BF16)
        for r in range(dil):
            o_ref[r] = res[r * n:(r + 1) * n]


def _inproj(x, g, sc, sh, w, cos_t, sin_t, gq, gkv, wq, wkv, swap, layer, tm=1024):
    b, s, d = x.shape
    d4, d16 = D_PATTERNS[1][1], D_PATTERNS[2][1]
    hw = C_HEADS * LANES
    vw = C_HEADS * C_V_DIM

    def const(shape):
        return pl.BlockSpec(shape, lambda bi, i: (0,) * len(shape))

    def rows(width):
        return pl.BlockSpec((None, tm, width), lambda bi, i: (bi, i, 0))

    def per_batch():
        return pl.BlockSpec((None, 1, d), lambda bi, i: (bi, 0, 0))

    return pl.pallas_call(
        functools.partial(_inproj_body, tm=tm),
        grid=(b, s // tm),
        in_specs=[rows(d), const(g.shape), per_batch(), per_batch(),
                  _layer_spec(w, layer, pipeline_mode=pl.Buffered(1)),
                  rows(LANES), rows(LANES), const(gq.shape), const(gkv.shape),
                  _layer_spec(wq, layer), _layer_spec(wkv, layer), const(swap.shape)],
        out_specs=[rows(MAIN_COLS),
                   pl.BlockSpec((None, d4, tm // d4, PATTERN_COLS), lambda bi, i: (bi, 0, i, 0)),
                   pl.BlockSpec((None, d16, tm // d16, PATTERN_COLS), lambda bi, i: (bi, 0, i, 0)),
                   rows(hw), rows(hw), rows(vw)],
        out_shape=[jax.ShapeDtypeStruct((b, s, MAIN_COLS), BF16),
                   jax.ShapeDtypeStruct((b, d4, s // d4, PATTERN_COLS), BF16),
                   jax.ShapeDtypeStruct((b, d16, s // d16, PATTERN_COLS), BF16),
                   jax.ShapeDtypeStruct((b, s, hw), BF16), jax.ShapeDtypeStruct((b, s, hw), BF16),
                   jax.ShapeDtypeStruct((b, s, vw), BF16)],
        scratch_shapes=[pltpu.VMEM((d // LANES, tm, LANES), F32)],
        compiler_params=_cparams(("parallel", "parallel")),
        name="inproj",
    )(x, g, sc, sh, w, cos_t, sin_t, gq, gkv, wq, wkv, swap)


def _t5_bucket(rel):
    half = REL_BUCKETS // 2
    max_exact = half // 2
    n = jnp.abs(rel)
    n_f = jnp.maximum(n, 1).astype(jnp.float32)
    large = max_exact + (jnp.log(n_f / max_exact) / math.log(REL_MAX_DISTANCE / max_exact)
                         * (half - max_exact)).astype(jnp.int32)
    large = jnp.minimum(large, half - 1)
    return jnp.where(rel > 0, half, 0) + jnp.where(n < max_exact, n, large)


def _band_bias(table, radius, stride, sub_q):
    qi = jnp.arange(sub_q)[:, None]
    kj = jnp.arange(sub_q + 2 * radius)[None, :]
    off = kj - radius - qi
    onehot = (_t5_bucket(off * stride)[..., None] == jnp.arange(REL_BUCKETS)).astype(F32)
    bias = jnp.einsum('qkb,bh->hqk', onehot, table.astype(F32), precision=lax.Precision.HIGHEST)
    bias = jnp.where((jnp.abs(off) <= radius)[None], bias, NEG)
    return bias.reshape(-1, bias.shape[-1])


def _band_body(*refs, radius, tq, seq, kv_w, has_sink, has_lse):
    q_ref, kp_ref, kc_ref, kn_ref, vp_ref, vc_ref, vn_ref, bias_ref = refs[:8]
    rest = refs[8:]
    if has_sink:
        sink_ref, rest = rest[0], rest[1:]
    o_ref = rest[0]
    lse_ref = rest[1] if has_lse else None

    i = pl.program_id(2)
    nh = 4
    sub_q = bias_ref.shape[0] // nh
    nk = sub_q + 2 * radius
    per = kv_w // HEAD_DIM
    lane = lax.broadcasted_iota(jnp.int32, (1, kv_w), 1)
    hmask = [(lane >= HEAD_DIM * (h % per)) & (lane < HEAD_DIM * (h % per + 1)) for h in range(nh)]
    scale = HEAD_DIM ** -0.5

    def pick(rows):
        out = rows[(per - 1) * sub_q:per * sub_q]
        for h in range(per - 2, -1, -1):
            out = jnp.where(hmask[h], rows[h * sub_q:(h + 1) * sub_q], out)
        return out

    for rr in range(q_ref.shape[0]):
        kcat = jnp.concatenate([kp_ref[rr], kc_ref[rr], kn_ref[rr]], axis=0)
        vcat = jnp.concatenate([vp_ref[rr], vc_ref[rr], vn_ref[rr]], axis=0)
        for sb in range(tq // sub_q):
            r0 = sb * sub_q
            qs = q_ref[rr, r0:r0 + sub_q, :] * scale
            ks = kcat[r0:r0 + nk]
            vs = vcat[r0:r0 + nk]
            at_edge = r0 - radius < 0 or r0 + sub_q + radius > tq
            kpos = i * tq + (r0 - radius) + lax.broadcasted_iota(jnp.int32, (1, nk), 1)
            inb = (kpos >= 0) & (kpos < seq)
            parts = []
            for h in range(nh):
                src = qs[:, (h // per) * kv_w:(h // per + 1) * kv_w]
                parts.append(jnp.where(hmask[h], src, jnp.zeros_like(src)))
            qst = jnp.concatenate(parts, axis=0)
            sc = lax.dot_general(qst, ks, (((1,), (1,)), ((), ())), preferred_element_type=F32)
            sc = sc + bias_ref[...]
            if at_edge:
                sc = jnp.where(inb, sc, NEG)
            m = jnp.max(sc, axis=-1, keepdims=True)
            if has_sink:
                m = jnp.maximum(m, jnp.max(sink_ref[...], axis=-1, keepdims=True))
            p = jnp.exp(sc - m)
            den = jnp.sum(p, axis=-1, keepdims=True)
            if has_sink:
                den = den + jnp.max(jnp.exp(sink_ref[...] - m), axis=-1, keepdims=True)
            o = jnp.dot(p.astype(BF16), vs, preferred_element_type=F32) / den
            lse = jnp.broadcast_to(m + jnp.log(den), o.shape) if has_lse else None
            for g in range(nh // per):
                rows = slice(g * per * sub_q, (g + 1) * per * sub_q)
                cols = slice(g * kv_w, (g + 1) * kv_w)
                o_ref[rr, r0:r0 + sub_q, cols] = pick(o[rows]).astype(BF16)
                if has_lse:
                    lse_ref[rr, r0:r0 + sub_q, cols] = pick(lse[rows])


def _band_attention(qkv, bias, sink, *, dil, radius, tq, q_off, k_off, v_off, kv_w, has_lse, group=1):
    b = qkv.shape[0]
    seq = qkv.shape[-2]
    s = seq * dil
    tq = min(tq, seq)
    qw = 2 * LANES
    steps = tq // radius
    last = seq // radius - 1
    if dil == 1:
        qkv = qkv.reshape(b, 1, seq, qkv.shape[-1])

    def spec(rows, width, off, row_index):
        return pl.BlockSpec((None, group, rows, width), lambda bi, r, i: (bi, r, row_index(i), off // width))

    def cur(i):
        return i

    def prev(i):
        return jnp.maximum(i * steps - 1, 0)

    def nxt(i):
        return jnp.minimum((i + 1) * steps, last)

    in_specs = [spec(tq, qw, q_off, cur)]
    for off in (k_off, v_off):
        in_specs += [spec(radius, kv_w, off, prev), spec(tq, kv_w, off, cur), spec(radius, kv_w, off, nxt)]
    in_specs.append(pl.BlockSpec(bias.shape, lambda bi, r, i: (0, 0)))
    args = [qkv] * 7 + [bias]
    if sink is not None:
        in_specs.append(pl.BlockSpec(sink.shape, lambda bi, r, i: (0, 0)))
        args.append(sink)

    o_spec = pl.BlockSpec((None, group, tq, qw), lambda bi, r, i: (bi, r, i, 0))
    out_shape = [jax.ShapeDtypeStruct((b, dil, seq, qw), BF16)]
    out_specs = [o_spec]
    if has_lse:
        out_shape.append(jax.ShapeDtypeStruct((b, dil, seq, qw), F32))
        out_specs.append(o_spec)

    outs = pl.pallas_call(
        functools.partial(_band_body, radius=radius, tq=tq, seq=seq, kv_w=kv_w,
                          has_sink=sink is not None, has_lse=has_lse),
        grid=(b, dil // group, seq // tq),
        in_specs=in_specs,
        out_specs=out_specs,
        out_shape=out_shape,
        compiler_params=_cparams(("parallel", "parallel", "parallel")),
        name=f"band_attention_d{dil}_r{radius}",
    )(*args)
    return outs


def _mla_attn_body(q_ref, k_ref, v_ref, o_ref, *, rows):
    lo = lax.broadcasted_iota(jnp.int32, (1, LANES), 1) < C_V_DIM
    for r0 in range(0, q_ref.shape[0], rows):
        for g in range(C_HEADS // 2):
            acc = None
            for half in range(2):
                h = 2 * g + half
                q = q_ref[r0:r0 + rows, h * LANES:(h + 1) * LANES]
                k = k_ref[:, h * LANES:(h + 1) * LANES]
                sc = lax.dot_general(q, k, (((1,), (1,)), ((), ())), preferred_element_type=F32)
                m = jnp.max(sc, axis=-1, keepdims=True)
                p = jnp.exp(sc - m)
                den = jnp.sum(p, axis=-1, keepdims=True)
                o = jnp.dot(p.astype(BF16), v_ref[:, g * LANES:(g + 1) * LANES],
                            preferred_element_type=F32) / den
                acc = o if acc is None else jnp.where(lo, acc, o)
            o_ref[r0:r0 + rows, g * LANES:(g + 1) * LANES] = acc.astype(BF16)


def _mla_attn(q, k, v, tq=1024, rows=256):
    b, s, hw = q.shape
    vw = v.shape[-1]
    return pl.pallas_call(
        functools.partial(_mla_attn_body, rows=rows),
        grid=(b, s // tq),
        in_specs=[pl.BlockSpec((None, tq, hw), lambda bi, i: (bi, i, 0)),
                  pl.BlockSpec((None, s, hw), lambda bi, i: (bi, 0, 0)),
                  pl.BlockSpec((None, s, vw), lambda bi, i: (bi, 0, 0))],
        out_specs=pl.BlockSpec((None, tq, vw), lambda bi, i: (bi, i, 0)),
        out_shape=jax.ShapeDtypeStruct((b, s, vw), BF16),
        compiler_params=_cparams(("parallel", "arbitrary")),
        name="mla_attention",
    )(q, k, v)


def _outproj_body(x_ref, oa_ref, bb_ref, bc_ref, bh_ref, bcp_ref, bhp_ref, bcn_ref, bhn_ref, oc_ref,
                  o0_ref, o1_ref, o2_ref, l0_ref, l1_ref, l2_ref, w_ref, cw_ref, gn_ref, gate_ref,
                  out_ref, conv_ref, *perm_refs, tm):
    i = pl.program_id(1)

    def natural(ref, s_ref):
        dil, n, width = ref.shape
        for r in range(dil):
            blk = ref[r].astype(F32)
            for c in range(width // LANES):
                s_ref[c, pl.ds(r, n, stride=dil), :] = blk[:, c * LANES:(c + 1) * LANES]
        return jnp.concatenate([s_ref[c] for c in range(width // LANES)], axis=1)

    pad = 8
    first = (i > 0).astype(F32)
    final = (i < pl.num_programs(1) - 1).astype(F32)
    conv_ref[pad:pad + tm, :] = bc_ref[...].astype(F32) * bh_ref[...].astype(F32)
    conv_ref[pad - 1:pad, :] = (bcp_ref[BF16_ROWS - 1:, :].astype(F32)
                                * bhp_ref[BF16_ROWS - 1:, :].astype(F32)) * first
    conv_ref[pad + tm:pad + tm + 1, :] = (bcn_ref[:1, :].astype(F32) * bhn_ref[:1, :].astype(F32)) * final
    cw = cw_ref[...]
    conv = (conv_ref[pad - 1:pad - 1 + tm, :] * cw[0:1] + conv_ref[pad:pad + tm, :] * cw[1:2]
            + conv_ref[pad + 1:pad + 1 + tm, :] * cw[2:3])
    ob = (bb_ref[...].astype(F32) * conv).astype(BF16)
    l0 = l0_ref[0]
    l1, l2 = natural(l1_ref, perm_refs[0]), natural(l2_ref, perm_refs[1])
    o1, o2 = natural(o1_ref, perm_refs[2]), natural(o2_ref, perm_refs[3])
    mx = jnp.maximum(jnp.maximum(l0, l1), l2)
    e0, e1, e2 = jnp.exp(l0 - mx), jnp.exp(l1 - mx), jnp.exp(l2 - mx)
    od = ((o0_ref[0].astype(F32) * e0 + o1 * e1 + o2 * e2) / (e0 + e1 + e2)).astype(BF16)
    wd = 2 * LANES
    y = jnp.dot(oa_ref[...], w_ref[0:wd, :], preferred_element_type=F32)
    y = y + jnp.dot(ob, w_ref[wd:2 * wd, :], preferred_element_type=F32)
    y = y + jnp.dot(oc_ref[...], w_ref[2 * wd:3 * wd, :], preferred_element_type=F32)
    y = y + jnp.dot(od, w_ref[3 * wd:4 * wd, :], preferred_element_type=F32)
    out_ref[...] = x_ref[...] + gate_ref[...] * _rms(y, gn_ref[...])


def _outproj(x, oa, qkv, oc, od, lse, w, cw, gn, gate, layer, tm=1024):
    b, s, d = x.shape
    wd = 2 * LANES
    hb = tm // BF16_ROWS
    last = s // BF16_ROWS - 1

    def rows(width, off=0):
        return pl.BlockSpec((None, tm, width), lambda bi, i: (bi, i, off // width))

    def halo_prev(off):
        return pl.BlockSpec((None, BF16_ROWS, wd), lambda bi, i: (bi, jnp.maximum(i * hb - 1, 0), off // wd))

    def halo_next(off):
        return pl.BlockSpec((None, BF16_ROWS, wd), lambda bi, i: (bi, jnp.minimum((i + 1) * hb, last), off // wd))

    def const(shape):
        return pl.BlockSpec(shape, lambda bi, i: (0,) * len(shape))

    def grouped(arr):
        dil = arr.shape[1]
        return pl.BlockSpec((None, dil, tm // dil, wd), lambda bi, i: (bi, 0, i, 0))

    return pl.pallas_call(
        functools.partial(_outproj_body, tm=tm),
        grid=(b, s // tm),
        in_specs=[rows(d), rows(wd), rows(wd, OFF_BB), rows(wd, OFF_BC), rows(wd, OFF_BH),
                  halo_prev(OFF_BC), halo_prev(OFF_BH), halo_next(OFF_BC), halo_next(OFF_BH),
                  rows(wd)] + [grouped(a) for a in od] + [grouped(a) for a in lse] + [
                  _layer_spec(w, layer), _layer_spec(cw, layer), const(gn.shape),
                  pl.BlockSpec((None, 1, d), lambda bi, i: (bi, 0, 0))],
        out_specs=rows(d),
        out_shape=jax.ShapeDtypeStruct((b, s, d), F32),
        scratch_shapes=[pltpu.VMEM((tm + 16, wd), F32)] + [pltpu.VMEM((wd // LANES, tm, LANES), F32)] * 4,
        compiler_params=_cparams(("parallel", "arbitrary")),
        name="outproj",
    )(x, oa, qkv, qkv, qkv, qkv, qkv, qkv, qkv, oc, od[0], od[1], od[2], lse[0], lse[1], lse[2],
      w, cw, gn, gate)


def _ffn_body(x_ref, xp_ref, xn_ref, g_ref, sc_ref, sh_ref, gate_ref, gy_ref, wup_ref, cw_ref, wd_ref, out_ref,
              hf_ref, u_ref, a_ref, *, tm, tn):
    i = pl.program_id(1)
    ff = wd_ref.shape[0]
    d = x_ref.shape[-1]
    nc = d // LANES
    sub = 8
    seg = tm // sub

    def mod(v):
        return _rms(v, g_ref[...]) * (1.0 + sc_ref[...]) + sh_ref[...]

    def to_tile_order(ref):
        return jnp.concatenate(
            [jnp.concatenate([ref[c, pl.ds(a, sub, stride=seg), :] for a in range(seg)], axis=0)
             for c in range(nc)], axis=1)

    def to_seq_order(ref):
        return jnp.concatenate(
            [jnp.concatenate([ref[c, pl.ds(b, seg, stride=sub), :] for b in range(sub)], axis=0)
             for c in range(nc)], axis=1)

    hm = mod(x_ref[...])
    for c in range(nc):
        hf_ref[c] = hm[:, c * LANES:(c + 1) * LANES]
    first = (i > 0).astype(F32)
    final = (i < pl.num_programs(1) - 1).astype(F32)
    edge = jnp.concatenate([mod(xp_ref[sub:, :]) * first, mod(xn_ref[:sub, :]) * final], axis=0)
    h = jnp.concatenate([to_tile_order(hf_ref), edge], axis=0).astype(BF16)

    row = lax.broadcasted_iota(jnp.int32, (sub, 1), 0)

    def conv(slot, half, c0):
        lanes = slice(half * tn, (half + 1) * tn)
        cw = cw_ref[:, c0:c0 + tn]
        mid = u_ref[slot, 0:tm, lanes]
        top = jnp.where(row == 0, pltpu.roll(u_ref[slot, tm:tm + sub, lanes], 1, axis=0),
                        pltpu.roll(u_ref[slot, tm - sub:tm, lanes], 1, axis=0))
        prev = jnp.concatenate([top, u_ref[slot, 0:tm - sub, lanes]], axis=0)
        bot = jnp.where(row == sub - 1, pltpu.roll(u_ref[slot, tm + sub:tm + 2 * sub, lanes], sub - 1, axis=0),
                        pltpu.roll(u_ref[slot, 0:sub, lanes], sub - 1, axis=0))
        nxt = jnp.concatenate([u_ref[slot, sub:tm, lanes], bot], axis=0)
        return prev * cw[0:1] + mid * cw[1:2] + nxt * cw[2:3]

    for j in range(ff // tn):
        slot = j % 2
        c0 = j * tn
        u_ref[slot, :, 0:tn] = jnp.dot(h, wup_ref[:, c0:c0 + tn], preferred_element_type=F32)
        u_ref[slot, :, tn:2 * tn] = jnp.dot(h, wup_ref[:, ff + c0:ff + c0 + tn], preferred_element_type=F32)
        a = jax.nn.gelu(conv(slot, 0, c0), approximate=True) * conv(slot, 1, ff + c0)
        a_ref[:, c0:c0 + tn] = a.astype(BF16)

    y = jnp.dot(a_ref[...], wd_ref[...], preferred_element_type=F32)
    for c in range(nc):
        hf_ref[c] = y[:, c * LANES:(c + 1) * LANES]
    out_ref[...] = x_ref[...] + gate_ref[...] * _rms(to_seq_order(hf_ref), gy_ref[...])


def _ffn(x, g, sc, sh, gate, gy, w_up, cw, w_down, layer, tm=512, tn=256):
    b, s, d = x.shape
    ff = w_down.shape[1]
    hb = tm // BF16_ROWS
    last = s // BF16_ROWS - 1

    def const(shape, **kw):
        return pl.BlockSpec(shape, lambda bi, i: (0,) * len(shape), **kw)

    def per_batch():
        return pl.BlockSpec((None, 1, d), lambda bi, i: (bi, 0, 0))

    resident = dict(pipeline_mode=pl.Buffered(1))
    return pl.pallas_call(
        functools.partial(_ffn_body, tm=tm, tn=tn),
        grid=(b, s // tm),
        in_specs=[pl.BlockSpec((None, tm, d), lambda bi, i: (bi, i, 0)),
                  pl.BlockSpec((None, BF16_ROWS, d), lambda bi, i: (bi, jnp.maximum(i * hb - 1, 0), 0)),
                  pl.BlockSpec((None, BF16_ROWS, d), lambda bi, i: (bi, jnp.minimum((i + 1) * hb, last), 0)),
                  const(g.shape), per_batch(), per_batch(), per_batch(), const(gy.shape),
                  _layer_spec(w_up, layer, **resident), _layer_spec(cw, layer),
                  _layer_spec(w_down, layer, **resident)],
        out_specs=pl.BlockSpec((None, tm, d), lambda bi, i: (bi, i, 0)),
        out_shape=jax.ShapeDtypeStruct((b, s, d), F32),
        scratch_shapes=[pltpu.VMEM((d // LANES, tm, LANES), F32),
                        pltpu.VMEM((2, tm + BF16_ROWS, 2 * tn), F32),
                        pltpu.VMEM((tm, ff), BF16)],
        compiler_params=_cparams(("parallel", "arbitrary")),
        name="ffn",
    )(x, x, x, g, sc, sh, gate, gy, w_up, cw, w_down)


def _pack_w_in(w_in):
    nl, d, _ = w_in.shape
    hd = HEAD_DIM
    aq = w_in[..., 0:256]
    aq = jnp.concatenate([aq[..., 0:hd], aq[..., 2 * hd:3 * hd], aq[..., hd:2 * hd], aq[..., 3 * hd:]], axis=-1)
    ckr = w_in[..., 1664:1696]
    ckr = jnp.concatenate([jnp.zeros((nl, d, C_NOPE_DIM), w_in.dtype), ckr,
                           jnp.zeros((nl, d, LANES - C_NOPE_DIM - C_ROPE_DIM), w_in.dtype)], axis=-1)
    d0 = 1696
    return jnp.concatenate([aq, w_in[..., 256:1280], w_in[..., d0:d0 + PATTERN_COLS], w_in[..., 1280:1664], ckr,
                            w_in[..., d0 + PATTERN_COLS:]], axis=-1).astype(BF16)


def _pack_w_uq(w):
    nl, r, _ = w.shape
    half = C_ROPE_DIM // 2
    w = w.reshape(nl, r, C_HEADS, C_NOPE_DIM + C_ROPE_DIM)
    nope, r1, r2 = w[..., :C_NOPE_DIM], w[..., C_NOPE_DIM:C_NOPE_DIM + half], w[..., C_NOPE_DIM + half:]
    z_tail = jnp.zeros((nl, r, C_HEADS, LANES - C_NOPE_DIM - C_ROPE_DIM), w.dtype)
    plain = jnp.concatenate([nope, r1, r2, z_tail], axis=-1).reshape(nl, r, C_HEADS * LANES)
    swapped = jnp.concatenate([jnp.zeros_like(nope), r2, r1, z_tail], axis=-1).reshape(nl, r, C_HEADS * LANES)
    return jnp.concatenate([plain, swapped], axis=-1).astype(BF16)


def _pack_w_ukv(w):
    nl, r, _ = w.shape
    w = w.reshape(nl, r, C_HEADS, C_NOPE_DIM + C_V_DIM)
    kn = jnp.concatenate([w[..., :C_NOPE_DIM], jnp.zeros((nl, r, C_HEADS, LANES - C_NOPE_DIM), w.dtype)], axis=-1)
    return jnp.concatenate([kn.reshape(nl, r, C_HEADS * LANES),
                            w[..., C_NOPE_DIM:].reshape(nl, r, C_HEADS * C_V_DIM)], axis=-1).astype(BF16)


def _pack_w_out(w_out):
    hd = HEAD_DIM
    return jnp.concatenate([w_out[:, 0:hd], w_out[:, 2 * hd:3 * hd], w_out[:, hd:2 * hd], w_out[:, 3 * hd:]],
                           axis=1).astype(BF16)


def _rope_tables(positions):
    half = C_ROPE_DIM // 2
    inv_freq = ROPE_THETA ** (-jnp.arange(half, dtype=jnp.float32) / half)
    zeros_lead = jnp.zeros((C_NOPE_DIM,), F32)
    zeros_tail = jnp.zeros((LANES - C_NOPE_DIM - C_ROPE_DIM,), F32)
    freq = jnp.concatenate([zeros_lead, inv_freq, inv_freq, zeros_tail])
    sign = jnp.concatenate([zeros_lead, -jnp.ones((half,), F32), jnp.ones((half,), F32), zeros_tail])
    keep = jnp.arange(LANES) < C_NOPE_DIM + C_ROPE_DIM
    ang = positions.astype(jnp.float32)[..., None] * freq
    cos_t = jnp.where(keep, jnp.cos(ang), 0.0)
    sin_t = jnp.sin(ang) * sign
    return cos_t, sin_t


def _swap_matrix():
    half = C_ROPE_DIM // 2
    idx = jnp.arange(LANES)
    src = jnp.where((idx >= C_NOPE_DIM) & (idx < C_NOPE_DIM + half), idx + half,
                    jnp.where((idx >= C_NOPE_DIM + half) & (idx < C_NOPE_DIM + C_ROPE_DIM), idx - half, idx))
    return (idx[:, None] == src[None, :]).astype(BF16)


def kernel(x, c, positions, rel_bias, w_mod, b_mod, norm_g, w_in, a_sink, b_conv, c_norm_q, c_norm_kv,
           c_w_uq, c_w_ukv, w_out, w_up, ffn_conv, w_down):
    b, s, d = x.shape
    depth = w_in.shape[0]

    a_order = jnp.array(A_HEAD_ORDER)
    bias_a = _band_bias(rel_bias[:, :A_Q_HEADS][:, a_order], A_RADIUS, 1, A_SUB_Q)
    bias_d = [_band_bias(rel_bias[:, A_Q_HEADS + i * D_HEADS:A_Q_HEADS + (i + 1) * D_HEADS], (w // 2) // dil, dil,
                         D_SUB_Q)
              for i, (w, dil) in enumerate(D_PATTERNS)]
    cos_t, sin_t = _rope_tables(positions)
    swap = _swap_matrix()

    w_in_p = _pack_w_in(w_in)
    w_uq_p = _pack_w_uq(c_w_uq)
    w_ukv_p = _pack_w_ukv(c_w_ukv)
    w_out_p = _pack_w_out(w_out)
    w_up_b = w_up.astype(BF16)
    w_down_b = w_down.astype(BF16)

    mod = _modulation(c, w_mod, b_mod)

    for l in range(depth):
        sh1, sc1, g1, sh2, sc2, g2 = (mod[l, k].reshape(b, 1, d) for k in range(6))
        gains = norm_g[l]

        qkv, qkv_d4, qkv_d16, qc, kc, vc = _inproj(x, gains[0:1], sc1, sh1, w_in_p, cos_t, sin_t, c_norm_q[l][None],
                                                   c_norm_kv[l][None], w_uq_p, w_ukv_p, swap, l)

        sink = jnp.broadcast_to(jnp.repeat(a_sink[l][a_order], A_SUB_Q)[:, None], (A_Q_HEADS * A_SUB_Q, LANES))
        oa = _band_attention(qkv, bias_a, sink, dil=1, radius=A_RADIUS, tq=BAND_ROWS, q_off=OFF_AQ, k_off=OFF_AK,
                             v_off=OFF_AV, kv_w=LANES, has_lse=False)[0].reshape(b, s, 2 * LANES)

        oc = _mla_attn(qc, kc, vc)

        od, lse = [], []
        for i, ((w, dil), src) in enumerate(zip(D_PATTERNS, (qkv, qkv_d4, qkv_d16))):
            off = OFF_D if dil == 1 else 0
            seq = s // dil
            o_i, l_i = _band_attention(src, bias_d[i], None, dil=dil, radius=(w // 2) // dil, tq=BAND_ROWS, q_off=off,
                                       k_off=off + 256, v_off=off + 512, kv_w=2 * LANES, has_lse=True,
                                       group=max(1, BAND_ROWS // seq))
            od.append(o_i)
            lse.append(l_i)

        x = _outproj(x, oa, qkv, oc, od, lse, w_out_p, b_conv, gains[1:2], g1, l)
        x = _ffn(x, gains[2:3], sc2, sh2, g2, gains[3:4], w_up_b, ffn_conv, w_down_b, l)
    return x
```

```python
import functools
import math

import jax
import jax.numpy as jnp
from jax import lax
from jax.experimental import pallas as pl
from jax.experimental.pallas import tpu as pltpu

HEAD_DIM = 64
A_Q_HEADS = 4
A_RADIUS = 128
C_HEADS = 4
C_Q_RANK = 256
C_KV_RANK = 128
C_NOPE_DIM = 64
C_ROPE_DIM = 32
C_V_DIM = 64
ROPE_THETA = 10000.0
D_HEADS = 4
D_PATTERNS = ((128, 1), (512, 4), (2048, 16))
REL_BUCKETS = 32
REL_MAX_DISTANCE = 1024
EPS = 1e-6
NEG = -1e30

LANES = 128
BF16_ROWS = 16
A_SUB_Q = 128
D_SUB_Q = 128
BAND_ROWS = 4096
VMEM_LIMIT = 48 * 1024 * 1024

OFF_AQ, OFF_AK, OFF_AV = 0, 256, 384
OFF_BB, OFF_BC, OFF_BH = 512, 768, 1024
OFF_D = 1280
A_HEAD_ORDER = (0, 2, 1, 3)
PATTERN_COLS = 3 * D_HEADS * HEAD_DIM
MAIN_COLS = OFF_D + PATTERN_COLS
OFF_C = MAIN_COLS
C_COLS = C_Q_RANK + C_KV_RANK + LANES
OFF_D4 = OFF_C + C_COLS
COL_CHUNK = 512

F32 = jnp.float32
BF16 = jnp.bfloat16


def _layer_spec(w, layer, **kw):
    return pl.BlockSpec((None,) + w.shape[1:], lambda *_: (layer,) + (0,) * (w.ndim - 1), **kw)


def _cparams(sem):
    return pltpu.CompilerParams(dimension_semantics=sem, vmem_limit_bytes=VMEM_LIMIT)


def _rms(v, g):
    return v * lax.rsqrt(jnp.mean(v * v, axis=-1, keepdims=True) + EPS) * g


def _mod_body(c_ref, w_ref, b_ref, o_ref):
    c = c_ref[...]
    act = c / (1.0 + jnp.exp(-c))
    o_ref[...] = jnp.dot(act, w_ref[...], preferred_element_type=F32,
                         precision=lax.Precision.HIGHEST) + b_ref[...]


def _modulation(c, w_mod, b_mod):
    nl, d, _ = w_mod.shape
    b = c.shape[0]
    return pl.pallas_call(
        _mod_body,
        grid=(nl, 6),
        in_specs=[pl.BlockSpec((b, d), lambda l, k: (0, 0)),
                  pl.BlockSpec((None, d, d), lambda l, k: (l, 0, k)),
                  pl.BlockSpec((None, 1, d), lambda l, k: (l, 0, k))],
        out_specs=pl.BlockSpec((None, None, b, d), lambda l, k: (l, k, 0, 0)),
        out_shape=jax.ShapeDtypeStruct((nl, 6, b, d), F32),
        compiler_params=_cparams(("parallel", "parallel")),
        name="modulation",
    )(c, w_mod, b_mod.reshape(nl, 1, 6 * d))


def _latent_qkv(cc, cs, sn, gq_ref, gkv_ref, wq_ref, wkv_ref, swap_ref, q_ref, k_ref, v_ref):
    scale = (C_NOPE_DIM + C_ROPE_DIM) ** -0.5
    nq = _rms(cc[:, :C_Q_RANK], gq_ref[...]).astype(BF16)
    qq = jnp.dot(nq, wq_ref[...], preferred_element_type=F32)
    nkv = _rms(cc[:, C_Q_RANK:C_Q_RANK + C_KV_RANK], gkv_ref[...]).astype(BF16)
    kvv = jnp.dot(nkv, wkv_ref[...], preferred_element_type=F32)
    kr = cc[:, C_Q_RANK + C_KV_RANK:].astype(BF16)
    kr_sw = jnp.dot(kr, swap_ref[...], preferred_element_type=F32)
    k_rope = kr.astype(F32) * cs + kr_sw * sn
    hw = C_HEADS * LANES
    for h in range(C_HEADS):
        a = qq[:, h * LANES:(h + 1) * LANES]
        a_sw = qq[:, hw + h * LANES:hw + (h + 1) * LANES]
        q_ref[:, h * LANES:(h + 1) * LANES] = ((a * cs + a_sw * sn) * scale).astype(BF16)
        k_ref[:, h * LANES:(h + 1) * LANES] = (kvv[:, h * LANES:(h + 1) * LANES] + k_rope).astype(BF16)
    v_ref[...] = kvv[:, hw:].astype(BF16)


def _inproj_body(x_ref, g_ref, sc_ref, sh_ref, w_ref, cos_ref, sin_ref, gq_ref, gkv_ref, wq_ref, wkv_ref, swap_ref,
                 om_ref, o4_ref, o16_ref, q_ref, k_ref, v_ref, hf_ref, *, tm):
    h = _rms(x_ref[...], g_ref[...]) * (1.0 + sc_ref[...]) + sh_ref[...]
    nc = h.shape[-1] // LANES
    for c in range(nc):
        hf_ref[c] = h[:, c * LANES:(c + 1) * LANES]
    hb = h.astype(BF16)
    for c0 in range(0, MAIN_COLS, COL_CHUNK):
        om_ref[:, c0:c0 + COL_CHUNK] = jnp.dot(hb, w_ref[:, c0:c0 + COL_CHUNK],
                                               preferred_element_type=F32).astype(BF16)
    cc = jnp.dot(hb, w_ref[:, OFF_C:OFF_C + C_COLS], preferred_element_type=F32)
    _latent_qkv(cc, cos_ref[...], sin_ref[...], gq_ref, gkv_ref, wq_ref, wkv_ref, swap_ref, q_ref, k_ref, v_ref)
    for o_ref, dil, c0 in ((o4_ref, 4, OFF_D4), (o16_ref, 16, OFF_D4 + PATTERN_COLS)):
        n = tm // dil
        hp = jnp.concatenate(
            [jnp.concatenate([hf_ref[c, pl.ds(r, n, stride=dil), :] for r in range(dil)], axis=0)
             for c in range(nc)], axis=1).astype(BF16)
        res = jnp.dot(hp, w_ref[:, c0:c0 + PATTERN_COLS], preferred_element_type=F32).astype(BF16)
        for r in range(dil):
            o_ref[r] = res[r * n:(r + 1) * n]


def _inproj(x, g, sc, sh, w, cos_t, sin_t, gq, gkv, wq, wkv, swap, layer, tm=1024):
    b, s, d = x.shape
    d4, d16 = D_PATTERNS[1][1], D_PATTERNS[2][1]
    hw = C_HEADS * LANES
    vw = C_HEADS * C_V_DIM

    def const(shape):
        return pl.BlockSpec(shape, lambda bi, i: (0,) * len(shape))

    def rows(width):
        return pl.BlockSpec((None, tm, width), lambda bi, i: (bi, i, 0))

    def per_batch():
        return pl.BlockSpec((None, 1, d), lambda bi, i: (bi, 0, 0))

    return pl.pallas_call(
        functools.partial(_inproj_body, tm=tm),
        grid=(b, s // tm),
        in_specs=[rows(d), const(g.shape), per_batch(), per_batch(),
                  _layer_spec(w, layer, pipeline_mode=pl.Buffered(1)),
                  rows(LANES), rows(LANES), const(gq.shape), const(gkv.shape),
                  _layer_spec(wq, layer), _layer_spec(wkv, layer), const(swap.shape)],
        out_specs=[rows(MAIN_COLS),
                   pl.BlockSpec((None, d4, tm // d4, PATTERN_COLS), lambda bi, i: (bi, 0, i, 0)),
                   pl.BlockSpec((None, d16, tm // d16, PATTERN_COLS), lambda bi, i: (bi, 0, i, 0)),
                   rows(hw), rows(hw), rows(vw)],
        out_shape=[jax.ShapeDtypeStruct((b, s, MAIN_COLS), BF16),
                   jax.ShapeDtypeStruct((b, d4, s // d4, PATTERN_COLS), BF16),
                   jax.ShapeDtypeStruct((b, d16, s // d16, PATTERN_COLS), BF16),
                   jax.ShapeDtypeStruct((b, s, hw), BF16), jax.ShapeDtypeStruct((b, s, hw), BF16),
                   jax.ShapeDtypeStruct((b, s, vw), BF16)],
        scratch_shapes=[pltpu.VMEM((d // LANES, tm, LANES), F32)],
        compiler_params=_cparams(("parallel", "parallel")),
        name="inproj",
    )(x, g, sc, sh, w, cos_t, sin_t, gq, gkv, wq, wkv, swap)


def _t5_bucket(rel):
    half = REL_BUCKETS // 2
    max_exact = half // 2
    n = jnp.abs(rel)
    n_f = jnp.maximum(n, 1).astype(jnp.float32)
    large = max_exact + (jnp.log(n_f / max_exact) / math.log(REL_MAX_DISTANCE / max_exact)
                         * (half - max_exact)).astype(jnp.int32)
    large = jnp.minimum(large, half - 1)
    return jnp.where(rel > 0, half, 0) + jnp.where(n < max_exact, n, large)


def _band_bias(table, radius, stride, sub_q):
    qi = jnp.arange(sub_q)[:, None]
    kj = jnp.arange(sub_q + 2 * radius)[None, :]
    off = kj - radius - qi
    onehot = (_t5_bucket(off * stride)[..., None] == jnp.arange(REL_BUCKETS)).astype(F32)
    bias = jnp.einsum('qkb,bh->hqk', onehot, table.astype(F32), precision=lax.Precision.HIGHEST)
    bias = jnp.where((jnp.abs(off) <= radius)[None], bias, NEG)
    return bias.reshape(-1, bias.shape[-1])


def _band_body(*refs, radius, tq, seq, kv_w, has_sink, has_lse):
    q_ref, kp_ref, kc_ref, kn_ref, vp_ref, vc_ref, vn_ref, bias_ref = refs[:8]
    rest = refs[8:]
    if has_sink:
        sink_ref, rest = rest[0], rest[1:]
    o_ref = rest[0]
    lse_ref = rest[1] if has_lse else None

    i = pl.program_id(2)
    nh = 4
    sub_q = bias_ref.shape[0] // nh
    nk = sub_q + 2 * radius
    per = kv_w // HEAD_DIM
    lane = lax.broadcasted_iota(jnp.int32, (1, kv_w), 1)
    hmask = [(lane >= HEAD_DIM * (h % per)) & (lane < HEAD_DIM * (h % per + 1)) for h in range(nh)]
    scale = HEAD_DIM ** -0.5

    def pick(rows):
        out = rows[(per - 1) * sub_q:per * sub_q]
        for h in range(per - 2, -1, -1):
            out = jnp.where(hmask[h], rows[h * sub_q:(h + 1) * sub_q], out)
        return out

    for rr in range(q_ref.shape[0]):
        kcat = jnp.concatenate([kp_ref[rr], kc_ref[rr], kn_ref[rr]], axis=0)
        vcat = jnp.concatenate([vp_ref[rr], vc_ref[rr], vn_ref[rr]], axis=0)
        for sb in range(tq // sub_q):
            r0 = sb * sub_q
            qs = q_ref[rr, r0:r0 + sub_q, :] * scale
            ks = kcat[r0:r0 + nk]
            vs = vcat[r0:r0 + nk]
            at_edge = r0 - radius < 0 or r0 + sub_q + radius > tq
            kpos = i * tq + (r0 - radius) + lax.broadcasted_iota(jnp.int32, (1, nk), 1)
            inb = (kpos >= 0) & (kpos < seq)
            parts = []
            for h in range(nh):
                src = qs[:, (h // per) * kv_w:(h // per + 1) * kv_w]
                parts.append(jnp.where(hmask[h], src, jnp.zeros_like(src)))
            qst = jnp.concatenate(parts, axis=0)
            sc = lax.dot_general(qst, ks, (((1,), (1,)), ((), ())), preferred_element_type=F32)
            sc = sc + bias_ref[...]
            if at_edge:
                sc = jnp.where(inb, sc, NEG)
            m = jnp.max(sc, axis=-1, keepdims=True)
            if has_sink:
                m = jnp.maximum(m, jnp.max(sink_ref[...], axis=-1, keepdims=True))
            p = jnp.exp(sc - m)
            den = jnp.sum(p, axis=-1, keepdims=True)
            if has_sink:
                den = den + jnp.max(jnp.exp(sink_ref[...] - m), axis=-1, keepdims=True)
            o = jnp.dot(p.astype(BF16), vs, preferred_element_type=F32) / den
            lse = jnp.broadcast_to(m + jnp.log(den), o.shape) if has_lse else None
            for g in range(nh // per):
                rows = slice(g * per * sub_q, (g + 1) * per * sub_q)
                cols = slice(g * kv_w, (g + 1) * kv_w)
                o_ref[rr, r0:r0 + sub_q, cols] = pick(o[rows]).astype(BF16)
                if has_lse:
                    lse_ref[rr, r0:r0 + sub_q, cols] = pick(lse[rows])


def _band_attention(qkv, bias, sink, *, dil, radius, tq, q_off, k_off, v_off, kv_w, has_lse, group=1):
    b = qkv.shape[0]
    seq = qkv.shape[-2]
    tq = min(tq, seq)
    qw = 2 * LANES
    steps = tq // radius
    last = seq // radius - 1
    if dil == 1:
        qkv = qkv.reshape(b, 1, seq, qkv.shape[-1])

    def spec(rows, width, off, row_index):
        return pl.BlockSpec((None, group, rows, width), lambda bi, r, i: (bi, r, row_index(i), off // width))

    def cur(i):
        return i

    def prev(i):
        return jnp.maximum(i * steps - 1, 0)

    def nxt(i):
        return jnp.minimum((i + 1) * steps, last)

    in_specs = [spec(tq, qw, q_off, cur)]
    for off in (k_off, v_off):
        in_specs += [spec(radius, kv_w, off, prev), spec(tq, kv_w, off, cur), spec(radius, kv_w, off, nxt)]
    in_specs.append(pl.BlockSpec(bias.shape, lambda bi, r, i: (0, 0)))
    args = [qkv] * 7 + [bias]
    if sink is not None:
        in_specs.append(pl.BlockSpec(sink.shape, lambda bi, r, i: (0, 0)))
        args.append(sink)

    o_spec = pl.BlockSpec((None, group, tq, qw), lambda bi, r, i: (bi, r, i, 0))
    out_shape = [jax.ShapeDtypeStruct((b, dil, seq, qw), BF16)]
    out_specs = [o_spec]
    if has_lse:
        out_shape.append(jax.ShapeDtypeStruct((b, dil, seq, qw), F32))
        out_specs.append(o_spec)

    outs = pl.pallas_call(
        functools.partial(_band_body, radius=radius, tq=tq, seq=seq, kv_w=kv_w,
                          has_sink=sink is not None, has_lse=has_lse),
        grid=(b, dil // group, seq // tq),
        in_specs=in_specs,
        out_specs=out_specs,
        out_shape=out_shape,
        compiler_params=_cparams(("parallel", "parallel", "parallel")),
        name=f"band_attention_d{dil}_r{radius}",
    )(*args)
    return outs


def _mla_attn_body(q_ref, k_ref, v_ref, o_ref, *, rows):
    lo = lax.broadcasted_iota(jnp.int32, (1, LANES), 1) < C_V_DIM
    for r0 in range(0, q_ref.shape[0], rows):
        for g in range(C_HEADS // 2):
            acc = None
            for half in range(2):
                h = 2 * g + half
                q = q_ref[r0:r0 + rows, h * LANES:(h + 1) * LANES]
                k = k_ref[:, h * LANES:(h + 1) * LANES]
                sc = lax.dot_general(q, k, (((1,), (1,)), ((), ())), preferred_element_type=F32)
                m = jnp.max(sc, axis=-1, keepdims=True)
                p = jnp.exp(sc - m)
                den = jnp.sum(p, axis=-1, keepdims=True)
                o = jnp.dot(p.astype(BF16), v_ref[:, g * LANES:(g + 1) * LANES],
                            preferred_element_type=F32) / den
                acc = o if acc is None else jnp.where(lo, acc, o)
            o_ref[r0:r0 + rows, g * LANES:(g + 1) * LANES] = acc.astype(BF16)


def _mla_attn(q, k, v, tq=1024, rows=256):
    b, s, hw = q.shape
    vw = v.shape[-1]
    return pl.pallas_call(
        functools.partial(_mla_attn_body, rows=rows),
        grid=(b, s // tq),
        in_specs=[pl.BlockSpec((None, tq, hw), lambda bi, i: (bi, i, 0)),
                  pl.BlockSpec((None, s, hw), lambda bi, i: (bi, 0, 0)),
                  pl.BlockSpec((None, s, vw), lambda bi, i: (bi, 0, 0))],
        out_specs=pl.BlockSpec((None, tq, vw), lambda bi, i: (bi, i, 0)),
        out_shape=jax.ShapeDtypeStruct((b, s, vw), BF16),
        compiler_params=_cparams(("parallel", "arbitrary")),
        name="mla_attention",
    )(q, k, v)


def _outproj_body(x_ref, oa_ref, bb_ref, bc_ref, bh_ref, bcp_ref, bhp_ref, bcn_ref, bhn_ref, oc_ref,
                  o0_ref, o1_ref, o2_ref, l0_ref, l1_ref, l2_ref, w_ref, cw_ref, gn_ref, gate_ref,
                  out_ref, conv_ref, *perm_refs, tm):
    i = pl.program_id(1)

    def natural(ref, s_ref):
        dil, n, width = ref.shape
        for r in range(dil):
            blk = ref[r].astype(F32)
            for c in range(width // LANES):
                s_ref[c, pl.ds(r, n, stride=dil), :] = blk[:, c * LANES:(c + 1) * LANES]
        return jnp.concatenate([s_ref[c] for c in range(width // LANES)], axis=1)

    pad = 8
    first = (i > 0).astype(F32)
    final = (i < pl.num_programs(1) - 1).astype(F32)
    conv_ref[pad:pad + tm, :] = bc_ref[...].astype(F32) * bh_ref[...].astype(F32)
    conv_ref[pad - 1:pad, :] = (bcp_ref[BF16_ROWS - 1:, :].astype(F32)
                                * bhp_ref[BF16_ROWS - 1:, :].astype(F32)) * first
    conv_ref[pad + tm:pad + tm + 1, :] = (bcn_ref[:1, :].astype(F32) * bhn_ref[:1, :].astype(F32)) * final
    cw = cw_ref[...]
    conv = (conv_ref[pad - 1:pad - 1 + tm, :] * cw[0:1] + conv_ref[pad:pad + tm, :] * cw[1:2]
            + conv_ref[pad + 1:pad + 1 + tm, :] * cw[2:3])
    ob = (bb_ref[...].astype(F32) * conv).astype(BF16)
    l0 = l0_ref[0]
    l1, l2 = natural(l1_ref, perm_refs[0]), natural(l2_ref, perm_refs[1])
    o1, o2 = natural(o1_ref, perm_refs[2]), natural(o2_ref, perm_refs[3])
    mx = jnp.maximum(jnp.maximum(l0, l1), l2)
    e0, e1, e2 = jnp.exp(l0 - mx), jnp.exp(l1 - mx), jnp.exp(l2 - mx)
    od = ((o0_ref[0].astype(F32) * e0 + o1 * e1 + o2 * e2) / (e0 + e1 + e2)).astype(BF16)
    wd = 2 * LANES
    y = jnp.dot(oa_ref[...], w_ref[0:wd, :], preferred_element_type=F32)
    y = y + jnp.dot(ob, w_ref[wd:2 * wd, :], preferred_element_type=F32)
    y = y + jnp.dot(oc_ref[...], w_ref[2 * wd:3 * wd, :], preferred_element_type=F32)
    y = y + jnp.dot(od, w_ref[3 * wd:4 * wd, :], preferred_element_type=F32)
    out_ref[...] = x_ref[...] + gate_ref[...] * _rms(y, gn_ref[...])


def _outproj(x, oa, qkv, oc, od, lse, w, cw, gn, gate, layer, tm=1024):
    b, s, d = x.shape
    wd = 2 * LANES
    hb = tm // BF16_ROWS
    last = s // BF16_ROWS - 1

    def rows(width, off=0):
        return pl.BlockSpec((None, tm, width), lambda bi, i: (bi, i, off // width))

    def halo_prev(off):
        return pl.BlockSpec((None, BF16_ROWS, wd), lambda bi, i: (bi, jnp.maximum(i * hb - 1, 0), off // wd))

    def halo_next(off):
        return pl.BlockSpec((None, BF16_ROWS, wd), lambda bi, i: (bi, jnp.minimum((i + 1) * hb, last), off // wd))

    def const(shape):
        return pl.BlockSpec(shape, lambda bi, i: (0,) * len(shape))

    def grouped(arr):
        dil = arr.shape[1]
        return pl.BlockSpec((None, dil, tm // dil, wd), lambda bi, i: (bi, 0, i, 0))

    return pl.pallas_call(
        functools.partial(_outproj_body, tm=tm),
        grid=(b, s // tm),
        in_specs=[rows(d), rows(wd), rows(wd, OFF_BB), rows(wd, OFF_BC), rows(wd, OFF_BH),
                  halo_prev(OFF_BC), halo_prev(OFF_BH), halo_next(OFF_BC), halo_next(OFF_BH),
                  rows(wd)] + [grouped(a) for a in od] + [grouped(a) for a in lse] + [
                  _layer_spec(w, layer), _layer_spec(cw, layer), const(gn.shape),
                  pl.BlockSpec((None, 1, d), lambda bi, i: (bi, 0, 0))],
        out_specs=rows(d),
        out_shape=jax.ShapeDtypeStruct((b, s, d), F32),
        scratch_shapes=[pltpu.VMEM((tm + 16, wd), F32)] + [pltpu.VMEM((wd // LANES, tm, LANES), F32)] * 4,
        compiler_params=_cparams(("parallel", "arbitrary")),
        name="outproj",
    )(x, oa, qkv, qkv, qkv, qkv, qkv, qkv, qkv, oc, od[0], od[1], od[2], lse[0], lse[1], lse[2],
      w, cw, gn, gate)


def _ffn_body(x_ref, xp_ref, xn_ref, g_ref, sc_ref, sh_ref, gate_ref, gy_ref, wup_ref, cw_ref, wd_ref, out_ref,
              hf_ref, u_ref, a_ref, *, tm, tn):
    i = pl.program_id(1)
    ff = wd_ref.shape[0]
    d = x_ref.shape[-1]
    nc = d // LANES
    sub = 8
    seg = tm // sub

    def mod(v):
        return _rms(v, g_ref[...]) * (1.0 + sc_ref[...]) + sh_ref[...]

    def to_tile_order(ref):
        return jnp.concatenate(
            [jnp.concatenate([ref[c, pl.ds(a, sub, stride=seg), :] for a in range(seg)], axis=0)
             for c in range(nc)], axis=1)

    def to_seq_order(ref):
        return jnp.concatenate(
            [jnp.concatenate([ref[c, pl.ds(b, seg, stride=sub), :] for b in range(sub)], axis=0)
             for c in range(nc)], axis=1)

    hm = mod(x_ref[...])
    for c in range(nc):
        hf_ref[c] = hm[:, c * LANES:(c + 1) * LANES]
    first = (i > 0).astype(F32)
    final = (i < pl.num_programs(1) - 1).astype(F32)
    edge = jnp.concatenate([mod(xp_ref[sub:, :]) * first, mod(xn_ref[:sub, :]) * final], axis=0)
    h = jnp.concatenate([to_tile_order(hf_ref), edge], axis=0).astype(BF16)

    row = lax.broadcasted_iota(jnp.int32, (sub, 1), 0)

    def conv(slot, half, c0):
        lanes = slice(half * tn, (half + 1) * tn)
        cw = cw_ref[:, c0:c0 + tn]
        mid = u_ref[slot, 0:tm, lanes]
        top = jnp.where(row == 0, pltpu.roll(u_ref[slot, tm:tm + sub, lanes], 1, axis=0),
                        pltpu.roll(u_ref[slot, tm - sub:tm, lanes], 1, axis=0))
        prev = jnp.concatenate([top, u_ref[slot, 0:tm - sub, lanes]], axis=0)
        bot = jnp.where(row == sub - 1, pltpu.roll(u_ref[slot, tm + sub:tm + 2 * sub, lanes], sub - 1, axis=0),
                        pltpu.roll(u_ref[slot, 0:sub, lanes], sub - 1, axis=0))
        nxt = jnp.concatenate([u_ref[slot, sub:tm, lanes], bot], axis=0)
        return prev * cw[0:1] + mid * cw[1:2] + nxt * cw[2:3]

    for j in range(ff // tn):
        slot = j % 2
        c0 = j * tn
        u_ref[slot, :, 0:tn] = jnp.dot(h, wup_ref[:, c0:c0 + tn], preferred_element_type=F32)
        u_ref[slot, :, tn:2 * tn] = jnp.dot(h, wup_ref[:, ff + c0:ff + c0 + tn], preferred_element_type=F32)
        a = jax.nn.gelu(conv(slot, 0, c0), approximate=True) * conv(slot, 1, ff + c0)
        a_ref[:, c0:c0 + tn] = a.astype(BF16)

    y = jnp.dot(a_ref[...], wd_ref[...], preferred_element_type=F32)
    for c in range(nc):
        hf_ref[c] = y[:, c * LANES:(c + 1) * LANES]
    out_ref[...] = x_ref[...] + gate_ref[...] * _rms(to_seq_order(hf_ref), gy_ref[...])


def _ffn(x, g, sc, sh, gate, gy, w_up, cw, w_down, layer, tm=512, tn=256):
    b, s, d = x.shape
    ff = w_down.shape[1]
    hb = tm // BF16_ROWS
    last = s // BF16_ROWS - 1

    def const(shape, **kw):
        return pl.BlockSpec(shape, lambda bi, i: (0,) * len(shape), **kw)

    def per_batch():
        return pl.BlockSpec((None, 1, d), lambda bi, i: (bi, 0, 0))

    resident = dict(pipeline_mode=pl.Buffered(1))
    return pl.pallas_call(
        functools.partial(_ffn_body, tm=tm, tn=tn),
        grid=(b, s // tm),
        in_specs=[pl.BlockSpec((None, tm, d), lambda bi, i: (bi, i, 0)),
                  pl.BlockSpec((None, BF16_ROWS, d), lambda bi, i: (bi, jnp.maximum(i * hb - 1, 0), 0)),
                  pl.BlockSpec((None, BF16_ROWS, d), lambda bi, i: (bi, jnp.minimum((i + 1) * hb, last), 0)),
                  const(g.shape), per_batch(), per_batch(), per_batch(), const(gy.shape),
                  _layer_spec(w_up, layer, **resident), _layer_spec(cw, layer),
                  _layer_spec(w_down, layer, **resident)],
        out_specs=pl.BlockSpec((None, tm, d), lambda bi, i: (bi, i, 0)),
        out_shape=jax.ShapeDtypeStruct((b, s, d), F32),
        scratch_shapes=[pltpu.VMEM((d // LANES, tm, LANES), F32),
                        pltpu.VMEM((2, tm + BF16_ROWS, 2 * tn), F32),
                        pltpu.VMEM((tm, ff), BF16)],
        compiler_params=_cparams(("parallel", "arbitrary")),
        name="ffn",
    )(x, x, x, g, sc, sh, gate, gy, w_up, cw, w_down)


def _pack_w_in(w_in):
    nl, d, _ = w_in.shape
    hd = HEAD_DIM
    aq = w_in[..., 0:256]
    aq = jnp.concatenate([aq[..., 0:hd], aq[..., 2 * hd:3 * hd], aq[..., hd:2 * hd], aq[..., 3 * hd:]], axis=-1)
    ckr = w_in[..., 1664:1696]
    ckr = jnp.concatenate([jnp.zeros((nl, d, C_NOPE_DIM), w_in.dtype), ckr,
                           jnp.zeros((nl, d, LANES - C_NOPE_DIM - C_ROPE_DIM), w_in.dtype)], axis=-1)
    d0 = 1696
    return jnp.concatenate([aq, w_in[..., 256:1280], w_in[..., d0:d0 + PATTERN_COLS], w_in[..., 1280:1664], ckr,
                            w_in[..., d0 + PATTERN_COLS:]], axis=-1).astype(BF16)


def _pack_w_uq(w):
    nl, r, _ = w.shape
    half = C_ROPE_DIM // 2
    w = w.reshape(nl, r, C_HEADS, C_NOPE_DIM + C_ROPE_DIM)
    nope, r1, r2 = w[..., :C_NOPE_DIM], w[..., C_NOPE_DIM:C_NOPE_DIM + half], w[..., C_NOPE_DIM + half:]
    z_tail = jnp.zeros((nl, r, C_HEADS, LANES - C_NOPE_DIM - C_ROPE_DIM), w.dtype)
    plain = jnp.concatenate([nope, r1, r2, z_tail], axis=-1).reshape(nl, r, C_HEADS * LANES)
    swapped = jnp.concatenate([jnp.zeros_like(nope), r2, r1, z_tail], axis=-1).reshape(nl, r, C_HEADS * LANES)
    return jnp.concatenate([plain, swapped], axis=-1).astype(BF16)


def _pack_w_ukv(w):
    nl, r, _ = w.shape
    w = w.reshape(nl, r, C_HEADS, C_NOPE_DIM + C_V_DIM)
    kn = jnp.concatenate([w[..., :C_NOPE_DIM], jnp.zeros((nl, r, C_HEADS, LANES - C_NOPE_DIM), w.dtype)], axis=-1)
    return jnp.concatenate([kn.reshape(nl, r, C_HEADS * LANES),
                            w[..., C_NOPE_DIM:].reshape(nl, r, C_HEADS * C_V_DIM)], axis=-1).astype(BF16)


def _pack_w_out(w_out):
    hd = HEAD_DIM
    return jnp.concatenate([w_out[:, 0:hd], w_out[:, 2 * hd:3 * hd], w_out[:, hd:2 * hd], w_out[:, 3 * hd:]],
                           axis=1).astype(BF16)


def _rope_tables(positions):
    half = C_ROPE_DIM // 2
    inv_freq = ROPE_THETA ** (-jnp.arange(half, dtype=jnp.float32) / half)
    zeros_lead = jnp.zeros((C_NOPE_DIM,), F32)
    zeros_tail = jnp.zeros((LANES - C_NOPE_DIM - C_ROPE_DIM,), F32)
    freq = jnp.concatenate([zeros_lead, inv_freq, inv_freq, zeros_tail])
    sign = jnp.concatenate([zeros_lead, -jnp.ones((half,), F32), jnp.ones((half,), F32), zeros_tail])
    keep = jnp.arange(LANES) < C_NOPE_DIM + C_ROPE_DIM
    ang = positions.astype(jnp.float32)[..., None] * freq
    cos_t = jnp.where(keep, jnp.cos(ang), 0.0)
    sin_t = jnp.sin(ang) * sign
    return cos_t, sin_t


def _swap_matrix():
    half = C_ROPE_DIM // 2
    idx = jnp.arange(LANES)
    src = jnp.where((idx >= C_NOPE_DIM) & (idx < C_NOPE_DIM + half), idx + half,
                    jnp.where((idx >= C_NOPE_DIM + half) & (idx < C_NOPE_DIM + C_ROPE_DIM), idx - half, idx))
    return (idx[:, None] == src[None, :]).astype(BF16)


def kernel(x, c, positions, rel_bias, w_mod, b_mod, norm_g, w_in, a_sink, b_conv, c_norm_q, c_norm_kv,
           c_w_uq, c_w_ukv, w_out, w_up, ffn_conv, w_down):
    b, s, d = x.shape
    depth = w_in.shape[0]

    a_order = jnp.array(A_HEAD_ORDER)
    bias_a = _band_bias(rel_bias[:, :A_Q_HEADS][:, a_order], A_RADIUS, 1, A_SUB_Q)
    bias_d = [_band_bias(rel_bias[:, A_Q_HEADS + i * D_HEADS:A_Q_HEADS + (i + 1) * D_HEADS], (w // 2) // dil, dil,
                         D_SUB_Q)
              for i, (w, dil) in enumerate(D_PATTERNS)]
    cos_t, sin_t = _rope_tables(positions)
    swap = _swap_matrix()

    w_in_p = _pack_w_in(w_in)
    w_uq_p = _pack_w_uq(c_w_uq)
    w_ukv_p = _pack_w_ukv(c_w_ukv)
    w_out_p = _pack_w_out(w_out)
    w_up_b = w_up.astype(BF16)
    w_down_b = w_down.astype(BF16)

    mod = _modulation(c, w_mod, b_mod)

    for l in range(depth):
        sh1, sc1, g1, sh2, sc2, g2 = (mod[l, k].reshape(b, 1, d) for k in range(6))
        gains = norm_g[l]

        qkv, qkv_d4, qkv_d16, qc, kc, vc = _inproj(x, gains[0:1], sc1, sh1, w_in_p, cos_t, sin_t, c_norm_q[l][None],
                                                   c_norm_kv[l][None], w_uq_p, w_ukv_p, swap, l)

        sink = jnp.broadcast_to(jnp.repeat(a_sink[l][a_order], A_SUB_Q)[:, None], (A_Q_HEADS * A_SUB_Q, LANES))
        oa = _band_attention(qkv, bias_a, sink, dil=1, radius=A_RADIUS, tq=BAND_ROWS, q_off=OFF_AQ, k_off=OFF_AK,
                             v_off=OFF_AV, kv_w=LANES, has_lse=False)[0].reshape(b, s, 2 * LANES)

        oc = _mla_attn(qc, kc, vc)

        od, lse = [], []
        for i, ((w, dil), src) in enumerate(zip(D_PATTERNS, (qkv, qkv_d4, qkv_d16))):
            off = OFF_D if dil == 1 else 0
            seq = s // dil
            o_i, l_i = _band_attention(src, bias_d[i], None, dil=dil, radius=(w // 2) // dil, tq=BAND_ROWS, q_off=off,
                                       k_off=off + 256, v_off=off + 512, kv_w=2 * LANES, has_lse=True,
                                       group=max(1, BAND_ROWS // seq))
            od.append(o_i)
            lse.append(l_i)

        x = _outproj(x, oa, qkv, oc, od, lse, w_out_p, b_conv, gains[1:2], g1, l)
        x = _ffn(x, gains[2:3], sc2, sh2, g2, gains[3:4], w_up_b, ffn_conv, w_down_b, l)
    return x
```
